```python
import jax, jax.numpy as jnp
from jax import lax
import numpy as np

D_MODEL = 1024
BATCH = 2
SEQ = 8192
DEPTH = 4
DEC_BATCH = 32
DEC_SEQ = 1
PAST_LEN = 8192
PAGE_SIZE = 128

HEAD_DIM = 64
H_FOX = D_MODEL // (2 * HEAD_DIM)
H_MOBA = D_MODEL // (2 * HEAD_DIM)
H_DIL = D_MODEL // HEAD_DIM
W_FOX = H_FOX * HEAD_DIM
W_MOBA = H_MOBA * HEAD_DIM
W_DIL = H_DIL * HEAD_DIM
EVEN_IN = 3 * W_FOX + H_FOX + 3 * W_MOBA
MOBA_BLOCK = 256
MOBA_TOPK = 3
MOBA_Q_BLOCK = 64
Q_BLOCK = 128
DIL_PAIRS = ((128, 1), (512, 4), (2048, 16))
DIL_MAX_WINDOW = 2048
N_MEM = 256
H_MEM = 4
HD_MEM = D_MODEL // H_MEM
D_FF = 2816
CONV_WIDTH = 3
ROPE_THETA = 10000.0
LN_EPS = 1e-5
FORGET_BIAS_INIT = 3.0
N_EVEN = (DEPTH + 1) // 2
N_ODD = DEPTH // 2
DN_ALPHA = (2 * DEPTH) ** 0.25
DN_BETA = (8 * DEPTH) ** -0.25
ATTN_SCALE = HEAD_DIM ** -0.5

kernel_name = 'fox_moba_dilated_mem_convffn_step'


def layer_norm(x, g, b):
    xf = x.astype(jnp.float32)
    mu = jnp.mean(xf, axis=-1, keepdims=True)
    var = jnp.mean(jnp.square(xf - mu), axis=-1, keepdims=True)
    return ((xf - mu) * lax.rsqrt(var + LN_EPS) * g + b).astype(x.dtype)


def rope(x, pos):
    half = x.shape[-1] // 2
    inv = ROPE_THETA ** (-jnp.arange(half, dtype=jnp.float32) / half)
    ang = pos.astype(jnp.float32)[:, None] * inv[None, :]
    cos = jnp.cos(ang)[None, :, None, :]
    sin = jnp.sin(ang)[None, :, None, :]
    xf = x.astype(jnp.float32)
    x1, x2 = xf[..., :half], xf[..., half:]
    return jnp.concatenate([x1 * cos - x2 * sin, x2 * cos + x1 * sin], axis=-1).astype(x.dtype)


def gather_pages(pool, layer, page_table):
    g = pool[layer, page_table]
    return g.reshape((g.shape[0], -1) + g.shape[3:])


def fox_core(q, c_q, q_pos, k, v, c_k, k_pos):
    s = jnp.einsum('nqhd,nkhd->nhqk', q, k, preferred_element_type=jnp.float32) * ATTN_SCALE
    s = s + jnp.swapaxes(c_q, 1, 2)[..., :, None] - jnp.swapaxes(c_k, 1, 2)[..., None, :]
    s = jnp.where((k_pos[None, :] <= q_pos[:, None])[None, None], s, -jnp.inf)
    p = jax.nn.softmax(s, axis=-1)
    return jnp.einsum('nhqk,nkhd->nqhd', p.astype(v.dtype), v)


def fox_prompt(q, k, v, logf, pos):
    B, S, H, dh = q.shape
    c = jnp.cumsum(logf, axis=1)
    nb = S // Q_BLOCK
    qb = q.reshape(B, nb, Q_BLOCK, H, dh).swapaxes(0, 1)
    cb = c.reshape(B, nb, Q_BLOCK, H).swapaxes(0, 1)
    pb = pos.reshape(nb, Q_BLOCK)
    out = lax.map(lambda a: fox_core(a[0], a[1], a[2], k, v, c, pos), (qb, cb, pb))
    return out.swapaxes(0, 1).reshape(B, S, H, dh)


def moba_blocks(k, v):
    N, L, H, dh = k.shape
    nb = -(-L // MOBA_BLOCK)
    pad = ((0, 0), (0, nb * MOBA_BLOCK - L), (0, 0), (0, 0))
    kb = jnp.pad(k, pad).reshape(N, nb, MOBA_BLOCK, H, dh).transpose(0, 3, 1, 2, 4)
    vb = jnp.pad(v, pad).reshape(N, nb, MOBA_BLOCK, H, dh).transpose(0, 3, 1, 2, 4)
    kmean = jnp.mean(kb.astype(jnp.float32), axis=3)
    return kb, vb, kmean


_gather_blocks = jax.vmap(jax.vmap(lambda blk, ix: blk[ix]))


def moba_core(q, q_pos, kblk, vblk, kmean):
    nb = kblk.shape[2]
    own = q_pos // MOBA_BLOCK
    past = jnp.arange(nb, dtype=jnp.int32)[None, :] < own[:, None]
    gate = jnp.einsum('nqhd,nhbd->nhqb', q.astype(jnp.float32), kmean)
    gate = jnp.where(past[None, None], gate, -jnp.inf)
    n_sel = min(MOBA_TOPK, nb)
    _, top = lax.top_k(gate, n_sel)
    top_ok = top < own[None, None, :, None]
    own_b = jnp.broadcast_to(own[None, None, :, None], top.shape[:3] + (1,)).astype(top.dtype)
    idx = jnp.concatenate([top, own_b], axis=-1)
    slot_ok = jnp.concatenate([top_ok, jnp.ones_like(top_ok[..., :1])], axis=-1)
    kg = _gather_blocks(kblk, idx)
    vg = _gather_blocks(vblk, idx)
    s = jnp.einsum('nqhd,nhqsbd->nhqsb', q, kg, preferred_element_type=jnp.float32) * ATTN_SCALE
    kpos = idx[..., None] * MOBA_BLOCK + jnp.arange(MOBA_BLOCK, dtype=jnp.int32)
    ok = slot_ok[..., None] & (kpos <= q_pos[None, None, :, None, None])
    s = jnp.where(ok, s, -jnp.inf)
    N, H, Tq = s.shape[:3]
    p = jax.nn.softmax(s.reshape(N, H, Tq, -1), axis=-1).reshape(s.shape)
    return jnp.einsum('nhqsb,nhqsbd->nqhd', p.astype(vg.dtype), vg)


def moba_prompt(q, k, v, pos):
    B, S, H, dh = q.shape
    kblk, vblk, kmean = moba_blocks(k, v)
    nb = S // MOBA_Q_BLOCK
    qb = q.reshape(B, nb, MOBA_Q_BLOCK, H, dh).swapaxes(0, 1)
    pb = pos.reshape(nb, MOBA_Q_BLOCK)
    out = lax.map(lambda a: moba_core(a[0], a[1], kblk, vblk, kmean), (qb, pb))
    return out.swapaxes(0, 1).reshape(B, S, H, dh)


def dilated_core(q, q_idx, kx, vx, start_pos):
    L = kx.shape[1]
    outs, lses = [], []
    for (win, dil) in DIL_PAIRS:
        dist = dil * jnp.arange(win // dil + 1, dtype=jnp.int32)
        ridx = q_idx[:, None] - dist[None, :]
        valid = (ridx >= 0) & (ridx + start_pos >= 0)
        ridx = jnp.clip(ridx, 0, L - 1)
        kg = kx[:, ridx]
        vg = vx[:, ridx]
        s = jnp.einsum('nqhd,nqjhd->nhqj', q, kg, preferred_element_type=jnp.float32) * ATTN_SCALE
        s = jnp.where(valid[None, None], s, -jnp.inf)
        lse = jax.nn.logsumexp(s, axis=-1)
        p = jnp.exp(s - lse[..., None])
        outs.append(jnp.einsum('nhqj,nqjhd->nqhd', p.astype(vg.dtype), vg))
        lses.append(lse)
    wts = jax.nn.softmax(jnp.stack(lses, axis=0), axis=0)
    out = outs[0] * jnp.swapaxes(wts[0], 1, 2)[..., None]
    for i in range(1, len(DIL_PAIRS)):
        out = out + outs[i] * jnp.swapaxes(wts[i], 1, 2)[..., None]
    return out.astype(vx.dtype)


def dilated_prompt(q, k, v):
    B, S, H, dh = q.shape
    pad = ((0, 0), (DIL_MAX_WINDOW, 0), (0, 0), (0, 0))
    kp, vp = jnp.pad(k, pad), jnp.pad(v, pad)
    nb = S // Q_BLOCK
    qb = q.reshape(B, nb, Q_BLOCK, H, dh).swapaxes(0, 1)
    q_idx = DIL_MAX_WINDOW + jnp.arange(Q_BLOCK, dtype=jnp.int32)
    span = DIL_MAX_WINDOW + Q_BLOCK

    def block(a):
        qi, i = a
        t0 = i * Q_BLOCK
        kx = lax.dynamic_slice_in_dim(kp, t0, span, axis=1)
        vx = lax.dynamic_slice_in_dim(vp, t0, span, axis=1)
        return dilated_core(qi, q_idx, kx, vx, t0 - DIL_MAX_WINDOW)

    out = lax.map(block, (qb, jnp.arange(nb, dtype=jnp.int32)))
    return out.swapaxes(0, 1).reshape(B, S, H, dh)


def even_project(x, pos, w_in, b_f):
    N, T, _ = x.shape
    z = x @ w_in
    cuts = [W_FOX, 2 * W_FOX, 3 * W_FOX, 3 * W_FOX + H_FOX,
            3 * W_FOX + H_FOX + W_MOBA, 3 * W_FOX + H_FOX + 2 * W_MOBA]
    qa, ka, va, fa, qb, kb, vb = jnp.split(z, cuts, axis=-1)
    qa, ka, va = (t.reshape(N, T, H_FOX, HEAD_DIM) for t in (qa, ka, va))
    qb, kb, vb = (t.reshape(N, T, H_MOBA, HEAD_DIM) for t in (qb, kb, vb))
    logf = jax.nn.log_sigmoid((fa + b_f).astype(jnp.float32))
    return qa, ka, va, logf, rope(qb, pos), rope(kb, pos), vb


def even_merge(oa, ob, w_out):
    N, T = oa.shape[:2]
    return jnp.concatenate([oa.reshape(N, T, W_FOX), ob.reshape(N, T, W_MOBA)], axis=-1) @ w_out


def odd_project(x, pos, w_in):
    N, T, _ = x.shape
    q, k, v = jnp.split(x @ w_in, 3, axis=-1)
    q, k, v = (t.reshape(N, T, H_DIL, HEAD_DIM) for t in (q, k, v))
    return rope(q, pos), rope(k, pos), v


def mem_attend(x, mk, mv, w_q, w_o):
    N, T, _ = x.shape
    q = (x @ w_q).reshape(N, T, H_MEM, HD_MEM)
    s = jnp.einsum('nqhd,nkhd->nhqk', q, mk, preferred_element_type=jnp.float32) * HD_MEM ** -0.5
    p = jax.nn.softmax(s, axis=-1)
    o = jnp.einsum('nhqk,nkhd->nqhd', p.astype(mv.dtype), mv)
    return o.reshape(N, T, D_MODEL) @ w_o


def conv_ffn(x, u_prev, w_in, conv_w, conv_b, w_out):
    T = x.shape[1]
    u = x @ w_in
    ue = jnp.concatenate([u_prev.astype(u.dtype), u], axis=1)
    h = conv_b
    for j in range(CONV_WIDTH):
        h = h + conv_w[j] * ue[:, j:j + T]
    a, g = jnp.split(h, 2, axis=-1)
    return (a * jax.nn.gelu(g, approximate=False)) @ w_out, ue[:, T:]


def setup_inputs(seed: int = 0) -> dict:
    key = jax.random.key(seed)
    ks = jax.random.split(key, 32)
    f32 = jnp.float32
    n_pages = PAST_LEN // PAGE_SIZE
    n_used = DEC_BATCH * n_pages
    n_pool = n_used + max(1, n_used // 4)
    l_win = min(DIL_MAX_WINDOW, PAST_LEN)

    def nrm(i, shape, scale=1.0):
        return jax.random.normal(ks[i], shape, f32) * scale

    page_table = jax.random.permutation(ks[3], n_pool)[:n_used].reshape(DEC_BATCH, n_pages).astype(jnp.int32)
    return {
        'x_prompt': nrm(0, (BATCH, SEQ, D_MODEL)),
        'x_sample': nrm(1, (DEC_BATCH, DEC_SEQ, D_MODEL)),
        'mem_prompt': nrm(2, (BATCH, N_MEM, D_MODEL)),
        'page_table': page_table,
        'cache_fox_k': nrm(4, (N_EVEN, n_pool, PAGE_SIZE, H_FOX, HEAD_DIM)),
        'cache_fox_v': nrm(5, (N_EVEN, n_pool, PAGE_SIZE, H_FOX, HEAD_DIM)),
        'cache_fox_logf': jax.nn.log_sigmoid(FORGET_BIAS_INIT + nrm(6, (N_EVEN, n_pool, PAGE_SIZE, H_FOX))),
        'cache_moba_k': nrm(7, (N_EVEN, n_pool, PAGE_SIZE, H_MOBA, HEAD_DIM)),
        'cache_moba_v': nrm(8, (N_EVEN, n_pool, PAGE_SIZE, H_MOBA, HEAD_DIM)),
        'cache_dwin_k': nrm(9, (N_ODD, DEC_BATCH, l_win, H_DIL, HEAD_DIM)),
        'cache_dwin_v': nrm(10, (N_ODD, DEC_BATCH, l_win, H_DIL, HEAD_DIM)),
        'cache_mem_k': nrm(11, (DEPTH, DEC_BATCH, N_MEM, H_MEM, HD_MEM)),
        'cache_mem_v': nrm(12, (DEPTH, DEC_BATCH, N_MEM, H_MEM, HD_MEM)),
        'state_conv': nrm(13, (DEPTH, DEC_BATCH, CONV_WIDTH - 1, 2 * D_FF)),
        'w_in_even': nrm(14, (N_EVEN, D_MODEL, EVEN_IN), D_MODEL ** -0.5),
        'b_forget': FORGET_BIAS_INIT + nrm(15, (N_EVEN, H_FOX), 0.1),
        'w_out_even': nrm(16, (N_EVEN, W_FOX + W_MOBA, D_MODEL), DN_BETA * (W_FOX + W_MOBA) ** -0.5),
        'w_in_odd': nrm(17, (N_ODD, D_MODEL, 3 * W_DIL), D_MODEL ** -0.5),
        'w_out_odd': nrm(18, (N_ODD, W_DIL, D_MODEL), DN_BETA * W_DIL ** -0.5),
        'w_mem_q': nrm(19, (DEPTH, D_MODEL, D_MODEL), D_MODEL ** -0.5),
        'w_mem_k': nrm(20, (DEPTH, D_MODEL, D_MODEL), D_MODEL ** -0.5),
        'w_mem_v': nrm(21, (DEPTH, D_MODEL, D_MODEL), D_MODEL ** -0.5),
        'w_mem_o': nrm(22, (DEPTH, D_MODEL, D_MODEL), DN_BETA * D_MODEL ** -0.5),
        'w_ffn_in': nrm(23, (DEPTH, D_MODEL, 2 * D_FF), D_MODEL ** -0.5),
        'ffn_conv_w': nrm(24, (DEPTH, CONV_WIDTH, 2 * D_FF), CONV_WIDTH ** -0.5),
        'ffn_conv_b': nrm(25, (DEPTH, 2 * D_FF), 0.02),
        'w_ffn_out': nrm(26, (DEPTH, D_FF, D_MODEL), DN_BETA * D_FF ** -0.5),
        'ln_g': 1.0 + nrm(27, (DEPTH, 3, D_MODEL), 0.02),
        'ln_b': nrm(28, (DEPTH, 3, D_MODEL), 0.02),
    }


def reference(x_prompt, x_sample, mem_prompt, page_table,
              cache_fox_k, cache_fox_v, cache_fox_logf, cache_moba_k, cache_moba_v,
              cache_dwin_k, cache_dwin_v, cache_mem_k, cache_mem_v, state_conv,
              w_in_even, b_forget, w_out_even, w_in_odd, w_out_odd,
              w_mem_q, w_mem_k, w_mem_v, w_mem_o,
              w_ffn_in, ffn_conv_w, ffn_conv_b, w_ffn_out, ln_g, ln_b):
    B, S, _ = x_prompt.shape
    NS, T, _ = x_sample.shape
    P = page_table.shape[1] * PAGE_SIZE
    L_C = cache_dwin_k.shape[2]
    pos_p = jnp.arange(S, dtype=jnp.int32)
    pos_s = P + jnp.arange(T, dtype=jnp.int32)
    kpos_s = jnp.arange(P + T, dtype=jnp.int32)
    hp, hs = x_prompt, x_sample
    fox_k_p, fox_v_p, fox_l_p, moba_k_p, moba_v_p = [], [], [], [], []
    fox_k_s, fox_v_s, fox_l_s, moba_k_s, moba_v_s = [], [], [], [], []
    dwin_k_p, dwin_v_p, dwin_k_s, dwin_v_s = [], [], [], []
    mem_k_p, mem_v_p, conv_p, conv_s = [], [], [], []
    for l in range(DEPTH):
        if l % 2 == 0:
            e = l // 2
            qa, ka, va, lf, qb, kb, vb = even_project(hp, pos_p, w_in_even[e], b_forget[e])
            oa = fox_prompt(qa, ka, va, lf, pos_p)
            ob = moba_prompt(qb, kb, vb, pos_p)
            mix_p = even_merge(oa, ob, w_out_even[e])
            fox_k_p.append(ka)
            fox_v_p.append(va)
            fox_l_p.append(lf)
            moba_k_p.append(kb)
            moba_v_p.append(vb)
            qa, ka, va, lf, qb, kb, vb = even_project(hs, pos_s, w_in_even[e], b_forget[e])
            ka_x = jnp.concatenate([gather_pages(cache_fox_k, e, page_table), ka], axis=1)
            va_x = jnp.concatenate([gather_pages(cache_fox_v, e, page_table), va], axis=1)
            lf_x = jnp.concatenate([gather_pages(cache_fox_logf, e, page_table).astype(jnp.float32), lf], axis=1)
            c_x = jnp.cumsum(lf_x, axis=1)
            oa = fox_core(qa, c_x[:, P:], pos_s, ka_x, va_x, c_x, kpos_s)
            kblk, vblk, kmean = moba_blocks(
                jnp.concatenate([gather_pages(cache_moba_k, e, page_table), kb], axis=1),
                jnp.concatenate([gather_pages(cache_moba_v, e, page_table), vb], axis=1))
            ob = moba_core(qb, pos_s, kblk, vblk, kmean)
            mix_s = even_merge(oa, ob, w_out_even[e])
            fox_k_s.append(ka)
            fox_v_s.append(va)
            fox_l_s.append(lf)
            moba_k_s.append(kb)
            moba_v_s.append(vb)
        else:
            od = l // 2
            q, k, v = odd_project(hp, pos_p, w_in_odd[od])
            mix_p = dilated_prompt(q, k, v).reshape(B, S, W_DIL) @ w_out_odd[od]
            keep = min(DIL_MAX_WINDOW, S)
            dwin_k_p.append(k[:, S - keep:])
            dwin_v_p.append(v[:, S - keep:])
            q, k, v = odd_project(hs, pos_s, w_in_odd[od])
            kx = jnp.concatenate([cache_dwin_k[od], k], axis=1)
            vx = jnp.concatenate([cache_dwin_v[od], v], axis=1)
            q_idx = L_C + jnp.arange(T, dtype=jnp.int32)
            mix_s = dilated_core(q, q_idx, kx, vx, P - L_C).reshape(NS, T, W_DIL) @ w_out_odd[od]
            dwin_k_s.append(k)
            dwin_v_s.append(v)
        hp = layer_norm(DN_ALPHA * hp + mix_p, ln_g[l, 0], ln_b[l, 0])
        hs = layer_norm(DN_ALPHA * hs + mix_s, ln_g[l, 0], ln_b[l, 0])
        mk = (mem_prompt @ w_mem_k[l]).reshape(B, N_MEM, H_MEM, HD_MEM)
        mv = (mem_prompt @ w_mem_v[l]).reshape(B, N_MEM, H_MEM, HD_MEM)
        mem_k_p.append(mk)
        mem_v_p.append(mv)
        hp = layer_norm(DN_ALPHA * hp + mem_attend(hp, mk, mv, w_mem_q[l], w_mem_o[l]), ln_g[l, 1], ln_b[l, 1])
        hs = layer_norm(DN_ALPHA * hs + mem_attend(hs, cache_mem_k[l], cache_mem_v[l], w_mem_q[l], w_mem_o[l]),
                        ln_g[l, 1], ln_b[l, 1])
        u0 = jnp.zeros((B, CONV_WIDTH - 1, 2 * D_FF), hp.dtype)
        f_p, cp = conv_ffn(hp, u0, w_ffn_in[l], ffn_conv_w[l], ffn_conv_b[l], w_ffn_out[l])
        f_s, cs = conv_ffn(hs, state_conv[l], w_ffn_in[l], ffn_conv_w[l], ffn_conv_b[l], w_ffn_out[l])
        conv_p.append(cp)
        conv_s.append(cs)
        hp = layer_norm(DN_ALPHA * hp + f_p, ln_g[l, 2], ln_b[l, 2])
        hs = layer_norm(DN_ALPHA * hs + f_s, ln_g[l, 2], ln_b[l, 2])
    st = jnp.stack
    return (hp, hs,
            st(fox_k_p), st(fox_v_p), st(fox_l_p), st(moba_k_p), st(moba_v_p),
            st(dwin_k_p), st(dwin_v_p), st(mem_k_p), st(mem_v_p), st(conv_p),
            st(fox_k_s), st(fox_v_s), st(fox_l_s), st(moba_k_s), st(moba_v_s),
            st(dwin_k_s), st(dwin_v_s), st(conv_s))
```

```python
import functools
import math

import jax
import jax.numpy as jnp
from jax import lax
from jax.experimental import pallas as pl
from jax.experimental.pallas import tpu as pltpu

F32 = jnp.float32
BF16 = jnp.bfloat16

LANES = 128

D_MODEL = 1024
HEAD_DIM = 64
HEADS_PER_VREG = LANES // HEAD_DIM
N_MEM_HEADS = 4
MEM_HEAD_DIM = D_MODEL // N_MEM_HEADS
D_FF = 2816
FFN_CHUNK = 256
MOBA_BLOCK = 256
MOBA_TOPK = 3
DIL_PAIRS = ((128, 1), (512, 4), (2048, 16))
DIL_KEYS = 128
ROPE_THETA = 10000.0
LN_EPS = 1e-5
DEPTH = 4
DN_ALPHA = (2 * DEPTH) ** 0.25
ATTN_SCALE = HEAD_DIM ** -0.5
MEM_SCALE = MEM_HEAD_DIM ** -0.5
NEG_BIG = -1e30
VMEM_LIMIT = 56 * 2 ** 20

NT_DIMS = (((1,), (1,)), ((), ()))
HIGHEST = lax.Precision.HIGHEST


def _params(n_axes):
    return pltpu.CompilerParams(dimension_semantics=("arbitrary",) * n_axes,
                                vmem_limit_bytes=VMEM_LIMIT)


def _const_spec(shape):
    zeros = (0,) * len(shape)
    return pl.BlockSpec(shape, lambda *_: zeros, pipeline_mode=pl.Buffered(1))


def _layer_norm(y, g, b):
    mu = jnp.mean(y, axis=-1, keepdims=True)
    yc = y - mu
    var = jnp.mean(yc * yc, axis=-1, keepdims=True)
    return yc * lax.rsqrt(var + LN_EPS) * g + b


def _rope_tables(pos):
    half = HEAD_DIM // 2
    inv = ROPE_THETA ** (-jnp.arange(half, dtype=F32) / half)
    ang = pos.astype(F32)[:, None] * inv[None, :]
    cos, sin = jnp.cos(ang), jnp.sin(ang)
    cos2 = jnp.tile(jnp.concatenate([cos, cos], axis=-1), (1, HEADS_PER_VREG))
    sin2 = jnp.tile(jnp.concatenate([-sin, sin], axis=-1), (1, HEADS_PER_VREG))
    return cos2, sin2


def _proj_kernel(x_ref, w_ref, o_ref):
    o_ref[...] = jnp.dot(x_ref[...].astype(BF16), w_ref[...], preferred_element_type=F32)


def _proj_rope_kernel(x_ref, w_ref, cos_ref, sin_ref, o_ref):
    z = jnp.dot(x_ref[...].astype(BF16), w_ref[...], preferred_element_type=F32)
    tm, tn = z.shape
    cos, sin = cos_ref[...], sin_ref[...]
    lane = lax.broadcasted_iota(jnp.int32, (tm, LANES), 1)
    first_half = (lane & (HEAD_DIM - 1)) < HEAD_DIM // 2
    for c in range(tn // LANES):
        zc = z[:, c * LANES:(c + 1) * LANES]
        partner = jnp.where(first_half, pltpu.roll(zc, LANES - HEAD_DIM // 2, 1),
                            pltpu.roll(zc, HEAD_DIM // 2, 1))
        o_ref[:, c * LANES:(c + 1) * LANES] = zc * cos + partner * sin


def _proj(x, w, tm, tn, rope=None):
    M, K = x.shape
    N = w.shape[1]
    grid = (M // tm, N // tn)
    x_spec = pl.BlockSpec((tm, K), lambda i, j: (i, 0))
    w_spec = pl.BlockSpec((K, tn), lambda i, j: (0, j))
    o_spec = pl.BlockSpec((tm, tn), lambda i, j: (i, j))
    out_shape = jax.ShapeDtypeStruct((M, N), F32)
    if rope is None:
        return pl.pallas_call(_proj_kernel, grid=grid, in_specs=[x_spec, w_spec], out_specs=o_spec,
                              out_shape=out_shape, compiler_params=_params(2), name="proj")(x, w)
    cos2, sin2 = rope
    pos_blocks = cos2.shape[0] // tm
    t_spec = pl.BlockSpec((tm, LANES), lambda i, j: (i % pos_blocks, 0))
    return pl.pallas_call(_proj_rope_kernel, grid=grid, in_specs=[x_spec, w_spec, t_spec, t_spec],
                          out_specs=o_spec, out_shape=out_shape, compiler_params=_params(2),
                          name="proj_rope")(x, w, cos2, sin2)


def _forget_kernel(x_ref, w_ref, b_ref, lf_ref, c_ref, carry_ref, *, tiles_per_seq):
    i = pl.program_id(0)
    z = jnp.dot(x_ref[...].astype(BF16), w_ref[...], preferred_element_type=F32) + b_ref[...]
    lf = -(jnp.maximum(-z, 0.0) + jnp.log1p(jnp.exp(-jnp.abs(z))))
    lf_ref[...] = lf

    @pl.when(i % tiles_per_seq == 0)
    def _():
        carry_ref[...] = jnp.zeros_like(carry_ref)

    tm = lf.shape[0]
    row = lax.broadcasted_iota(jnp.int32, (tm, tm), 0)
    col = lax.broadcasted_iota(jnp.int32, (tm, tm), 1)
    tri = jnp.where(col <= row, 1.0, 0.0).astype(F32)
    c = jnp.dot(tri, lf, precision=HIGHEST, preferred_element_type=F32) + carry_ref[0:1, :]
    c_ref[...] = c
    carry_ref[0:1, :] = c[tm - 1:tm, :]


def _forget(x, w_f, b_f, tm, seq_len):
    M, K = x.shape
    n_heads = w_f.shape[1]
    w = jnp.zeros((K, LANES), BF16).at[:, :n_heads].set(w_f.astype(BF16))
    b = jnp.zeros((1, LANES), F32).at[0, :n_heads].set(b_f)
    kern = functools.partial(_forget_kernel, tiles_per_seq=seq_len // tm)
    out = jax.ShapeDtypeStruct((M, LANES), F32)
    return pl.pallas_call(
        kern, grid=(M // tm,),
        in_specs=[pl.BlockSpec((tm, K), lambda i: (i, 0)), _const_spec((K, LANES)), _const_spec((1, LANES))],
        out_specs=[pl.BlockSpec((tm, LANES), lambda i: (i, 0))] * 2,
        out_shape=[out, out], scratch_shapes=[pltpu.VMEM((8, LANES), F32)],
        compiler_params=_params(1), name="forget")(x, w, b)


def _out_ln_kernel(*refs, n_in):
    a_refs, w_refs = refs[:n_in], refs[n_in:2 * n_in]
    x_ref, g_ref, b_ref, o_ref = refs[2 * n_in:]
    acc = None
    for a_ref, w_ref in zip(a_refs, w_refs):
        d = jnp.dot(a_ref[...].astype(BF16), w_ref[...], preferred_element_type=F32)
        acc = d if acc is None else acc + d
    o_ref[...] = _layer_norm(DN_ALPHA * x_ref[...] + acc, g_ref[...], b_ref[...])


def _out_ln(a_list, w_list, x, g, b, tm):
    M, D = x.shape
    n_in = len(a_list)
    in_specs = [pl.BlockSpec((tm, a.shape[1]), lambda i: (i, 0)) for a in a_list]
    in_specs += [_const_spec(w.shape) for w in w_list]
    in_specs += [pl.BlockSpec((tm, D), lambda i: (i, 0)), _const_spec((1, D)), _const_spec((1, D))]
    return pl.pallas_call(
        functools.partial(_out_ln_kernel, n_in=n_in), grid=(M // tm,), in_specs=in_specs,
        out_specs=pl.BlockSpec((tm, D), lambda i: (i, 0)), out_shape=jax.ShapeDtypeStruct((M, D), F32),
        compiler_params=_params(1), name="out_ln")(*a_list, *w_list, x, g.reshape(1, D), b.reshape(1, D))


def _dil_merge_ln_kernel(o1, o2, o3, e1, e2, e3, w_ref, x_ref, g_ref, b_ref, o_ref):
    es = [e1[...], e2[...], e3[...]]
    m = jnp.maximum(jnp.maximum(es[0], es[1]), es[2])
    ws = [jnp.exp(e - m) for e in es]
    num = ws[0] * o1[...] + ws[1] * o2[...] + ws[2] * o3[...]
    a = num / (ws[0] + ws[1] + ws[2])
    mix = jnp.dot(a.astype(BF16), w_ref[...], preferred_element_type=F32)
    o_ref[...] = _layer_norm(DN_ALPHA * x_ref[...] + mix, g_ref[...], b_ref[...])


def _dil_merge_ln(outs, lses, w, x, g, b, tm):
    M, D = x.shape
    row = pl.BlockSpec((tm, D), lambda i: (i, 0))
    return pl.pallas_call(
        _dil_merge_ln_kernel, grid=(M // tm,),
        in_specs=[row] * 6 + [_const_spec(w.shape), row, _const_spec((1, D)), _const_spec((1, D))],
        out_specs=row, out_shape=jax.ShapeDtypeStruct((M, D), F32),
        compiler_params=_params(1), name="dil_merge_ln")(*outs, *lses, w, x, g.reshape(1, D), b.reshape(1, D))


def _mem_kernel(x_ref, wq_ref, mk_ref, mv_ref, wo_ref, g_ref, b_ref, o_ref):
    x = x_ref[...]
    q = jnp.dot(x.astype(BF16), wq_ref[...], preferred_element_type=F32)
    mk = mk_ref[0].astype(BF16)
    mv = mv_ref[0].astype(BF16)
    heads = []
    for h in range(N_MEM_HEADS):
        sl = slice(h * MEM_HEAD_DIM, (h + 1) * MEM_HEAD_DIM)
        qh = (q[:, sl] * MEM_SCALE).astype(BF16)
        s = lax.dot_general(qh, mk[:, sl], NT_DIMS, preferred_element_type=F32)
        e = jnp.exp(s - jnp.max(s, axis=-1, keepdims=True))
        l = jnp.sum(e, axis=-1, keepdims=True)
        heads.append(jnp.dot(e.astype(BF16), mv[:, sl], preferred_element_type=F32) / l)
    o = jnp.concatenate(heads, axis=-1)
    mix = jnp.dot(o.astype(BF16), wo_ref[...], preferred_element_type=F32)
    o_ref[...] = _layer_norm(DN_ALPHA * x + mix, g_ref[...], b_ref[...])


def _mem_block(x, wq, mk, mv, wo, g, b, tm, seq_len):
    M, D = x.shape
    n_mem = mk.shape[1]
    tiles_per_seq = seq_len // tm
    row = pl.BlockSpec((tm, D), lambda i: (i, 0))
    mem = pl.BlockSpec((1, n_mem, D), lambda i: (i // tiles_per_seq, 0, 0))
    return pl.pallas_call(
        _mem_kernel, grid=(M // tm,),
        in_specs=[row, _const_spec((D, D)), mem, mem, _const_spec((D, D)),
                  _const_spec((1, D)), _const_spec((1, D))],
        out_specs=row, out_shape=jax.ShapeDtypeStruct((M, D), F32),
        compiler_params=_params(1), name="mem_block")(x, wq, mk, mv, wo, g.reshape(1, D), b.reshape(1, D))


def _gated(ha, hg):
    return ha * (0.5 * hg * (1.0 + lax.erf(hg * (2.0 ** -0.5))))


def _ffn_prompt_kernel(x_ref, win_ref, cw_ref, cb_ref, wout_ref, g_ref, b_ref, o_ref, st_ref, carry_ref,
                       *, tiles_per_seq):
    i = pl.program_id(0)

    @pl.when(i % tiles_per_seq == 0)
    def _():
        carry_ref[...] = jnp.zeros_like(carry_ref)

    x = x_ref[...]
    xb = x.astype(BF16)
    tm = x.shape[0]
    row = lax.broadcasted_iota(jnp.int32, (tm, FFN_CHUNK), 0)
    acc = jnp.zeros((tm, D_MODEL), F32)
    for c in range(D_FF // FFN_CHUNK):
        hs = []
        for off in (c * FFN_CHUNK, D_FF + c * FFN_CHUNK):
            sl = slice(off, off + FFN_CHUNK)
            u = jnp.dot(xb, win_ref[:, sl], preferred_element_type=F32)
            prev2 = carry_ref[0:1, sl]
            prev1 = carry_ref[1:2, sl]
            u1 = jnp.where(row == 0, prev1, pltpu.roll(u, 1, 0))
            u2 = jnp.where(row == 0, prev2, jnp.where(row == 1, prev1, pltpu.roll(u, 2, 0)))
            last = u[tm - 2:tm, :]
            carry_ref[0:2, sl] = last
            st_ref[0, :, sl] = last
            hs.append(cb_ref[:, sl] + cw_ref[0:1, sl] * u2 + cw_ref[1:2, sl] * u1 + cw_ref[2:3, sl] * u)
        act = _gated(hs[0], hs[1])
        acc = acc + jnp.dot(act.astype(BF16), wout_ref[c * FFN_CHUNK:(c + 1) * FFN_CHUNK, :],
                            preferred_element_type=F32)
    o_ref[...] = _layer_norm(DN_ALPHA * x + acc, g_ref[...], b_ref[...])


def _ffn_prompt(x, win, cw, cb, wout, g, b, tm, seq_len):
    M, D = x.shape
    F2 = win.shape[1]
    tiles_per_seq = seq_len // tm
    row = pl.BlockSpec((tm, D), lambda i: (i, 0))
    return pl.pallas_call(
        functools.partial(_ffn_prompt_kernel, tiles_per_seq=tiles_per_seq), grid=(M // tm,),
        in_specs=[row, _const_spec((D, F2)), _const_spec((3, F2)), _const_spec((1, F2)),
                  _const_spec((F2 // 2, D)), _const_spec((1, D)), _const_spec((1, D))],
        out_specs=[row, pl.BlockSpec((1, 2, F2), lambda i: (i // tiles_per_seq, 0, 0))],
        out_shape=[jax.ShapeDtypeStruct((M, D), F32), jax.ShapeDtypeStruct((M // seq_len, 2, F2), F32)],
        scratch_shapes=[pltpu.VMEM((8, F2), F32)],
        compiler_params=_params(1), name="ffn_prompt")(x, win, cw, cb.reshape(1, F2), wout,
                                                       g.reshape(1, D), b.reshape(1, D))


def _ffn_sample_kernel(x_ref, s0_ref, s1_ref, win_ref, cw_ref, cb_ref, wout_ref, g_ref, b_ref, o_ref, u_ref):
    x = x_ref[...]
    xb = x.astype(BF16)
    acc = jnp.zeros(x.shape, F32)
    for c in range(D_FF // FFN_CHUNK):
        hs = []
        for off in (c * FFN_CHUNK, D_FF + c * FFN_CHUNK):
            sl = slice(off, off + FFN_CHUNK)
            u = jnp.dot(xb, win_ref[:, sl], preferred_element_type=F32)
            u_ref[:, sl] = u
            hs.append(cb_ref[:, sl] + cw_ref[0:1, sl] * s0_ref[:, sl] + cw_ref[1:2, sl] * s1_ref[:, sl]
                      + cw_ref[2:3, sl] * u)
        act = _gated(hs[0], hs[1])
        acc = acc + jnp.dot(act.astype(BF16), wout_ref[c * FFN_CHUNK:(c + 1) * FFN_CHUNK, :],
                            preferred_element_type=F32)
    o_ref[...] = _layer_norm(DN_ALPHA * x + acc, g_ref[...], b_ref[...])


def _ffn_sample(x, s0, s1, win, cw, cb, wout, g, b):
    M, D = x.shape
    F2 = win.shape[1]
    return pl.pallas_call(
        _ffn_sample_kernel, grid=(1,),
        in_specs=[_const_spec((M, D)), _const_spec((M, F2)), _const_spec((M, F2)), _const_spec((D, F2)),
                  _const_spec((3, F2)), _const_spec((1, F2)), _const_spec((F2 // 2, D)),
                  _const_spec((1, D)), _const_spec((1, D))],
        out_specs=[pl.BlockSpec((M, D), lambda i: (0, 0)), pl.BlockSpec((M, F2), lambda i: (0, 0))],
        out_shape=[jax.ShapeDtypeStruct((M, D), F32), jax.ShapeDtypeStruct((M, F2), F32)],
        compiler_params=_params(1), name="ffn_sample")(x, s0, s1, win, cw, cb.reshape(1, F2), wout,
                                                       g.reshape(1, D), b.reshape(1, D))


def _head_masks(tq):
    lane = lax.broadcasted_iota(jnp.int32, (tq, LANES), 1)
    return lane, lane < HEAD_DIM


def _online_update(state, s, vc):
    m, l, acc = state
    m_new = jnp.maximum(m, jnp.max(s, axis=-1, keepdims=True))
    alpha = jnp.exp(m - m_new)
    p = jnp.exp(s - m_new)
    l = alpha * l + jnp.sum(p, axis=-1, keepdims=True)
    acc = alpha * acc + jnp.dot(p.astype(BF16), vc, preferred_element_type=F32)
    return m_new, l, acc


def _init_state(tq):
    return (jnp.full((tq, 1), -jnp.inf, F32), jnp.zeros((tq, 1), F32), jnp.zeros((tq, LANES), F32))


def _fox_kernel(q_ref, k_ref, v_ref, ccol_ref, crow_ref, o_ref, kb_ref, vb_ref, *, tq):
    qi = pl.program_id(2)

    @pl.when(qi == 0)
    def _():
        kb_ref[...] = k_ref[0].astype(BF16)
        vb_ref[...] = v_ref[0].astype(BF16)

    _, is_h0 = _head_masks(tq)
    q = q_ref[0] * ATTN_SCALE
    qs = (jnp.where(is_h0, q, 0.0).astype(BF16), jnp.where(is_h0, 0.0, q).astype(BF16))
    ccol = ccol_ref[0, 0]
    cq = (ccol[:, 0:1], ccol[:, 1:2])
    causal = (lax.broadcasted_iota(jnp.int32, (tq, tq), 1) <= lax.broadcasted_iota(jnp.int32, (tq, tq), 0))

    def step(j, states, diagonal):
        ks = pl.multiple_of(j * tq, tq)
        kc = kb_ref[pl.ds(ks, tq), :]
        vc = vb_ref[pl.ds(ks, tq), :]
        new = []
        for h in range(HEADS_PER_VREG):
            s = lax.dot_general(qs[h], kc, NT_DIMS, preferred_element_type=F32)
            s = s + cq[h] - crow_ref[0, 0, h:h + 1, pl.ds(ks, tq)]
            if diagonal:
                s = jnp.where(causal, s, -jnp.inf)
            new.append(_online_update(states[h], s, vc))
        return tuple(new)

    states = lax.fori_loop(0, qi, lambda j, st: step(j, st, False), (_init_state(tq), _init_state(tq)))
    (_, l0, a0), (_, l1, a1) = step(qi, states, True)
    o_ref[0] = jnp.where(is_h0, a0 / l0, a1 / l1)


def _fox_prompt(q, k, v, c, tq):
    B, S, W = q.shape
    n_pairs = W // LANES
    c4 = c.reshape(B, S, n_pairs, HEADS_PER_VREG)
    ccol = c4.transpose(0, 2, 1, 3)
    crow = c4.transpose(0, 2, 3, 1)
    q_spec = pl.BlockSpec((1, tq, LANES), lambda b, p, i: (b, i, p))
    kv_spec = pl.BlockSpec((1, S, LANES), lambda b, p, i: (b, 0, p))
    return pl.pallas_call(
        functools.partial(_fox_kernel, tq=tq), grid=(B, n_pairs, S // tq),
        in_specs=[q_spec, kv_spec, kv_spec,
                  pl.BlockSpec((1, 1, tq, HEADS_PER_VREG), lambda b, p, i: (b, p, i, 0)),
                  pl.BlockSpec((1, 1, HEADS_PER_VREG, S), lambda b, p, i: (b, p, 0, 0))],
        out_specs=q_spec, out_shape=jax.ShapeDtypeStruct((B, S, W), F32),
        scratch_shapes=[pltpu.VMEM((S, LANES), BF16), pltpu.VMEM((S, LANES), BF16)],
        compiler_params=_params(3), name="fox_prompt")(q, k, v, ccol, crow)


def _moba_kernel(q_ref, k_ref, v_ref, o_ref, ka0_ref, ka1_ref, vb_ref, kmp_ref, *, n_blocks):
    tq = MOBA_BLOCK
    qi = pl.program_id(2)
    lane, is_h0 = _head_masks(tq)
    blk_lane = lane & (HEAD_DIM - 1)
    lane_f = lane.astype(F32)

    @pl.when(qi == 0)
    def _():
        kmp_ref[...] = jnp.zeros_like(kmp_ref)

        def fill(j, _):
            rs = pl.multiple_of(j * tq, tq)
            kc = k_ref[0, pl.ds(rs, tq), :]
            onehot = jnp.where(blk_lane == j, 1.0, 0.0)
            ka0_ref[pl.ds(rs, tq), :] = jnp.where(is_h0, kc, onehot).astype(BF16)
            ka1_ref[pl.ds(rs, tq), :] = jnp.where(is_h0, onehot, kc).astype(BF16)
            vb_ref[pl.ds(rs, tq), :] = v_ref[0, pl.ds(rs, tq), :].astype(BF16)
            kmean = jnp.sum(kc, axis=0, keepdims=True) * (1.0 / tq)
            h0row = is_h0[0:1, :]
            kmp_ref[pl.ds(HEAD_DIM + j, 1), :] = jnp.where(h0row, kmean, 0.0)
            kmp_ref[pl.ds(j, 1), :] = jnp.where(h0row, 0.0, kmean)
            return 0

        lax.fori_loop(0, n_blocks, fill, 0)

    q = q_ref[0]
    gate = lax.dot_general(q, kmp_ref[...], NT_DIMS, precision=HIGHEST, preferred_element_type=F32)
    qsc = q * ATTN_SCALE
    q_aug, q_own = [], []
    for h, in_head in enumerate((is_h0, jnp.logical_not(is_h0))):
        spare = jnp.logical_not(in_head)
        valid = spare & (blk_lane < qi)
        g = jnp.where(valid, gate, -jnp.inf)
        sel = jnp.zeros((tq, LANES), jnp.bool_)
        for _ in range(MOBA_TOPK):
            mx = jnp.max(g, axis=-1, keepdims=True)
            first = jnp.min(jnp.where(g == mx, lane_f, float(LANES)), axis=-1, keepdims=True)
            pick = (lane_f == first) & valid
            sel = sel | pick
            g = jnp.where(pick, -jnp.inf, g)
        bias = jnp.where(sel, 0.0, NEG_BIG)
        q_aug.append(jnp.where(in_head, qsc, bias).astype(BF16))
        q_own.append(jnp.where(in_head, qsc, 0.0).astype(BF16))
    ka_refs = (ka0_ref, ka1_ref)

    def past(j, states):
        ks = pl.multiple_of(j * tq, tq)
        vc = vb_ref[pl.ds(ks, tq), :]
        new = []
        for h in range(HEADS_PER_VREG):
            s = lax.dot_general(q_aug[h], ka_refs[h][pl.ds(ks, tq), :], NT_DIMS, preferred_element_type=F32)
            new.append(_online_update(states[h], s, vc))
        return tuple(new)

    states = lax.fori_loop(0, qi, past, (_init_state(tq), _init_state(tq)))
    ks = pl.multiple_of(qi * tq, tq)
    vc = vb_ref[pl.ds(ks, tq), :]
    causal = (lax.broadcasted_iota(jnp.int32, (tq, tq), 1) <= lax.broadcasted_iota(jnp.int32, (tq, tq), 0))
    fin = []
    for h in range(HEADS_PER_VREG):
        s = lax.dot_general(q_own[h], ka_refs[h][pl.ds(ks, tq), :], NT_DIMS, preferred_element_type=F32)
        fin.append(_online_update(states[h], jnp.where(causal, s, -jnp.inf), vc))
    (_, l0, a0), (_, l1, a1) = fin
    o_ref[0] = jnp.where(is_h0, a0 / l0, a1 / l1)


def _moba_prompt(q, k, v):
    B, S, W = q.shape
    n_pairs = W // LANES
    n_blocks = S // MOBA_BLOCK
    assert S % MOBA_BLOCK == 0 and n_blocks <= HEAD_DIM
    q_spec = pl.BlockSpec((1, MOBA_BLOCK, LANES), lambda b, p, i: (b, i, p))
    kv_spec = pl.BlockSpec((1, S, LANES), lambda b, p, i: (b, 0, p))
    return pl.pallas_call(
        functools.partial(_moba_kernel, n_blocks=n_blocks), grid=(B, n_pairs, n_blocks),
        in_specs=[q_spec, kv_spec, kv_spec], out_specs=q_spec,
        out_shape=jax.ShapeDtypeStruct((B, S, W), F32),
        scratch_shapes=[pltpu.VMEM((S, LANES), BF16), pltpu.VMEM((S, LANES), BF16),
                        pltpu.VMEM((S, LANES), BF16), pltpu.VMEM((LANES, LANES), F32)],
        compiler_params=_params(3), name="moba_prompt")(q, k, v)


def _dil_kernel(q_ref, kp_ref, kc_ref, vp_ref, vc_ref, o_ref, e_ref, *, tq):
    t = pl.program_id(3)
    sub = DIL_KEYS
    _, is_h0 = _head_masks(sub)
    a = lax.broadcasted_iota(jnp.int32, (sub, 2 * sub), 0)
    c = lax.broadcasted_iota(jnp.int32, (sub, 2 * sub), 1)
    band = (c >= a) & (c <= a + sub)
    band_first = band & (c >= jnp.where(t > 0, 0, sub))
    for u in range(tq // sub):
        q = q_ref[0, u * sub:(u + 1) * sub, :] * ATTN_SCALE
        if u == 0:
            kk = jnp.concatenate([kp_ref[0], kc_ref[0, 0:sub, :]], axis=0)
            vv = jnp.concatenate([vp_ref[0], vc_ref[0, 0:sub, :]], axis=0)
            mask = band_first
        else:
            kk = kc_ref[0, (u - 1) * sub:(u + 1) * sub, :]
            vv = vc_ref[0, (u - 1) * sub:(u + 1) * sub, :]
            mask = band
        kk = kk.astype(BF16)
        vv = vv.astype(BF16)
        outs, lses = [], []
        for h in range(HEADS_PER_VREG):
            qh = jnp.where(is_h0, q, 0.0) if h == 0 else jnp.where(is_h0, 0.0, q)
            s = lax.dot_general(qh.astype(BF16), kk, NT_DIMS, preferred_element_type=F32)
            s = jnp.where(mask, s, -jnp.inf)
            m = jnp.max(s, axis=-1, keepdims=True)
            p = jnp.exp(s - m)
            l = jnp.sum(p, axis=-1, keepdims=True)
            outs.append(jnp.dot(p.astype(BF16), vv, preferred_element_type=F32) / l)
            lses.append(m + jnp.log(l))
        o_ref[0, u * sub:(u + 1) * sub, :] = jnp.where(is_h0, outs[0], outs[1])
        e_ref[0, u * sub:(u + 1) * sub, :] = jnp.where(is_h0, lses[0], lses[1])


def _dil_branch(q, k, v, dil, tq):
    B, S, W = q.shape
    n_pairs = W // LANES
    rows = S // dil
    qv, kv, vv = (t.reshape(B, rows, dil * W) for t in (q, k, v))
    ratio = tq // DIL_KEYS
    cur = pl.BlockSpec((1, tq, LANES), lambda b, r, p, t: (b, t, r * n_pairs + p))
    prev = pl.BlockSpec((1, DIL_KEYS, LANES),
                        lambda b, r, p, t: (b, jnp.maximum(t * ratio - 1, 0), r * n_pairs + p))
    shape = jax.ShapeDtypeStruct((B, rows, dil * W), F32)
    o, e = pl.pallas_call(
        functools.partial(_dil_kernel, tq=tq), grid=(B, dil, n_pairs, rows // tq),
        in_specs=[cur, prev, cur, prev, cur], out_specs=[cur, cur], out_shape=[shape, shape],
        compiler_params=_params(4), name="dil_prompt")(qv, kv, kv, vv, vv)
    return o.reshape(B, S, W), e.reshape(B, S, W)


def _segment_matrix(width, seg, dtype):
    r = jnp.arange(width) // seg
    return (r[:, None] == r[None, :]).astype(dtype)


def _scores(k, qs, seg_ref):
    return jnp.dot((k * qs).astype(BF16), seg_ref[...], preferred_element_type=F32)


def _dec_attn_kernel(*refs, n_blocks, new_mult, scale):
    q_ref, seg_ref = refs[0], refs[1]
    k_refs = refs[2:2 + n_blocks]
    v_refs = refs[2 + n_blocks:2 + 2 * n_blocks]
    rest = refs[2 + 2 * n_blocks:]
    qs = q_ref[0] * scale
    ss = [_scores(k_ref[0], qs, seg_ref) for k_ref in k_refs]
    m = functools.reduce(jnp.maximum, [jnp.max(s, axis=0, keepdims=True) for s in ss])
    if new_mult:
        kn_ref, vn_ref, o_ref = rest
        s_new = _scores(jnp.broadcast_to(kn_ref[0], (8, qs.shape[1])), qs, seg_ref)[0:1]
        m = jnp.maximum(m, s_new)
        p_new = new_mult * jnp.exp(s_new - m)
        l, acc = p_new, p_new * vn_ref[0]
    else:
        (o_ref,) = rest
        l, acc = 0.0, 0.0
    for s, v_ref in zip(ss, v_refs):
        p = jnp.exp(s - m)
        l = l + jnp.sum(p, axis=0, keepdims=True)
        acc = acc + jnp.sum(p * v_ref[0], axis=0, keepdims=True)
    o_ref[0] = acc / l


def _mem_sample(q, mk, mv):
    NS, D = q.shape
    n_mem = mk.shape[1]
    seg = _segment_matrix(D, MEM_HEAD_DIM, BF16)
    vec = pl.BlockSpec((1, 1, D), lambda n: (n, 0, 0))
    blk = pl.BlockSpec((1, n_mem, D), lambda n: (n, 0, 0))
    out = pl.pallas_call(
        functools.partial(_dec_attn_kernel, n_blocks=1, new_mult=0, scale=MEM_SCALE), grid=(NS,),
        in_specs=[vec, _const_spec((D, D)), blk, blk], out_specs=vec,
        out_shape=jax.ShapeDtypeStruct((NS, 1, D), F32),
        compiler_params=_params(1), name="mem_sample")(q.reshape(NS, 1, D), seg, mk, mv)
    return out.reshape(NS, D)


def _dil_sample(q, k_new, v_new, ck, cv):
    NS, L, W = ck.shape
    assert all(win == DIL_KEYS * dil and win <= L and L % dil == 0 for win, dil in DIL_PAIRS)
    seg = _segment_matrix(W, HEAD_DIM, BF16)
    vec = pl.BlockSpec((1, 1, W), lambda n: (n, 0, 0))
    k_views, v_views, specs = [], [], []
    for _, dil in DIL_PAIRS:
        rows = L // dil
        spec = pl.BlockSpec((1, DIL_KEYS, W), lambda n, rows=rows: (n, rows // DIL_KEYS - 1, 0))
        specs.append(spec)
        k_views.append(ck.reshape(NS, rows, dil * W))
        v_views.append(cv.reshape(NS, rows, dil * W))
    nb = len(DIL_PAIRS)
    out = pl.pallas_call(
        functools.partial(_dec_attn_kernel, n_blocks=nb, new_mult=nb, scale=ATTN_SCALE), grid=(NS,),
        in_specs=[vec, _const_spec((W, W))] + specs + specs + [vec, vec], out_specs=vec,
        out_shape=jax.ShapeDtypeStruct((NS, 1, W), F32),
        compiler_params=_params(1), name="dil_sample")(
            q.reshape(NS, 1, W), seg, *k_views, *v_views, k_new.reshape(NS, 1, W), v_new.reshape(NS, 1, W))
    return out.reshape(NS, W)


def _expand_heads(x8, width):
    rows = x8.shape[0]
    lane = lax.broadcasted_iota(jnp.int32, (rows, LANES), 1)
    parts = []
    for p in range(width // LANES):
        lo = jnp.broadcast_to(x8[:, 2 * p:2 * p + 1], (rows, LANES))
        hi = jnp.broadcast_to(x8[:, 2 * p + 1:2 * p + 2], (rows, LANES))
        parts.append(jnp.where(lane < HEAD_DIM, lo, hi))
    return jnp.concatenate(parts, axis=-1)


def _fox_sample_kernel(pt_ref, q_ref, kn_ref, vn_ref, lfn_ref, seg_ref, *refs, pp):
    del pt_ref
    k_refs, v_refs, lf_refs = refs[:pp], refs[pp:2 * pp], refs[2 * pp:3 * pp]
    o_ref, m_ref, l_ref, acc_ref, c_ref = refs[3 * pp:]
    g = pl.program_id(1)
    W = q_ref.shape[-1]
    page = k_refs[0].shape[-2]

    @pl.when(g == 0)
    def _():
        m_ref[...] = jnp.full_like(m_ref, -jnp.inf)
        l_ref[...] = jnp.zeros_like(l_ref)
        acc_ref[...] = jnp.zeros_like(acc_ref)
        c_ref[...] = jnp.zeros_like(c_ref)

    qs = q_ref[0] * ATTN_SCALE
    row = lax.broadcasted_iota(jnp.int32, (page, page), 0)
    col = lax.broadcasted_iota(jnp.int32, (page, page), 1)
    tri = jnp.where(col <= row, 1.0, 0.0).astype(F32)
    m, l, acc, c_run = m_ref[0:1, :], l_ref[0:1, :], acc_ref[0:1, :], c_ref[0:1, :]
    for i in range(pp):
        lf = _expand_heads(lf_refs[i][0], W)
        c = jnp.dot(tri, lf, precision=HIGHEST, preferred_element_type=F32) + c_run
        c_run = c[page - 1:page, :]
        s = _scores(k_refs[i][0], qs, seg_ref) - c
        m_new = jnp.maximum(m, jnp.max(s, axis=0, keepdims=True))
        alpha = jnp.exp(m - m_new)
        p = jnp.exp(s - m_new)
        l = alpha * l + jnp.sum(p, axis=0, keepdims=True)
        acc = alpha * acc + jnp.sum(p * v_refs[i][0], axis=0, keepdims=True)
        m = m_new
    m_ref[0:1, :], l_ref[0:1, :], acc_ref[0:1, :], c_ref[0:1, :] = m, l, acc, c_run

    @pl.when(g == pl.num_programs(1) - 1)
    def _():
        c_q = c_run + _expand_heads(lfn_ref[0], W)
        s_new = _scores(jnp.broadcast_to(kn_ref[0], (8, W)), qs, seg_ref)[0:1] - c_q
        m_fin = jnp.maximum(m, s_new)
        alpha = jnp.exp(m - m_fin)
        p_new = jnp.exp(s_new - m_fin)
        o_ref[0] = (alpha * acc + p_new * vn_ref[0]) / (alpha * l + p_new)


def _fox_sample(q, k_new, v_new, lf_new, page_table, pool_k, pool_v, pool_lf, pp):
    NS, W = q.shape
    n_pages = page_table.shape[1]
    page = pool_k.shape[1]
    H = pool_lf.shape[-1]
    seg = _segment_matrix(W, HEAD_DIM, BF16)
    vec = pl.BlockSpec((1, 1, W), lambda n, g, pt: (n, 0, 0))

    def paged(width, i):
        return pl.BlockSpec((1, page, width), lambda n, g, pt, i=i: (pt[n, g * pp + i], 0, 0))

    in_specs = [vec, vec, vec, pl.BlockSpec((1, 1, H), lambda n, g, pt: (n, 0, 0)),
                pl.BlockSpec((W, W), lambda n, g, pt: (0, 0), pipeline_mode=pl.Buffered(1))]
    in_specs += [paged(W, i) for i in range(pp)] * 2 + [paged(H, i) for i in range(pp)]
    stat = pltpu.VMEM((8, W), F32)
    out = pl.pallas_call(
        functools.partial(_fox_sample_kernel, pp=pp),
        grid_spec=pltpu.PrefetchScalarGridSpec(
            num_scalar_prefetch=1, grid=(NS, n_pages // pp), in_specs=in_specs, out_specs=vec,
            scratch_shapes=[stat, stat, stat, stat]),
        out_shape=jax.ShapeDtypeStruct((NS, 1, W), F32),
        compiler_params=_params(2), name="fox_sample")(
            page_table, q.reshape(NS, 1, W), k_new.reshape(NS, 1, W), v_new.reshape(NS, 1, W),
            lf_new.reshape(NS, 1, H), seg, *([pool_k] * pp), *([pool_v] * pp), *([pool_lf] * pp))
    return out.reshape(NS, W)


def _moba_sample_kernel(pt_ref, q_ref, kn_ref, vn_ref, seg_ref, segf_ref, *refs, bps):
    del pt_ref
    k_refs, v_refs = refs[:2 * bps], refs[2 * bps:4 * bps]
    o_ref, m_ref, l_ref, acc_ref, gate_ref = refs[4 * bps:]
    g = pl.program_id(1)
    W = q_ref.shape[-1]
    q = q_ref[0]
    qs = q * ATTN_SCALE
    for i in range(bps):
        k = jnp.concatenate([k_refs[2 * i][0], k_refs[2 * i + 1][0]], axis=0)
        v = jnp.concatenate([v_refs[2 * i][0], v_refs[2 * i + 1][0]], axis=0)
        s = _scores(k, qs, seg_ref)
        m = jnp.max(s, axis=0, keepdims=True)
        p = jnp.exp(s - m)
        kmean = jnp.sum(k, axis=0, keepdims=True) * (1.0 / MOBA_BLOCK)
        gate = jnp.dot(jnp.broadcast_to(kmean * q, (8, W)), segf_ref[...], precision=HIGHEST,
                       preferred_element_type=F32)[0:1]
        j = g * bps + i
        m_ref[pl.ds(j, 1), :] = m
        l_ref[pl.ds(j, 1), :] = jnp.sum(p, axis=0, keepdims=True)
        acc_ref[pl.ds(j, 1), :] = jnp.sum(p * v, axis=0, keepdims=True)
        gate_ref[pl.ds(j, 1), :] = gate

    @pl.when(g == pl.num_programs(1) - 1)
    def _():
        gates = gate_ref[...]
        blk = lax.broadcasted_iota(jnp.int32, gates.shape, 0).astype(F32)
        sel = jnp.zeros(gates.shape, jnp.bool_)
        for _ in range(MOBA_TOPK):
            mx = jnp.max(gates, axis=0, keepdims=True)
            first = jnp.min(jnp.where(gates == mx, blk, float(gates.shape[0])), axis=0, keepdims=True)
            pick = blk == first
            sel = sel | pick
            gates = jnp.where(pick, -jnp.inf, gates)
        s_new = _scores(jnp.broadcast_to(kn_ref[0], (8, W)), qs, seg_ref)[0:1]
        ms = m_ref[...]
        m_fin = jnp.maximum(jnp.max(jnp.where(sel, ms, -jnp.inf), axis=0, keepdims=True), s_new)
        w = jnp.where(sel, jnp.exp(ms - m_fin), 0.0)
        p_new = jnp.exp(s_new - m_fin)
        l = jnp.sum(w * l_ref[...], axis=0, keepdims=True) + p_new
        acc = jnp.sum(w * acc_ref[...], axis=0, keepdims=True) + p_new * vn_ref[0]
        o_ref[0] = acc / l


def _moba_sample(q, k_new, v_new, page_table, pool_k, pool_v, bps):
    NS, W = q.shape
    n_pages = page_table.shape[1]
    page = pool_k.shape[1]
    assert MOBA_BLOCK == 2 * page and n_pages % (2 * bps) == 0
    n_blocks = n_pages // 2
    assert n_blocks >= MOBA_TOPK
    seg = _segment_matrix(W, HEAD_DIM, BF16)
    segf = _segment_matrix(W, HEAD_DIM, F32)
    vec = pl.BlockSpec((1, 1, W), lambda n, g, pt: (n, 0, 0))
    const = lambda: pl.BlockSpec((W, W), lambda n, g, pt: (0, 0), pipeline_mode=pl.Buffered(1))
    pages = [pl.BlockSpec((1, page, W), lambda n, g, pt, i=i: (pt[n, g * 2 * bps + i], 0, 0))
             for i in range(2 * bps)]
    stat = pltpu.VMEM((n_blocks, W), F32)
    out = pl.pallas_call(
        functools.partial(_moba_sample_kernel, bps=bps),
        grid_spec=pltpu.PrefetchScalarGridSpec(
            num_scalar_prefetch=1, grid=(NS, n_blocks // bps),
            in_specs=[vec, vec, vec, const(), const()] + pages * 2, out_specs=vec,
            scratch_shapes=[stat, stat, stat, stat]),
        out_shape=jax.ShapeDtypeStruct((NS, 1, W), F32),
        compiler_params=_params(2), name="moba_sample")(
            page_table, q.reshape(NS, 1, W), k_new.reshape(NS, 1, W), v_new.reshape(NS, 1, W), seg, segf,
            *([pool_k] * (2 * bps)), *([pool_v] * (2 * bps)))
    return out.reshape(NS, W)


TM_PROJ = 512
TN_PROJ = 512
TM_ROW = 256
TQ_FOX = 256
TQ_DIL = 512
FOX_PAGES_PER_STEP = 8
MOBA_BLOCKS_PER_STEP = 4


def kernel(x_prompt, x_sample, mem_prompt, page_table, cache_fox_k, cache_fox_v, cache_fox_logf, cache_moba_k, cache_moba_v, cache_dwin_k, cache_dwin_v, cache_mem_k, cache_mem_v, state_conv, w_in_even, b_forget, w_out_even, w_in_odd, w_out_odd, w_mem_q, w_mem_k, w_mem_v, w_mem_o, w_ffn_in, ffn_conv_w, ffn_conv_b, w_ffn_out, ln_g, ln_b):
    B, S, D = x_prompt.shape
    NS, T, _ = x_sample.shape
    assert T == 1 and D == D_MODEL
    n_pool, page = cache_fox_k.shape[1], cache_fox_k.shape[2]
    P = page_table.shape[1] * page
    H8 = b_forget.shape[1]
    W8 = H8 * HEAD_DIM
    W16 = D
    n_mem = mem_prompt.shape[1]

    rope_p = _rope_tables(jnp.arange(S, dtype=jnp.int32))
    rope_s = tuple(jnp.broadcast_to(t, (NS, LANES)) for t in _rope_tables(jnp.full((1,), P, jnp.int32)))

    hp = x_prompt.reshape(B * S, D)
    hs = x_sample.reshape(NS, D)
    mem2 = mem_prompt.reshape(B * n_mem, D)
    out = {name: [] for name in (
        "fox_k_p", "fox_v_p", "fox_l_p", "moba_k_p", "moba_v_p", "dwin_k_p", "dwin_v_p", "mem_k_p", "mem_v_p",
        "conv_p", "fox_k_s", "fox_v_s", "fox_l_s", "moba_k_s", "moba_v_s", "dwin_k_s", "dwin_v_s", "conv_s")}

    for l in range(DEPTH):
        g, bta = ln_g[l], ln_b[l]
        if l % 2 == 0:
            e = l // 2
            w = w_in_even[e]
            w_fox = w[:, :3 * W8].astype(BF16)
            w_f = w[:, 3 * W8:3 * W8 + H8]
            w_mq = w[:, 3 * W8 + H8:3 * W8 + H8 + 2 * W8].astype(BF16)
            w_mv = w[:, 3 * W8 + H8 + 2 * W8:].astype(BF16)
            wo = w_out_even[e].astype(BF16)
            fox = _proj(hp, w_fox, TM_PROJ, TN_PROJ)
            qa, ka, va = (fox[:, i * W8:(i + 1) * W8].reshape(B, S, W8) for i in range(3))
            lf, c = _forget(hp, w_f, b_forget[e], TM_ROW, S)
            mqk = _proj(hp, w_mq, TM_PROJ, TN_PROJ, rope=rope_p)
            qb, kb = (mqk[:, i * W8:(i + 1) * W8].reshape(B, S, W8) for i in range(2))
            vb = _proj(hp, w_mv, TM_PROJ, TN_PROJ).reshape(B, S, W8)
            oa = _fox_prompt(qa, ka, va, c[:, :H8].reshape(B, S, H8), TQ_FOX)
            ob = _moba_prompt(qb, kb, vb)
            hp = _out_ln([oa.reshape(B * S, W8), ob.reshape(B * S, W8)], [wo[:W8], wo[W8:]], hp,
                         g[0], bta[0], TM_ROW)
            out["fox_k_p"].append(ka.reshape(B, S, H8, HEAD_DIM))
            out["fox_v_p"].append(va.reshape(B, S, H8, HEAD_DIM))
            out["fox_l_p"].append(lf[:, :H8].reshape(B, S, H8))
            out["moba_k_p"].append(kb.reshape(B, S, H8, HEAD_DIM))
            out["moba_v_p"].append(vb.reshape(B, S, H8, HEAD_DIM))
            fox = _proj(hs, w_fox, NS, TN_PROJ)
            qa, ka, va = (fox[:, i * W8:(i + 1) * W8] for i in range(3))
            lf, _ = _forget(hs, w_f, b_forget[e], NS, NS)
            lf = lf[:, :H8]
            mqk = _proj(hs, w_mq, NS, TN_PROJ, rope=rope_s)
            qb, kb = (mqk[:, i * W8:(i + 1) * W8] for i in range(2))
            vb = _proj(hs, w_mv, NS, TN_PROJ)
            oa = _fox_sample(qa, ka, va, lf, page_table, cache_fox_k[e].reshape(n_pool, page, W8),
                             cache_fox_v[e].reshape(n_pool, page, W8), cache_fox_logf[e], FOX_PAGES_PER_STEP)
            ob = _moba_sample(qb, kb, vb, page_table, cache_moba_k[e].reshape(n_pool, page, W8),
                              cache_moba_v[e].reshape(n_pool, page, W8), MOBA_BLOCKS_PER_STEP)
            hs = _out_ln([oa, ob], [wo[:W8], wo[W8:]], hs, g[0], bta[0], NS)
            out["fox_k_s"].append(ka.reshape(NS, 1, H8, HEAD_DIM))
            out["fox_v_s"].append(va.reshape(NS, 1, H8, HEAD_DIM))
            out["fox_l_s"].append(lf.reshape(NS, 1, H8))
            out["moba_k_s"].append(kb.reshape(NS, 1, H8, HEAD_DIM))
            out["moba_v_s"].append(vb.reshape(NS, 1, H8, HEAD_DIM))
        else:
            od = l // 2
            w = w_in_odd[od]
            w_qk = w[:, :2 * W16].astype(BF16)
            w_v = w[:, 2 * W16:].astype(BF16)
            wo = w_out_odd[od].astype(BF16)
            qk = _proj(hp, w_qk, TM_PROJ, TN_PROJ, rope=rope_p)
            q, k = (qk[:, i * W16:(i + 1) * W16].reshape(B, S, W16) for i in range(2))
            v = _proj(hp, w_v, TM_PROJ, TN_PROJ).reshape(B, S, W16)
            branches = [_dil_branch(q, k, v, dil, TQ_DIL) for _, dil in DIL_PAIRS]
            hp = _dil_merge_ln([o.reshape(B * S, W16) for o, _ in branches],
                               [e_.reshape(B * S, W16) for _, e_ in branches], wo, hp, g[0], bta[0], TM_ROW)
            keep = min(DIL_PAIRS[-1][0], S)
            out["dwin_k_p"].append(k[:, S - keep:].reshape(B, keep, W16 // HEAD_DIM, HEAD_DIM))
            out["dwin_v_p"].append(v[:, S - keep:].reshape(B, keep, W16 // HEAD_DIM, HEAD_DIM))
            qk = _proj(hs, w_qk, NS, TN_PROJ, rope=rope_s)
            q, k = (qk[:, i * W16:(i + 1) * W16] for i in range(2))
            v = _proj(hs, w_v, NS, TN_PROJ)
            L_C = cache_dwin_k.shape[2]
            assert P >= L_C
            o = _dil_sample(q, k, v, cache_dwin_k[od].reshape(NS, L_C, W16), cache_dwin_v[od].reshape(NS, L_C, W16))
            hs = _out_ln([o], [wo], hs, g[0], bta[0], NS)
            out["dwin_k_s"].append(k.reshape(NS, 1, W16 // HEAD_DIM, HEAD_DIM))
            out["dwin_v_s"].append(v.reshape(NS, 1, W16 // HEAD_DIM, HEAD_DIM))
        wq, wk, wv, wo = (t[l].astype(BF16) for t in (w_mem_q, w_mem_k, w_mem_v, w_mem_o))
        mk = _proj(mem2, wk, TM_ROW, TN_PROJ).reshape(B, n_mem, D)
        mv = _proj(mem2, wv, TM_ROW, TN_PROJ).reshape(B, n_mem, D)
        out["mem_k_p"].append(mk.reshape(B, n_mem, N_MEM_HEADS, MEM_HEAD_DIM))
        out["mem_v_p"].append(mv.reshape(B, n_mem, N_MEM_HEADS, MEM_HEAD_DIM))
        hp = _mem_block(hp, wq, mk, mv, wo, g[1], bta[1], TM_ROW, S)
        qs_ = _proj(hs, wq, NS, TN_PROJ)
        os_ = _mem_sample(qs_, cache_mem_k[l].reshape(NS, n_mem, D), cache_mem_v[l].reshape(NS, n_mem, D))
        hs = _out_ln([os_], [wo], hs, g[1], bta[1], NS)
        win, wout = w_ffn_in[l].astype(BF16), w_ffn_out[l].astype(BF16)
        hp, cp = _ffn_prompt(hp, win, ffn_conv_w[l], ffn_conv_b[l], wout, g[2], bta[2], TM_ROW, S)
        st = state_conv[l]
        hs, u = _ffn_sample(hs, st[:, 0], st[:, 1], win, ffn_conv_w[l], ffn_conv_b[l], wout, g[2], bta[2])
        out["conv_p"].append(cp)
        out["conv_s"].append(jnp.stack([st[:, 1], u], axis=1))

    st = jnp.stack
    return (hp.reshape(B, S, D), hs.reshape(NS, 1, D),
            st(out["fox_k_p"]), st(out["fox_v_p"]), st(out["fox_l_p"]), st(out["moba_k_p"]), st(out["moba_v_p"]),
            st(out["dwin_k_p"]), st(out["dwin_v_p"]), st(out["mem_k_p"]), st(out["mem_v_p"]), st(out["conv_p"]),
            st(out["fox_k_s"]), st(out["fox_v_s"]), st(out["fox_l_s"]), st(out["moba_k_s"]), st(out["moba_v_s"]),
            st(out["dwin_k_s"]), st(out["dwin_v_s"]), st(out["conv_s"]))
```

```python
import functools
import math

import jax
import jax.numpy as jnp
from jax import lax
from jax.experimental import pallas as pl
from jax.experimental.pallas import tpu as pltpu

F32 = jnp.float32
BF16 = jnp.bfloat16

LANES = 128

D_MODEL = 1024
HEAD_DIM = 64
HEADS_PER_VREG = LANES // HEAD_DIM
N_MEM_HEADS = 4
MEM_HEAD_DIM = D_MODEL // N_MEM_HEADS
D_FF = 2816
FFN_CHUNK = 256
MOBA_BLOCK = 256
MOBA_BLOCK_SHIFT = 8
MOBA_TOPK = 3
DIL_PAIRS = ((128, 1), (512, 4), (2048, 16))
DIL_KEYS = 128
ROPE_THETA = 10000.0
LN_EPS = 1e-5
DEPTH = 4
DN_ALPHA = (2 * DEPTH) ** 0.25
ATTN_SCALE = HEAD_DIM ** -0.5
MEM_SCALE = MEM_HEAD_DIM ** -0.5
LOG2E = math.log2(math.e)
NEG_BIG = -1e30
VMEM_LIMIT = 56 * 2 ** 20

NT_DIMS = (((1,), (1,)), ((), ()))
HIGHEST = lax.Precision.HIGHEST


def _params(n_axes):
    return pltpu.CompilerParams(dimension_semantics=("arbitrary",) * n_axes,
                                vmem_limit_bytes=VMEM_LIMIT)


def _const_spec(shape):
    zeros = (0,) * len(shape)
    return pl.BlockSpec(shape, lambda *_: zeros, pipeline_mode=pl.Buffered(1))


def _layer_norm(y, g, b):
    mu = jnp.mean(y, axis=-1, keepdims=True)
    yc = y - mu
    var = jnp.mean(yc * yc, axis=-1, keepdims=True)
    return yc * lax.rsqrt(var + LN_EPS) * g + b


def _rope_angles(pos):
    half = HEAD_DIM // 2
    inv = ROPE_THETA ** (-jnp.arange(half, dtype=F32) / half)
    ang = pos.astype(F32)[:, None] * inv[None, :]
    return jnp.cos(ang), jnp.sin(ang)


def _rope_tables(pos):
    cos, sin = _rope_angles(pos)
    cos2 = jnp.tile(jnp.concatenate([cos, cos], axis=-1), (1, HEADS_PER_VREG))
    sin2 = jnp.tile(jnp.concatenate([-sin, sin], axis=-1), (1, HEADS_PER_VREG))
    return cos2, sin2


def _rope_tables_t(pos):
    cos, sin = _rope_angles(pos)
    return jnp.concatenate([cos, cos], axis=-1).T, jnp.concatenate([-sin, sin], axis=-1).T


def _proj_kernel(x_ref, w_ref, o_ref):
    o_ref[...] = jnp.dot(x_ref[...].astype(BF16), w_ref[...], preferred_element_type=F32)


def _proj_rope_kernel(x_ref, w_ref, cos_ref, sin_ref, o_ref):
    z = jnp.dot(x_ref[...].astype(BF16), w_ref[...], preferred_element_type=F32)
    tm, tn = z.shape
    cos, sin = cos_ref[...], sin_ref[...]
    lane = lax.broadcasted_iota(jnp.int32, (tm, LANES), 1)
    first_half = (lane & (HEAD_DIM - 1)) < HEAD_DIM // 2
    for c in range(tn // LANES):
        zc = z[:, c * LANES:(c + 1) * LANES]
        partner = jnp.where(first_half, pltpu.roll(zc, LANES - HEAD_DIM // 2, 1),
                            pltpu.roll(zc, HEAD_DIM // 2, 1))
        o_ref[:, c * LANES:(c + 1) * LANES] = zc * cos + partner * sin


def _proj(x, w, tm, tn, rope=None):
    M, K = x.shape
    N = w.shape[1]
    grid = (M // tm, N // tn)
    x_spec = pl.BlockSpec((tm, K), lambda i, j: (i, 0))
    w_spec = pl.BlockSpec((K, tn), lambda i, j: (0, j))
    o_spec = pl.BlockSpec((tm, tn), lambda i, j: (i, j))
    out_shape = jax.ShapeDtypeStruct((M, N), F32)
    if rope is None:
        return pl.pallas_call(_proj_kernel, grid=grid, in_specs=[x_spec, w_spec], out_specs=o_spec,
                              out_shape=out_shape, compiler_params=_params(2), name="proj")(x, w)
    cos2, sin2 = rope
    pos_blocks = cos2.shape[0] // tm
    t_spec = pl.BlockSpec((tm, LANES), lambda i, j: (i % pos_blocks, 0))
    return pl.pallas_call(_proj_rope_kernel, grid=grid, in_specs=[x_spec, w_spec, t_spec, t_spec],
                          out_specs=o_spec, out_shape=out_shape, compiler_params=_params(2),
                          name="proj_rope")(x, w, cos2, sin2)


def _proj_t_kernel(x_ref, wt_ref, o_ref):
    o_ref[0] = lax.dot_general(wt_ref[...], x_ref[...].astype(BF16), NT_DIMS, preferred_element_type=F32)


def _proj_t_rope_kernel(x_ref, wt_ref, cos_ref, sin_ref, o_ref):
    z = lax.dot_general(wt_ref[...], x_ref[...].astype(BF16), NT_DIMS, preferred_element_type=F32)
    cos, sin = cos_ref[...], sin_ref[...]
    half = HEAD_DIM // 2
    for g in range(z.shape[0] // HEAD_DIM):
        zg = z[g * HEAD_DIM:(g + 1) * HEAD_DIM]
        partner = jnp.concatenate([zg[half:], zg[:half]], axis=0)
        o_ref[0, g * HEAD_DIM:(g + 1) * HEAD_DIM, :] = zg * cos + partner * sin


def _proj_t(x, wt, seq_len, keep, tm, tn, rope_t=None):
    M, K = x.shape
    N = wt.shape[0]
    B = M // seq_len
    tiles_in, tiles_out = seq_len // tm, keep // tm
    first = tiles_in - tiles_out
    x_spec = pl.BlockSpec((tm, K), lambda i, j: ((i // tiles_out) * tiles_in + first + i % tiles_out, 0))
    w_spec = pl.BlockSpec((tn, K), lambda i, j: (j, 0))
    o_spec = pl.BlockSpec((1, tn, tm), lambda i, j: (i // tiles_out, j, i % tiles_out))
    out_shape = jax.ShapeDtypeStruct((B, N, keep), F32)
    grid = (B * tiles_out, N // tn)
    if rope_t is None:
        return pl.pallas_call(_proj_t_kernel, grid=grid, in_specs=[x_spec, w_spec], out_specs=o_spec,
                              out_shape=out_shape, compiler_params=_params(2), name="proj_t")(x, wt)
    t_spec = pl.BlockSpec((HEAD_DIM, tm), lambda i, j: (0, first + i % tiles_out))
    return pl.pallas_call(_proj_t_rope_kernel, grid=grid, in_specs=[x_spec, w_spec, t_spec, t_spec],
                          out_specs=o_spec, out_shape=out_shape, compiler_params=_params(2),
                          name="proj_t_rope")(x, wt, *rope_t)


def _forget_kernel(x_ref, w_ref, b_ref, lf_ref, c_ref, carry_ref, *, tiles_per_seq):
    i = pl.program_id(0)
    z = jnp.dot(x_ref[...].astype(BF16), w_ref[...], preferred_element_type=F32) + b_ref[...]
    lf = -(jnp.maximum(-z, 0.0) + jnp.log1p(jnp.exp(-jnp.abs(z))))
    lf_ref[...] = lf

    @pl.when(i % tiles_per_seq == 0)
    def _():
        carry_ref[...] = jnp.zeros_like(carry_ref)

    tm = lf.shape[0]
    row = lax.broadcasted_iota(jnp.int32, (tm, tm), 0)
    col = lax.broadcasted_iota(jnp.int32, (tm, tm), 1)
    tri = jnp.where(col <= row, 1.0, 0.0).astype(F32)
    c = jnp.dot(tri, lf, precision=HIGHEST, preferred_element_type=F32) + carry_ref[0:1, :]
    c_ref[...] = c
    carry_ref[0:1, :] = c[tm - 1:tm, :]


def _forget(x, w_f, b_f, tm, seq_len):
    M, K = x.shape
    n_heads = w_f.shape[1]
    w = jnp.zeros((K, LANES), BF16).at[:, :n_heads].set(w_f.astype(BF16))
    b = jnp.zeros((1, LANES), F32).at[0, :n_heads].set(b_f)
    kern = functools.partial(_forget_kernel, tiles_per_seq=seq_len // tm)
    out = jax.ShapeDtypeStruct((M, LANES), F32)
    return pl.pallas_call(
        kern, grid=(M // tm,),
        in_specs=[pl.BlockSpec((tm, K), lambda i: (i, 0)), _const_spec((K, LANES)), _const_spec((1, LANES))],
        out_specs=[pl.BlockSpec((tm, LANES), lambda i: (i, 0))] * 2,
        out_shape=[out, out], scratch_shapes=[pltpu.VMEM((8, LANES), F32)],
        compiler_params=_params(1), name="forget")(x, w, b)


def _out_ln_kernel(*refs, n_in):
    a_refs, w_refs = refs[:n_in], refs[n_in:2 * n_in]
    x_ref, g_ref, b_ref, o_ref = refs[2 * n_in:]
    acc = None
    for a_ref, w_ref in zip(a_refs, w_refs):
        d = jnp.dot(a_ref[...].astype(BF16), w_ref[...], preferred_element_type=F32)
        acc = d if acc is None else acc + d
    o_ref[...] = _layer_norm(DN_ALPHA * x_ref[...] + acc, g_ref[...], b_ref[...])


def _out_ln(a_list, w_list, x, g, b, tm):
    M, D = x.shape
    n_in = len(a_list)
    in_specs = [pl.BlockSpec((tm, a.shape[1]), lambda i: (i, 0)) for a in a_list]
    in_specs += [_const_spec(w.shape) for w in w_list]
    in_specs += [pl.BlockSpec((tm, D), lambda i: (i, 0)), _const_spec((1, D)), _const_spec((1, D))]
    return pl.pallas_call(
        functools.partial(_out_ln_kernel, n_in=n_in), grid=(M // tm,), in_specs=in_specs,
        out_specs=pl.BlockSpec((tm, D), lambda i: (i, 0)), out_shape=jax.ShapeDtypeStruct((M, D), F32),
        compiler_params=_params(1), name="out_ln")(*a_list, *w_list, x, g.reshape(1, D), b.reshape(1, D))


def _dil_merge_ln_kernel(o1, o2, o3, e1, e2, e3, w_ref, x_ref, g_ref, b_ref, o_ref):
    es = [e1[...], e2[...], e3[...]]
    m = jnp.maximum(jnp.maximum(es[0], es[1]), es[2])
    ws = [jnp.exp(e - m) for e in es]
    num = ws[0] * o1[...] + ws[1] * o2[...] + ws[2] * o3[...]
    a = num / (ws[0] + ws[1] + ws[2])
    mix = jnp.dot(a.astype(BF16), w_ref[...], preferred_element_type=F32)
    o_ref[...] = _layer_norm(DN_ALPHA * x_ref[...] + mix, g_ref[...], b_ref[...])


def _dil_merge_ln(outs, lses, w, x, g, b, tm):
    M, D = x.shape
    row = pl.BlockSpec((tm, D), lambda i: (i, 0))
    return pl.pallas_call(
        _dil_merge_ln_kernel, grid=(M // tm,),
        in_specs=[row] * 6 + [_const_spec(w.shape), row, _const_spec((1, D)), _const_spec((1, D))],
        out_specs=row, out_shape=jax.ShapeDtypeStruct((M, D), F32),
        compiler_params=_params(1), name="dil_merge_ln")(*outs, *lses, w, x, g.reshape(1, D), b.reshape(1, D))


def _mem_kernel(x_ref, wq_ref, mk_ref, mv_ref, wo_ref, g_ref, b_ref, o_ref):
    x = x_ref[...]
    q = jnp.dot(x.astype(BF16), wq_ref[...], preferred_element_type=F32)
    mk = mk_ref[0].astype(BF16)
    mv = mv_ref[0].astype(BF16)
    heads = []
    for h in range(N_MEM_HEADS):
        sl = slice(h * MEM_HEAD_DIM, (h + 1) * MEM_HEAD_DIM)
        qh = (q[:, sl] * MEM_SCALE).astype(BF16)
        s = lax.dot_general(qh, mk[:, sl], NT_DIMS, preferred_element_type=F32)
        e = jnp.exp(s - jnp.max(s, axis=-1, keepdims=True))
        l = jnp.sum(e, axis=-1, keepdims=True)
        heads.append(jnp.dot(e.astype(BF16), mv[:, sl], preferred_element_type=F32) / l)
    o = jnp.concatenate(heads, axis=-1)
    mix = jnp.dot(o.astype(BF16), wo_ref[...], preferred_element_type=F32)
    o_ref[...] = _layer_norm(DN_ALPHA * x + mix, g_ref[...], b_ref[...])


def _mem_block(x, wq, mk, mv, wo, g, b, tm, seq_len):
    M, D = x.shape
    n_mem = mk.shape[1]
    tiles_per_seq = seq_len // tm
    row = pl.BlockSpec((tm, D), lambda i: (i, 0))
    mem = pl.BlockSpec((1, n_mem, D), lambda i: (i // tiles_per_seq, 0, 0))
    return pl.pallas_call(
        _mem_kernel, grid=(M // tm,),
        in_specs=[row, _const_spec((D, D)), mem, mem, _const_spec((D, D)),
                  _const_spec((1, D)), _const_spec((1, D))],
        out_specs=row, out_shape=jax.ShapeDtypeStruct((M, D), F32),
        compiler_params=_params(1), name="mem_block")(x, wq, mk, mv, wo, g.reshape(1, D), b.reshape(1, D))


def _gated(ha, hg):
    return ha * (0.5 * hg * (1.0 + lax.erf(hg * (2.0 ** -0.5))))


def _ffn_prompt_kernel(x_ref, win_ref, cw_ref, cb_ref, wout_ref, g_ref, b_ref, o_ref, st_ref, carry_ref,
                       *, tiles_per_seq):
    i = pl.program_id(0)

    @pl.when(i % tiles_per_seq == 0)
    def _():
        carry_ref[...] = jnp.zeros_like(carry_ref)

    x = x_ref[...]
    xb = x.astype(BF16)
    tm = x.shape[0]
    row = lax.broadcasted_iota(jnp.int32, (tm, FFN_CHUNK), 0)
    acc = jnp.zeros((tm, D_MODEL), F32)
    for c in range(D_FF // FFN_CHUNK):
        hs = []
        for off in (c * FFN_CHUNK, D_FF + c * FFN_CHUNK):
            sl = slice(off, off + FFN_CHUNK)
            u = jnp.dot(xb, win_ref[:, sl], preferred_element_type=F32)
            prev2 = carry_ref[0:1, sl]
            prev1 = carry_ref[1:2, sl]
            u1 = jnp.where(row == 0, prev1, pltpu.roll(u, 1, 0))
            u2 = jnp.where(row == 0, prev2, jnp.where(row == 1, prev1, pltpu.roll(u, 2, 0)))
            last = u[tm - 2:tm, :]
            carry_ref[0:2, sl] = last
            st_ref[0, :, sl] = last
            hs.append(cb_ref[:, sl] + cw_ref[0:1, sl] * u2 + cw_ref[1:2, sl] * u1 + cw_ref[2:3, sl] * u)
        act = _gated(hs[0], hs[1])
        acc = acc + jnp.dot(act.astype(BF16), wout_ref[c * FFN_CHUNK:(c + 1) * FFN_CHUNK, :],
                            preferred_element_type=F32)
    o_ref[...] = _layer_norm(DN_ALPHA * x + acc, g_ref[...], b_ref[...])


def _ffn_prompt(x, win, cw, cb, wout, g, b, tm, seq_len):
    M, D = x.shape
    F2 = win.shape[1]
    tiles_per_seq = seq_len // tm
    row = pl.BlockSpec((tm, D), lambda i: (i, 0))
    return pl.pallas_call(
        functools.partial(_ffn_prompt_kernel, tiles_per_seq=tiles_per_seq), grid=(M // tm,),
        in_specs=[row, _const_spec((D, F2)), _const_spec((3, F2)), _const_spec((1, F2)),
                  _const_spec((F2 // 2, D)), _const_spec((1, D)), _const_spec((1, D))],
        out_specs=[row, pl.BlockSpec((1, 2, F2), lambda i: (i // tiles_per_seq, 0, 0))],
        out_shape=[jax.ShapeDtypeStruct((M, D), F32), jax.ShapeDtypeStruct((M // seq_len, 2, F2), F32)],
        scratch_shapes=[pltpu.VMEM((8, F2), F32)],
        compiler_params=_params(1), name="ffn_prompt")(x, win, cw, cb.reshape(1, F2), wout,
                                                       g.reshape(1, D), b.reshape(1, D))


def _ffn_sample_kernel(x_ref, s0_ref, s1_ref, win_ref, cw_ref, cb_ref, wout_ref, g_ref, b_ref, o_ref, u_ref):
    x = x_ref[...]
    xb = x.astype(BF16)
    acc = jnp.zeros(x.shape, F32)
    for c in range(D_FF // FFN_CHUNK):
        hs = []
        for off in (c * FFN_CHUNK, D_FF + c * FFN_CHUNK):
            sl = slice(off, off + FFN_CHUNK)
            u = jnp.dot(xb, win_ref[:, sl], preferred_element_type=F32)
            u_ref[:, sl] = u
            hs.append(cb_ref[:, sl] + cw_ref[0:1, sl] * s0_ref[:, sl] + cw_ref[1:2, sl] * s1_ref[:, sl]
                      + cw_ref[2:3, sl] * u)
        act = _gated(hs[0], hs[1])
        acc = acc + jnp.dot(act.astype(BF16), wout_ref[c * FFN_CHUNK:(c + 1) * FFN_CHUNK, :],
                            preferred_element_type=F32)
    o_ref[...] = _layer_norm(DN_ALPHA * x + acc, g_ref[...], b_ref[...])


def _ffn_sample(x, s0, s1, win, cw, cb, wout, g, b):
    M, D = x.shape
    F2 = win.shape[1]
    return pl.pallas_call(
        _ffn_sample_kernel, grid=(1,),
        in_specs=[_const_spec((M, D)), _const_spec((M, F2)), _const_spec((M, F2)), _const_spec((D, F2)),
                  _const_spec((3, F2)), _const_spec((1, F2)), _const_spec((F2 // 2, D)),
                  _const_spec((1, D)), _const_spec((1, D))],
        out_specs=[pl.BlockSpec((M, D), lambda i: (0, 0)), pl.BlockSpec((M, F2), lambda i: (0, 0))],
        out_shape=[jax.ShapeDtypeStruct((M, D), F32), jax.ShapeDtypeStruct((M, F2), F32)],
        compiler_params=_params(1), name="ffn_sample")(x, s0, s1, win, cw, cb.reshape(1, F2), wout,
                                                       g.reshape(1, D), b.reshape(1, D))


def _lane_masks(rows):
    lane = lax.broadcasted_iota(jnp.int32, (rows, LANES), 1)
    is_h0 = lane < HEAD_DIM
    return lane, (is_h0, jnp.logical_not(is_h0))


def _flash_core(ka_refs, vb_ref, q_aug, qi, tq, tk):
    ratio = tq // tk
    qs = qi * tq

    def step(ks, states, mask):
        vt = vb_ref[:, pl.ds(ks, tk)]
        new = []
        for h in range(HEADS_PER_VREG):
            m, l, acc = states[h]
            s = lax.dot_general(ka_refs[h][pl.ds(ks, tk), :], q_aug[h], NT_DIMS, preferred_element_type=F32)
            if mask is not None:
                s = jnp.where(mask, s, -jnp.inf)
            m_new = jnp.maximum(m, jnp.max(s, axis=0, keepdims=True))
            alpha = jnp.exp2(m - m_new)
            p = jnp.exp2(s - m_new)
            l = alpha * l + jnp.sum(p, axis=0, keepdims=True)
            acc = alpha * acc + jnp.dot(vt, p.astype(BF16), preferred_element_type=F32)
            new.append((m_new, l, acc))
        return tuple(new)

    init = (jnp.full((1, tq), -jnp.inf, F32), jnp.zeros((1, tq), F32), jnp.zeros((LANES, tq), F32))
    states = lax.fori_loop(0, qi * ratio, lambda j, st: step(pl.multiple_of(j * tk, tk), st, None), (init, init))
    kpos = lax.broadcasted_iota(jnp.int32, (tk, tq), 0)
    qpos = lax.broadcasted_iota(jnp.int32, (tk, tq), 1)
    for d in range(ratio):
        states = step(pl.multiple_of(qs + d * tk, tk), states, kpos + d * tk <= qpos)
    (_, l0, a0), (_, l1, a1) = states
    row = lax.broadcasted_iota(jnp.int32, (LANES, tq), 0)
    return jnp.where(row < HEAD_DIM, a0 / l0, a1 / l1).T


def _split3(x):
    hi = x.astype(BF16).astype(F32)
    r = x - hi
    mid = r.astype(BF16).astype(F32)
    lo = (r - mid).astype(BF16).astype(F32)
    return hi, mid, lo


def _place(lane, base, cols):
    out = jnp.zeros(lane.shape, F32)
    for i, v in enumerate(cols):
        out = jnp.where(lane == base + i, v, out)
    return out


def _fox_kernel(q_ref, k_ref, vt_ref, cc_ref, o_ref, ka0_ref, ka1_ref, vb_ref, *, tq, tk):
    qi = pl.program_id(2)
    S = k_ref.shape[1]
    fill = MOBA_BLOCK
    ka_refs = (ka0_ref, ka1_ref)

    @pl.when(qi == 0)
    def _():
        vb_ref[...] = vt_ref[0].astype(BF16)
        lane, in_head = _lane_masks(fill)

        def body(j, _):
            rs = pl.multiple_of(j * fill, fill)
            kc = k_ref[0, pl.ds(rs, fill), :]
            cc = cc_ref[0, 0, pl.ds(rs, fill), :] * LOG2E
            for h in range(HEADS_PER_VREG):
                hi, mid, lo = _split3(cc[:, h:h + 1])
                one = jnp.ones_like(hi)
                extra = _place(lane, (1 - h) * HEAD_DIM, (-hi, -mid, -lo, one, one, one))
                ka_refs[h][pl.ds(rs, fill), :] = jnp.where(in_head[h], kc, extra).astype(BF16)
            return 0

        lax.fori_loop(0, S // fill, body, 0)

    lane, in_head = _lane_masks(tq)
    q = q_ref[0] * (ATTN_SCALE * LOG2E)
    cq = cc_ref[0, 0, pl.ds(pl.multiple_of(qi * tq, tq), tq), :] * LOG2E
    q_aug = []
    for h in range(HEADS_PER_VREG):
        hi, mid, lo = _split3(cq[:, h:h + 1])
        one = jnp.ones_like(hi)
        extra = _place(lane, (1 - h) * HEAD_DIM, (one, one, one, hi, mid, lo))
        q_aug.append(jnp.where(in_head[h], q, extra).astype(BF16))
    o_ref[0] = _flash_core(ka_refs, vb_ref, q_aug, qi, tq, tk)


def _fox_prompt(qk, vt, c, tq, tk):
    B, S, W2 = qk.shape
    W = W2 // 2
    n_pairs = W // LANES
    cc = c.reshape(B, S, n_pairs, HEADS_PER_VREG).transpose(0, 2, 1, 3)
    one = pl.Buffered(1)
    return pl.pallas_call(
        functools.partial(_fox_kernel, tq=tq, tk=tk), grid=(B, n_pairs, S // tq),
        in_specs=[pl.BlockSpec((1, tq, LANES), lambda b, p, i: (b, i, p)),
                  pl.BlockSpec((1, S, LANES), lambda b, p, i: (b, 0, n_pairs + p), pipeline_mode=one),
                  pl.BlockSpec((1, LANES, S), lambda b, p, i: (b, p, 0), pipeline_mode=one),
                  pl.BlockSpec((1, 1, S, HEADS_PER_VREG), lambda b, p, i: (b, p, 0, 0), pipeline_mode=one)],
        out_specs=pl.BlockSpec((1, tq, LANES), lambda b, p, i: (b, i, p)),
        out_shape=jax.ShapeDtypeStruct((B, S, W), F32),
        scratch_shapes=[pltpu.VMEM((S, LANES), BF16), pltpu.VMEM((S, LANES), BF16), pltpu.VMEM((LANES, S), BF16)],
        compiler_params=_params(3), name="fox_prompt")(qk, qk, vt, cc)


def _moba_kernel(q_ref, k_ref, vt_ref, o_ref, ka0_ref, ka1_ref, vb_ref, kmp_ref, *, tq, tk):
    qi = pl.program_id(2)
    S = k_ref.shape[1]
    blk = MOBA_BLOCK
    ka_refs = (ka0_ref, ka1_ref)

    @pl.when(qi == 0)
    def _():
        vb_ref[...] = vt_ref[0].astype(BF16)
        kmp_ref[...] = jnp.zeros_like(kmp_ref)
        lane, in_head = _lane_masks(blk)
        blk_lane = lane & (HEAD_DIM - 1)

        def body(j, _):
            rs = pl.multiple_of(j * blk, blk)
            kc = k_ref[0, pl.ds(rs, blk), :]
            onehot = jnp.where(blk_lane == j, 1.0, 0.0)
            for h in range(HEADS_PER_VREG):
                ka_refs[h][pl.ds(rs, blk), :] = jnp.where(in_head[h], kc, onehot).astype(BF16)
            kmean = jnp.sum(kc, axis=0, keepdims=True) * (1.0 / blk)
            h0row = in_head[0][0:1, :]
            kmp_ref[pl.ds(HEAD_DIM + j, 1), :] = jnp.where(h0row, kmean, 0.0)
            kmp_ref[pl.ds(j, 1), :] = jnp.where(h0row, 0.0, kmean)
            return 0

        lax.fori_loop(0, S // blk, body, 0)

    lane, in_head = _lane_masks(tq)
    blk_lane = lane & (HEAD_DIM - 1)
    lane_f = lane.astype(F32)
    q = q_ref[0]
    gate = lax.dot_general(q, kmp_ref[...], NT_DIMS, precision=HIGHEST, preferred_element_type=F32)
    qpos = qi * tq + lax.broadcasted_iota(jnp.int32, (tq, LANES), 0)
    own = lax.shift_right_logical(qpos, MOBA_BLOCK_SHIFT)
    qsc = q * (ATTN_SCALE * LOG2E)
    q_aug = []
    for h in range(HEADS_PER_VREG):
        spare = in_head[1 - h]
        valid = spare & (blk_lane < own)
        g = jnp.where(valid, gate, -jnp.inf)
        sel = spare & (blk_lane == own)
        for _ in range(MOBA_TOPK):
            mx = jnp.max(g, axis=-1, keepdims=True)
            first = jnp.min(jnp.where(g == mx, lane_f, float(LANES)), axis=-1, keepdims=True)
            pick = (lane_f == first) & valid
            sel = sel | pick
            g = jnp.where(pick, -jnp.inf, g)
        bias = jnp.where(sel, 0.0, NEG_BIG)
        q_aug.append(jnp.where(in_head[h], qsc, bias).astype(BF16))
    o_ref[0] = _flash_core(ka_refs, vb_ref, q_aug, qi, tq, tk)


def _moba_prompt(qk, vt, tq, tk):
    B, S, W2 = qk.shape
    W = W2 // 2
    n_pairs = W // LANES
    assert S % MOBA_BLOCK == 0 and S // MOBA_BLOCK <= HEAD_DIM
    assert tk % MOBA_BLOCK == 0 and tq % tk == 0
    one = pl.Buffered(1)
    return pl.pallas_call(
        functools.partial(_moba_kernel, tq=tq, tk=tk), grid=(B, n_pairs, S // tq),
        in_specs=[pl.BlockSpec((1, tq, LANES), lambda b, p, i: (b, i, p)),
                  pl.BlockSpec((1, S, LANES), lambda b, p, i: (b, 0, n_pairs + p), pipeline_mode=one),
                  pl.BlockSpec((1, LANES, S), lambda b, p, i: (b, p, 0), pipeline_mode=one)],
        out_specs=pl.BlockSpec((1, tq, LANES), lambda b, p, i: (b, i, p)),
        out_shape=jax.ShapeDtypeStruct((B, S, W), F32),
        scratch_shapes=[pltpu.VMEM((S, LANES), BF16), pltpu.VMEM((S, LANES), BF16),
                        pltpu.VMEM((LANES, S), BF16), pltpu.VMEM((LANES, LANES), F32)],
        compiler_params=_params(3), name="moba_prompt")(qk, qk, vt)


def _dil_kernel(q_ref, kp_ref, kc_ref, vp_ref, vc_ref, o_ref, e_ref, *, tq):
    t = pl.program_id(3)
    sub = DIL_KEYS
    _, (is_h0, _) = _lane_masks(sub)
    a = lax.broadcasted_iota(jnp.int32, (sub, 2 * sub), 0)
    c = lax.broadcasted_iota(jnp.int32, (sub, 2 * sub), 1)
    band = (c >= a) & (c <= a + sub)
    band_first = band & (c >= jnp.where(t > 0, 0, sub))
    for u in range(tq // sub):
        q = q_ref[0, u * sub:(u + 1) * sub, :] * ATTN_SCALE
        if u == 0:
            kk = jnp.concatenate([kp_ref[0], kc_ref[0, 0:sub, :]], axis=0)
            vv = jnp.concatenate([vp_ref[0], vc_ref[0, 0:sub, :]], axis=0)
            mask = band_first
        else:
            kk = kc_ref[0, (u - 1) * sub:(u + 1) * sub, :]
            vv = vc_ref[0, (u - 1) * sub:(u + 1) * sub, :]
            mask = band
        kk = kk.astype(BF16)
        vv = vv.astype(BF16)
        outs, lses = [], []
        for h in range(HEADS_PER_VREG):
            qh = jnp.where(is_h0, q, 0.0) if h == 0 else jnp.where(is_h0, 0.0, q)
            s = lax.dot_general(qh.astype(BF16), kk, NT_DIMS, preferred_element_type=F32)
            s = jnp.where(mask, s, -jnp.inf)
            m = jnp.max(s, axis=-1, keepdims=True)
            p = jnp.exp(s - m)
            l = jnp.sum(p, axis=-1, keepdims=True)
            outs.append(jnp.dot(p.astype(BF16), vv, preferred_element_type=F32) / l)
            lses.append(m + jnp.log(l))
        o_ref[0, u * sub:(u + 1) * sub, :] = jnp.where(is_h0, outs[0], outs[1])
        e_ref[0, u * sub:(u + 1) * sub, :] = jnp.where(is_h0, lses[0], lses[1])


def _dil_branch(qk, v, dil, tq):
    B, S, W = v.shape
    n_pairs = W // LANES
    rows = S // dil
    qkv = qk.reshape(B, rows, dil * 2 * W)
    vv = v.reshape(B, rows, dil * W)
    ratio = tq // DIL_KEYS

    def cur(width_blocks, offset):
        return pl.BlockSpec((1, tq, LANES), lambda b, r, p, t: (b, t, r * width_blocks + offset + p))

    def prev(width_blocks, offset):
        return pl.BlockSpec((1, DIL_KEYS, LANES),
                            lambda b, r, p, t: (b, jnp.maximum(t * ratio - 1, 0), r * width_blocks + offset + p))

    shape = jax.ShapeDtypeStruct((B, rows, dil * W), F32)
    o, e = pl.pallas_call(
        functools.partial(_dil_kernel, tq=tq), grid=(B, dil, n_pairs, rows // tq),
        in_specs=[cur(2 * n_pairs, 0), prev(2 * n_pairs, n_pairs), cur(2 * n_pairs, n_pairs),
                  prev(n_pairs, 0), cur(n_pairs, 0)],
        out_specs=[cur(n_pairs, 0), cur(n_pairs, 0)], out_shape=[shape, shape],
        compiler_params=_params(4), name="dil_prompt")(qkv, qkv, qkv, vv, vv)
    return o.reshape(B, S, W), e.reshape(B, S, W)


def _segment_matrix(width, seg, dtype):
    r = jnp.arange(width) // seg
    return (r[:, None] == r[None, :]).astype(dtype)


def _mem_sample_kernel(q_ref, seg_ref, k_ref, v_ref, o_ref):
    qs = q_ref[0] * MEM_SCALE
    s = jnp.dot((k_ref[0] * qs).astype(BF16), seg_ref[...], preferred_element_type=F32)
    p = jnp.exp(s - jnp.max(s, axis=0, keepdims=True))
    o_ref[0] = jnp.sum(p * v_ref[0], axis=0, keepdims=True) / jnp.sum(p, axis=0, keepdims=True)


def _mem_sample(q, mk, mv):
    NS, D = q.shape
    n_mem = mk.shape[1]
    seg = _segment_matrix(D, MEM_HEAD_DIM, BF16)
    vec = pl.BlockSpec((1, 1, D), lambda n: (n, 0, 0))
    blk = pl.BlockSpec((1, n_mem, D), lambda n: (n, 0, 0))
    out = pl.pallas_call(
        _mem_sample_kernel, grid=(NS,), in_specs=[vec, _const_spec((D, D)), blk, blk], out_specs=vec,
        out_shape=jax.ShapeDtypeStruct((NS, 1, D), F32),
        compiler_params=_params(1), name="mem_sample")(q.reshape(NS, 1, D), seg, mk, mv)
    return out.reshape(NS, D)


def _head_columns(x, n_heads):
    return x.reshape(x.shape[0], n_heads, HEAD_DIM, 1)


def _as_page(cols, page):
    pad = [(0, 0)] * (cols.ndim - 1) + [(0, page - 1)]
    return jnp.pad(cols, pad)


def _page_scores(kt, q_cols):
    return jnp.concatenate([jnp.sum(kt[h] * q_cols[h], axis=0, keepdims=True) for h in range(len(q_cols))], axis=0)


def _fox_sample_kernel(pt_ref, q_ref, *refs, pp):
    del pt_ref
    k_refs, v_refs, lf_refs = refs[:pp + 1], refs[pp + 1:2 * pp + 2], refs[2 * pp + 2:3 * pp + 3]
    o_ref, m_ref, l_ref, acc_ref, c_ref = refs[3 * pp + 3:]
    g = pl.program_id(1)
    H = q_ref.shape[1]
    R = k_refs[0].shape[-1]

    @pl.when(g == 0)
    def _():
        m_ref[...] = jnp.full_like(m_ref, -jnp.inf)
        l_ref[...] = jnp.zeros_like(l_ref)
        acc_ref[...] = jnp.zeros_like(acc_ref)
        c_ref[...] = jnp.zeros_like(c_ref)

    q_cols = [q_ref[0, h] * ATTN_SCALE for h in range(H)]
    row = lax.broadcasted_iota(jnp.int32, (R, R), 0)
    col = lax.broadcasted_iota(jnp.int32, (R, R), 1)
    upper = jnp.where(row <= col, 1.0, 0.0).astype(F32)

    def update(ks, vs, lfs, valid):
        c_run = c_ref[:, 0:1]
        ss = []
        for k_ref, lf_ref in zip(ks, lfs):
            c = jnp.dot(lf_ref[...], upper, precision=HIGHEST, preferred_element_type=F32) + c_run
            c_run = c[:, R - 1:R]
            s = _page_scores(k_ref[...], q_cols) - c
            ss.append(s if valid is None else jnp.where(valid, s, -jnp.inf))
        m_old = m_ref[:, 0:1]
        m_new = functools.reduce(jnp.maximum, [jnp.max(s, axis=1, keepdims=True) for s in ss], m_old)
        alpha = jnp.exp(m_old - m_new)
        ps = [jnp.exp(s - m_new) for s in ss]
        m_ref[...] = jnp.broadcast_to(m_new, m_ref.shape)
        c_ref[...] = jnp.broadcast_to(c_run, c_ref.shape)
        for h in range(H):
            l_ref[h:h + 1, :] = alpha[h:h + 1, :] * l_ref[h:h + 1, :] + sum(p[h:h + 1, :] for p in ps)
            acc_ref[h] = alpha[h:h + 1, :] * acc_ref[h] + sum(v_ref[h] * p[h:h + 1, :] for v_ref, p in zip(vs, ps))

    update(k_refs[:pp], v_refs[:pp], lf_refs[:pp], None)

    @pl.when(g == pl.num_programs(1) - 1)
    def _():
        update(k_refs[pp:], v_refs[pp:], lf_refs[pp:], lax.broadcasted_iota(jnp.int32, (H, R), 1) < 1)
        for h in range(H):
            o_ref[0, h] = (jnp.sum(acc_ref[h], axis=1, keepdims=True)
                           / jnp.sum(l_ref[h:h + 1, :], axis=1, keepdims=True))


def _fox_sample(q, k_new, v_new, lf_new, page_table, layer, pool_kt, pool_vt, pool_lft, pp):
    NS, W = q.shape
    n_pages = page_table.shape[1]
    _, _, H, _, page = pool_kt.shape

    def paged(tail, i):
        zeros = (0,) * len(tail)
        return pl.BlockSpec((None, None) + tail, lambda n, g, pt, i=i: (layer, pt[n, g * pp + i]) + zeros)

    kv_new = pl.BlockSpec((None, H, HEAD_DIM, page), lambda n, g, pt: (n, 0, 0, 0))
    lf_new_spec = pl.BlockSpec((None, H, page), lambda n, g, pt: (n, 0, 0))
    cols = pl.BlockSpec((1, H, HEAD_DIM, 1), lambda n, g, pt: (n, 0, 0, 0))
    in_specs = [cols]
    in_specs += [paged((H, HEAD_DIM, page), i) for i in range(pp)] + [kv_new]
    in_specs += [paged((H, HEAD_DIM, page), i) for i in range(pp)] + [kv_new]
    in_specs += [paged((H, page), i) for i in range(pp)] + [lf_new_spec]
    stat = pltpu.VMEM((H, page), F32)
    out = pl.pallas_call(
        functools.partial(_fox_sample_kernel, pp=pp),
        grid_spec=pltpu.PrefetchScalarGridSpec(
            num_scalar_prefetch=1, grid=(NS, n_pages // pp), in_specs=in_specs, out_specs=cols,
            scratch_shapes=[stat, stat, pltpu.VMEM((H, HEAD_DIM, page), F32), stat]),
        out_shape=jax.ShapeDtypeStruct((NS, H, HEAD_DIM, 1), F32),
        compiler_params=_params(2), name="fox_sample")(
            page_table, _head_columns(q, H),
            *([pool_kt] * pp), _as_page(_head_columns(k_new, H), page),
            *([pool_vt] * pp), _as_page(_head_columns(v_new, H), page),
            *([pool_lft] * pp), _as_page(lf_new.reshape(NS, H, 1), page))
    return out.reshape(NS, W)


def _moba_sample_kernel(pt_ref, q_ref, kn_ref, vn_ref, *refs, bps, n_blocks):
    del pt_ref
    k_refs, v_refs = refs[:2 * bps], refs[2 * bps:4 * bps]
    o_ref, m_ref, l_ref, gate_ref, acc_ref = refs[4 * bps:]
    g = pl.program_id(1)
    H = q_ref.shape[1]
    R = k_refs[0].shape[-1]
    lane = lax.broadcasted_iota(jnp.int32, (H, R), 1)
    lane_d = lax.broadcasted_iota(jnp.int32, (HEAD_DIM, R), 1)

    @pl.when(g == 0)
    def _():
        m_ref[...] = jnp.zeros_like(m_ref)
        l_ref[...] = jnp.zeros_like(l_ref)
        gate_ref[...] = jnp.zeros_like(gate_ref)
        acc_ref[...] = jnp.zeros_like(acc_ref)

    q_cols = [q_ref[0, h] * ATTN_SCALE for h in range(H)]
    ms, ls, gates, probs = m_ref[...], l_ref[...], gate_ref[...], []
    for i in range(bps):
        sa = _page_scores(k_refs[2 * i][...], q_cols)
        sb = _page_scores(k_refs[2 * i + 1][...], q_cols)
        m = jnp.maximum(jnp.max(sa, axis=1, keepdims=True), jnp.max(sb, axis=1, keepdims=True))
        pa, pb = jnp.exp(sa - m), jnp.exp(sb - m)
        l = jnp.sum(pa, axis=1, keepdims=True) + jnp.sum(pb, axis=1, keepdims=True)
        gate = (jnp.sum(sa, axis=1, keepdims=True) + jnp.sum(sb, axis=1, keepdims=True)) * (
            1.0 / (ATTN_SCALE * MOBA_BLOCK))
        hit = lane == g * bps + i
        ms, ls, gates = jnp.where(hit, m, ms), jnp.where(hit, l, ls), jnp.where(hit, gate, gates)
        probs.append((pa, pb))
    m_ref[...], l_ref[...], gate_ref[...] = ms, ls, gates
    for h in range(H):
        acc = acc_ref[h]
        for i, (pa, pb) in enumerate(probs):
            pv = v_refs[2 * i][h] * pa[h:h + 1, :] + v_refs[2 * i + 1][h] * pb[h:h + 1, :]
            acc = jnp.where(lane_d == g * bps + i, jnp.sum(pv, axis=1, keepdims=True), acc)
        acc_ref[h] = acc

    @pl.when(g == pl.num_programs(1) - 1)
    def _():
        lane_f = lane.astype(F32)
        valid = lane < n_blocks
        gates = jnp.where(valid, gate_ref[...], -jnp.inf)
        sel = jnp.zeros((H, R), jnp.bool_)
        for _ in range(MOBA_TOPK):
            mx = jnp.max(gates, axis=1, keepdims=True)
            first = jnp.min(jnp.where(gates == mx, lane_f, float(R)), axis=1, keepdims=True)
            pick = (lane_f == first) & valid
            sel = sel | pick
            gates = jnp.where(pick, -jnp.inf, gates)
        s_new = _page_scores(kn_ref[...], q_cols)[:, 0:1]
        ms = m_ref[...]
        m_fin = jnp.maximum(jnp.max(jnp.where(sel, ms, -jnp.inf), axis=1, keepdims=True), s_new)
        w = jnp.where(sel, jnp.exp(ms - m_fin), 0.0)
        p_new = jnp.exp(s_new - m_fin)
        l = jnp.sum(w * l_ref[...], axis=1, keepdims=True) + p_new
        for h in range(H):
            num = jnp.sum(acc_ref[h] * w[h:h + 1, :], axis=1, keepdims=True) + p_new[h:h + 1, :] * vn_ref[h][:, 0:1]
            o_ref[0, h] = num / l[h:h + 1, :]


def _moba_sample(q, k_new, v_new, page_table, layer, pool_kt, pool_vt, bps):
    NS, W = q.shape
    n_pages = page_table.shape[1]
    _, _, H, _, page = pool_kt.shape
    assert MOBA_BLOCK == 2 * page and n_pages % (2 * bps) == 0
    n_blocks = n_pages // 2
    assert MOBA_TOPK <= n_blocks <= page
    pages = [pl.BlockSpec((None, None, H, HEAD_DIM, page),
                          lambda n, g, pt, i=i: (layer, pt[n, g * 2 * bps + i], 0, 0, 0)) for i in range(2 * bps)]
    kv_new = pl.BlockSpec((None, H, HEAD_DIM, page), lambda n, g, pt: (n, 0, 0, 0))
    cols = pl.BlockSpec((1, H, HEAD_DIM, 1), lambda n, g, pt: (n, 0, 0, 0))
    stat = pltpu.VMEM((H, page), F32)
    out = pl.pallas_call(
        functools.partial(_moba_sample_kernel, bps=bps, n_blocks=n_blocks),
        grid_spec=pltpu.PrefetchScalarGridSpec(
            num_scalar_prefetch=1, grid=(NS, n_blocks // bps),
            in_specs=[cols, kv_new, kv_new] + pages * 2, out_specs=cols,
            scratch_shapes=[stat, stat, stat, pltpu.VMEM((H, HEAD_DIM, page), F32)]),
        out_shape=jax.ShapeDtypeStruct((NS, H, HEAD_DIM, 1), F32),
        compiler_params=_params(2), name="moba_sample")(
            page_table, _head_columns(q, H), _as_page(_head_columns(k_new, H), page),
            _as_page(_head_columns(v_new, H), page), *([pool_kt] * (2 * bps)), *([pool_vt] * (2 * bps)))
    return out.reshape(NS, W)


def _dil_sample_kernel(q_ref, kn_ref, vn_ref, k_ref, v_ref, o_ref):
    hb, _, L = k_ref.shape
    dist = L - lax.broadcasted_iota(jnp.int32, (1, L), 1)
    mult = jnp.zeros((1, L), F32)
    for win, dil in DIL_PAIRS:
        mult = mult + jnp.where(((dist & (dil - 1)) == 0) & (dist <= win), 1.0, 0.0)
    valid = mult > 0.0
    n_br = float(len(DIL_PAIRS))
    for h in range(hb):
        qc = q_ref[0, h] * ATTN_SCALE
        s = jnp.sum(k_ref[h] * qc, axis=0, keepdims=True)
        s_new = jnp.sum(kn_ref[0, h] * qc, axis=0, keepdims=True)
        m = jnp.maximum(jnp.max(jnp.where(valid, s, -jnp.inf), axis=1, keepdims=True), s_new)
        p = jnp.where(valid, jnp.exp(s - m), 0.0) * mult
        p_new = n_br * jnp.exp(s_new - m)
        l = jnp.sum(p, axis=1, keepdims=True) + p_new
        o_ref[0, h] = (jnp.sum(v_ref[h] * p, axis=1, keepdims=True) + p_new * vn_ref[0, h]) / l


def _dil_sample(q, k_new, v_new, layer, buf_kt, buf_vt, hb):
    NS, W = q.shape
    _, _, H, _, L = buf_kt.shape
    assert all(dil & (dil - 1) == 0 and win == DIL_KEYS * dil and win <= L for win, dil in DIL_PAIRS)
    cols = pl.BlockSpec((1, hb, HEAD_DIM, 1), lambda n, j: (n, j, 0, 0))
    buf = pl.BlockSpec((None, None, hb, HEAD_DIM, L), lambda n, j: (layer, n, j, 0, 0))
    out = pl.pallas_call(
        _dil_sample_kernel, grid=(NS, H // hb), in_specs=[cols, cols, cols, buf, buf], out_specs=cols,
        out_shape=jax.ShapeDtypeStruct((NS, H, HEAD_DIM, 1), F32),
        compiler_params=_params(2), name="dil_sample")(
            _head_columns(q, H), _head_columns(k_new, H), _head_columns(v_new, H), buf_kt, buf_vt)
    return out.reshape(NS, W)


TM_PROJ = 512
TN_PROJ = 512
TM_ROW = 256
TQ_FLASH = 1024
TK_FLASH = 512
TQ_DIL = 512
FOX_PAGES_PER_STEP = 8
MOBA_BLOCKS_PER_STEP = 4
DIL_HEADS_PER_STEP = 4


def _positions_last(x):
    n = x.ndim
    return jnp.transpose(x, tuple(range(n - 3)) + (n - 2, n - 1, n - 3))


def _heads_from_t(xt, n_heads):
    B, _, T = xt.shape
    return jnp.transpose(xt.reshape(B, n_heads, HEAD_DIM, T), (0, 3, 1, 2))


def kernel(x_prompt, x_sample, mem_prompt, page_table, cache_fox_k, cache_fox_v, cache_fox_logf, cache_moba_k, cache_moba_v, cache_dwin_k, cache_dwin_v, cache_mem_k, cache_mem_v, state_conv, w_in_even, b_forget, w_out_even, w_in_odd, w_out_odd, w_mem_q, w_mem_k, w_mem_v, w_mem_o, w_ffn_in, ffn_conv_w, ffn_conv_b, w_ffn_out, ln_g, ln_b):
    B, S, D = x_prompt.shape
    NS, T, _ = x_sample.shape
    assert T == 1 and D == D_MODEL
    page = cache_fox_k.shape[2]
    P = page_table.shape[1] * page
    H8 = b_forget.shape[1]
    W8 = H8 * HEAD_DIM
    H16 = D // HEAD_DIM
    n_mem = mem_prompt.shape[1]
    L_C = cache_dwin_k.shape[2]
    assert P >= L_C
    keep = min(DIL_PAIRS[-1][0], S)

    pos_p = jnp.arange(S, dtype=jnp.int32)
    rope_p = _rope_tables(pos_p)
    rope_pt = _rope_tables_t(pos_p)
    rope_s = tuple(jnp.broadcast_to(t, (NS, LANES)) for t in _rope_tables(jnp.full((1,), P, jnp.int32)))

    fox_kt, fox_vt, moba_kt, moba_vt, dwin_kt, dwin_vt = (
        _positions_last(t) for t in (cache_fox_k, cache_fox_v, cache_moba_k, cache_moba_v, cache_dwin_k, cache_dwin_v))
    fox_lft = jnp.swapaxes(cache_fox_logf, -1, -2)

    hp = x_prompt.reshape(B * S, D)
    hs = x_sample.reshape(NS, D)
    mem2 = mem_prompt.reshape(B * n_mem, D)
    out = {name: [] for name in (
        "fox_k_p", "fox_v_p", "fox_l_p", "moba_k_p", "moba_v_p", "dwin_k_p", "dwin_v_p", "mem_k_p", "mem_v_p",
        "conv_p", "fox_k_s", "fox_v_s", "fox_l_s", "moba_k_s", "moba_v_s", "dwin_k_s", "dwin_v_s", "conv_s")}

    for l in range(DEPTH):
        g, bta = ln_g[l], ln_b[l]
        if l % 2 == 0:
            e = l // 2
            w = w_in_even[e]
            cuts = [0, W8, 2 * W8, 3 * W8, 3 * W8 + H8, 4 * W8 + H8, 5 * W8 + H8, 6 * W8 + H8]
            w_qa, w_ka, w_va, w_f, w_qb, w_kb, w_vb = (w[:, cuts[i]:cuts[i + 1]] for i in range(7))
            w_fox = jnp.concatenate([w_qa, w_ka], axis=1).astype(BF16)
            w_moba = jnp.concatenate([w_qb, w_kb], axis=1).astype(BF16)
            wo = w_out_even[e].astype(BF16)
            qk_a = _proj(hp, w_fox, TM_PROJ, TN_PROJ)
            kt_a = _proj_t(hp, w_ka.T.astype(BF16), S, S, TM_PROJ, TN_PROJ)
            vt_a = _proj_t(hp, w_va.T.astype(BF16), S, S, TM_PROJ, TN_PROJ)
            lf, c = _forget(hp, w_f, b_forget[e], TM_ROW, S)
            qk_b = _proj(hp, w_moba, TM_PROJ, TN_PROJ, rope=rope_p)
            kt_b = _proj_t(hp, w_kb.T.astype(BF16), S, S, TM_PROJ, TN_PROJ, rope_t=rope_pt)
            vt_b = _proj_t(hp, w_vb.T.astype(BF16), S, S, TM_PROJ, TN_PROJ)
            oa = _fox_prompt(qk_a.reshape(B, S, 2 * W8), vt_a, c[:, :H8].reshape(B, S, H8), TQ_FLASH, TK_FLASH)
            ob = _moba_prompt(qk_b.reshape(B, S, 2 * W8), vt_b, TQ_FLASH, TK_FLASH)
            hp = _out_ln([oa.reshape(B * S, W8), ob.reshape(B * S, W8)], [wo[:W8], wo[W8:]], hp,
                         g[0], bta[0], TM_ROW)
            out["fox_k_p"].append(_heads_from_t(kt_a, H8))
            out["fox_v_p"].append(_heads_from_t(vt_a, H8))
            out["fox_l_p"].append(lf[:, :H8].reshape(B, S, H8))
            out["moba_k_p"].append(_heads_from_t(kt_b, H8))
            out["moba_v_p"].append(_heads_from_t(vt_b, H8))
            fox = _proj(hs, jnp.concatenate([w_fox, w_va.astype(BF16)], axis=1), NS, TN_PROJ)
            qa, ka, va = (fox[:, i * W8:(i + 1) * W8] for i in range(3))
            lf, _ = _forget(hs, w_f, b_forget[e], NS, NS)
            lf = lf[:, :H8]
            mqk = _proj(hs, w_moba, NS, TN_PROJ, rope=rope_s)
            qb, kb = (mqk[:, i * W8:(i + 1) * W8] for i in range(2))
            vb = _proj(hs, w_vb.astype(BF16), NS, TN_PROJ)
            oa = _fox_sample(qa, ka, va, lf, page_table, e, fox_kt, fox_vt, fox_lft, FOX_PAGES_PER_STEP)
            ob = _moba_sample(qb, kb, vb, page_table, e, moba_kt, moba_vt, MOBA_BLOCKS_PER_STEP)
            hs = _out_ln([oa, ob], [wo[:W8], wo[W8:]], hs, g[0], bta[0], NS)
            out["fox_k_s"].append(ka.reshape(NS, 1, H8, HEAD_DIM))
            out["fox_v_s"].append(va.reshape(NS, 1, H8, HEAD_DIM))
            out["fox_l_s"].append(lf.reshape(NS, 1, H8))
            out["moba_k_s"].append(kb.reshape(NS, 1, H8, HEAD_DIM))
            out["moba_v_s"].append(vb.reshape(NS, 1, H8, HEAD_DIM))
        else:
            od = l // 2
            w = w_in_odd[od]
            w_qk = w[:, :2 * D].astype(BF16)
            w_k, w_v = w[:, D:2 * D], w[:, 2 * D:]
            wo = w_out_odd[od].astype(BF16)
            qk = _proj(hp, w_qk, TM_PROJ, TN_PROJ, rope=rope_p).reshape(B, S, 2 * D)
            v = _proj(hp, w_v.astype(BF16), TM_PROJ, TN_PROJ).reshape(B, S, D)
            kt = _proj_t(hp, w_k.T.astype(BF16), S, keep, TM_PROJ, TN_PROJ, rope_t=rope_pt)
            vt = _proj_t(hp, w_v.T.astype(BF16), S, keep, TM_PROJ, TN_PROJ)
            branches = [_dil_branch(qk, v, dil, TQ_DIL) for _, dil in DIL_PAIRS]
            hp = _dil_merge_ln([o.reshape(B * S, D) for o, _ in branches],
                               [e_.reshape(B * S, D) for _, e_ in branches], wo, hp, g[0], bta[0], TM_ROW)
            out["dwin_k_p"].append(_heads_from_t(kt, H16))
            out["dwin_v_p"].append(_heads_from_t(vt, H16))
            qk = _proj(hs, w_qk, NS, TN_PROJ, rope=rope_s)
            q, k = qk[:, :D], qk[:, D:]
            v = _proj(hs, w_v.astype(BF16), NS, TN_PROJ)
            o = _dil_sample(q, k, v, od, dwin_kt, dwin_vt, DIL_HEADS_PER_STEP)
            hs = _out_ln([o], [wo], hs, g[0], bta[0], NS)
            out["dwin_k_s"].append(k.reshape(NS, 1, H16, HEAD_DIM))
            out["dwin_v_s"].append(v.reshape(NS, 1, H16, HEAD_DIM))
        wq, wk, wv, wo = (t[l].astype(BF16) for t in (w_mem_q, w_mem_k, w_mem_v, w_mem_o))
        mk = _proj(mem2, wk, TM_ROW, TN_PROJ).reshape(B, n_mem, D)
        mv = _proj(mem2, wv, TM_ROW, TN_PROJ).reshape(B, n_mem, D)
        out["mem_k_p"].append(mk.reshape(B, n_mem, N_MEM_HEADS, MEM_HEAD_DIM))
        out["mem_v_p"].append(mv.reshape(B, n_mem, N_MEM_HEADS, MEM_HEAD_DIM))
        hp = _mem_block(hp, wq, mk, mv, wo, g[1], bta[1], TM_ROW, S)
        qs_ = _proj(hs, wq, NS, TN_PROJ)
        os_ = _mem_sample(qs_, cache_mem_k[l].reshape(NS, n_mem, D), cache_mem_v[l].reshape(NS, n_mem, D))
        hs = _out_ln([os_], [wo], hs, g[1], bta[1], NS)
        win, wout = w_ffn_in[l].astype(BF16), w_ffn_out[l].astype(BF16)
        hp, cp = _ffn_prompt(hp, win, ffn_conv_w[l], ffn_conv_b[l], wout, g[2], bta[2], TM_ROW, S)
        st = state_conv[l]
        hs, u = _ffn_sample(hs, st[:, 0], st[:, 1], win, ffn_conv_w[l], ffn_conv_b[l], wout, g[2], bta[2])
        out["conv_p"].append(cp)
        out["conv_s"].append(jnp.stack([st[:, 1], u], axis=1))

    st = jnp.stack
    return (hp.reshape(B, S, D), hs.reshape(NS, 1, D),
            st(out["fox_k_p"]), st(out["fox_v_p"]), st(out["fox_l_p"]), st(out["moba_k_p"]), st(out["moba_v_p"]),
            st(out["dwin_k_p"]), st(out["dwin_v_p"]), st(out["mem_k_p"]), st(out["mem_v_p"]), st(out["conv_p"]),
            st(out["fox_k_s"]), st(out["fox_v_s"]), st(out["fox_l_s"]), st(out["moba_k_s"]), st(out["moba_v_s"]),
            st(out["dwin_k_s"]), st(out["dwin_v_s"]), st(out["conv_s"]))
```

```python
import functools
import math

import jax
import jax.numpy as jnp
from jax import lax
from jax.experimental import pallas as pl
from jax.experimental.pallas import tpu as pltpu

F32 = jnp.float32
BF16 = jnp.bfloat16

LANES = 128

D_MODEL = 1024
HEAD_DIM = 64
HEADS_PER_VREG = LANES // HEAD_DIM
N_MEM_HEADS = 4
MEM_HEAD_DIM = D_MODEL // N_MEM_HEADS
D_FF = 2816
FFN_CHUNK = 256
MOBA_BLOCK = 256
MOBA_BLOCK_SHIFT = 8
MOBA_TOPK = 3
DIL_PAIRS = ((128, 1), (512, 4), (2048, 16))
DIL_KEYS = 128
ROPE_THETA = 10000.0
LN_EPS = 1e-5
DEPTH = 4
DN_ALPHA = (2 * DEPTH) ** 0.25
ATTN_SCALE = HEAD_DIM ** -0.5
MEM_SCALE = MEM_HEAD_DIM ** -0.5
LOG2E = math.log2(math.e)
NEG_BIG = -1e30
VMEM_LIMIT = 56 * 2 ** 20

NT_DIMS = (((1,), (1,)), ((), ()))
HIGHEST = lax.Precision.HIGHEST


def _params(n_axes):
    return pltpu.CompilerParams(dimension_semantics=("arbitrary",) * n_axes,
                                vmem_limit_bytes=VMEM_LIMIT)


def _const_spec(shape):
    zeros = (0,) * len(shape)
    return pl.BlockSpec(shape, lambda *_: zeros, pipeline_mode=pl.Buffered(1))


def _layer_norm(y, g, b):
    mu = jnp.mean(y, axis=-1, keepdims=True)
    yc = y - mu
    var = jnp.mean(yc * yc, axis=-1, keepdims=True)
    return yc * lax.rsqrt(var + LN_EPS) * g + b


def _rope_angles(pos):
    half = HEAD_DIM // 2
    inv = ROPE_THETA ** (-jnp.arange(half, dtype=F32) / half)
    ang = pos.astype(F32)[:, None] * inv[None, :]
    return jnp.cos(ang), jnp.sin(ang)


def _rope_tables(pos):
    cos, sin = _rope_angles(pos)
    cos2 = jnp.tile(jnp.concatenate([cos, cos], axis=-1), (1, HEADS_PER_VREG))
    sin2 = jnp.tile(jnp.concatenate([-sin, sin], axis=-1), (1, HEADS_PER_VREG))
    return cos2, sin2


def _rope_tables_t(pos):
    cos, sin = _rope_angles(pos)
    return jnp.concatenate([cos, cos], axis=-1).T, jnp.concatenate([-sin, sin], axis=-1).T


def _proj_kernel(x_ref, w_ref, o_ref):
    o_ref[...] = jnp.dot(x_ref[...].astype(BF16), w_ref[...], preferred_element_type=F32)


def _proj_rope_kernel(x_ref, w_ref, cos_ref, sin_ref, o_ref):
    z = jnp.dot(x_ref[...].astype(BF16), w_ref[...], preferred_element_type=F32)
    tm, tn = z.shape
    cos, sin = cos_ref[...], sin_ref[...]
    lane = lax.broadcasted_iota(jnp.int32, (tm, LANES), 1)
    first_half = (lane & (HEAD_DIM - 1)) < HEAD_DIM // 2
    for c in range(tn // LANES):
        zc = z[:, c * LANES:(c + 1) * LANES]
        partner = jnp.where(first_half, pltpu.roll(zc, LANES - HEAD_DIM // 2, 1),
                            pltpu.roll(zc, HEAD_DIM // 2, 1))
        o_ref[:, c * LANES:(c + 1) * LANES] = zc * cos + partner * sin


def _proj(x, w, tm, tn, rope=None):
    M, K = x.shape
    N = w.shape[1]
    grid = (M // tm, N // tn)
    x_spec = pl.BlockSpec((tm, K), lambda i, j: (i, 0))
    w_spec = pl.BlockSpec((K, tn), lambda i, j: (0, j))
    o_spec = pl.BlockSpec((tm, tn), lambda i, j: (i, j))
    out_shape = jax.ShapeDtypeStruct((M, N), F32)
    if rope is None:
        return pl.pallas_call(_proj_kernel, grid=grid, in_specs=[x_spec, w_spec], out_specs=o_spec,
                              out_shape=out_shape, compiler_params=_params(2), name="proj")(x, w)
    cos2, sin2 = rope
    pos_blocks = cos2.shape[0] // tm
    t_spec = pl.BlockSpec((tm, LANES), lambda i, j: (i % pos_blocks, 0))
    return pl.pallas_call(_proj_rope_kernel, grid=grid, in_specs=[x_spec, w_spec, t_spec, t_spec],
                          out_specs=o_spec, out_shape=out_shape, compiler_params=_params(2),
                          name="proj_rope")(x, w, cos2, sin2)


def _proj_t_kernel(x_ref, wt_ref, o_ref):
    o_ref[0] = lax.dot_general(wt_ref[...], x_ref[...].astype(BF16), NT_DIMS, preferred_element_type=F32)


def _proj_t_rope_kernel(x_ref, wt_ref, cos_ref, sin_ref, o_ref):
    z = lax.dot_general(wt_ref[...], x_ref[...].astype(BF16), NT_DIMS, preferred_element_type=F32)
    cos, sin = cos_ref[...], sin_ref[...]
    half = HEAD_DIM // 2
    for g in range(z.shape[0] // HEAD_DIM):
        zg = z[g * HEAD_DIM:(g + 1) * HEAD_DIM]
        partner = jnp.concatenate([zg[half:], zg[:half]], axis=0)
        o_ref[0, g * HEAD_DIM:(g + 1) * HEAD_DIM, :] = zg * cos + partner * sin


def _proj_t(x, wt, seq_len, keep, tm, tn, rope_t=None):
    M, K = x.shape
    N = wt.shape[0]
    B = M // seq_len
    tiles_in, tiles_out = seq_len // tm, keep // tm
    first = tiles_in - tiles_out
    x_spec = pl.BlockSpec((tm, K), lambda i, j: ((i // tiles_out) * tiles_in + first + i % tiles_out, 0))
    w_spec = pl.BlockSpec((tn, K), lambda i, j: (j, 0))
    o_spec = pl.BlockSpec((1, tn, tm), lambda i, j: (i // tiles_out, j, i % tiles_out))
    out_shape = jax.ShapeDtypeStruct((B, N, keep), F32)
    grid = (B * tiles_out, N // tn)
    if rope_t is None:
        return pl.pallas_call(_proj_t_kernel, grid=grid, in_specs=[x_spec, w_spec], out_specs=o_spec,
                              out_shape=out_shape, compiler_params=_params(2), name="proj_t")(x, wt)
    t_spec = pl.BlockSpec((HEAD_DIM, tm), lambda i, j: (0, first + i % tiles_out))
    return pl.pallas_call(_proj_t_rope_kernel, grid=grid, in_specs=[x_spec, w_spec, t_spec, t_spec],
                          out_specs=o_spec, out_shape=out_shape, compiler_params=_params(2),
                          name="proj_t_rope")(x, wt, *rope_t)


def _forget_kernel(x_ref, w_ref, b_ref, lf_ref, c_ref, carry_ref, *, tiles_per_seq):
    i = pl.program_id(0)
    z = jnp.dot(x_ref[...].astype(BF16), w_ref[...], preferred_element_type=F32) + b_ref[...]
    lf = -(jnp.maximum(-z, 0.0) + jnp.log1p(jnp.exp(-jnp.abs(z))))
    lf_ref[...] = lf

    @pl.when(i % tiles_per_seq == 0)
    def _():
        carry_ref[...] = jnp.zeros_like(carry_ref)

    tm = lf.shape[0]
    row = lax.broadcasted_iota(jnp.int32, (tm, tm), 0)
    col = lax.broadcasted_iota(jnp.int32, (tm, tm), 1)
    tri = jnp.where(col <= row, 1.0, 0.0).astype(F32)
    c = jnp.dot(tri, lf, precision=HIGHEST, preferred_element_type=F32) + carry_ref[0:1, :]
    c_ref[...] = c
    carry_ref[0:1, :] = c[tm - 1:tm, :]


def _forget(x, w_f, b_f, tm, seq_len):
    M, K = x.shape
    n_heads = w_f.shape[1]
    w = jnp.zeros((K, LANES), BF16).at[:, :n_heads].set(w_f.astype(BF16))
    b = jnp.zeros((1, LANES), F32).at[0, :n_heads].set(b_f)
    kern = functools.partial(_forget_kernel, tiles_per_seq=seq_len // tm)
    out = jax.ShapeDtypeStruct((M, LANES), F32)
    return pl.pallas_call(
        kern, grid=(M // tm,),
        in_specs=[pl.BlockSpec((tm, K), lambda i: (i, 0)), _const_spec((K, LANES)), _const_spec((1, LANES))],
        out_specs=[pl.BlockSpec((tm, LANES), lambda i: (i, 0))] * 2,
        out_shape=[out, out], scratch_shapes=[pltpu.VMEM((8, LANES), F32)],
        compiler_params=_params(1), name="forget")(x, w, b)


def _out_ln_kernel(*refs, n_in):
    a_refs, w_refs = refs[:n_in], refs[n_in:2 * n_in]
    x_ref, g_ref, b_ref, o_ref = refs[2 * n_in:]
    acc = None
    for a_ref, w_ref in zip(a_refs, w_refs):
        d = jnp.dot(a_ref[...].astype(BF16), w_ref[...], preferred_element_type=F32)
        acc = d if acc is None else acc + d
    o_ref[...] = _layer_norm(DN_ALPHA * x_ref[...] + acc, g_ref[...], b_ref[...])


def _out_ln(a_list, w_list, x, g, b, tm):
    M, D = x.shape
    n_in = len(a_list)
    in_specs = [pl.BlockSpec((tm, a.shape[1]), lambda i: (i, 0)) for a in a_list]
    in_specs += [_const_spec(w.shape) for w in w_list]
    in_specs += [pl.BlockSpec((tm, D), lambda i: (i, 0)), _const_spec((1, D)), _const_spec((1, D))]
    return pl.pallas_call(
        functools.partial(_out_ln_kernel, n_in=n_in), grid=(M // tm,), in_specs=in_specs,
        out_specs=pl.BlockSpec((tm, D), lambda i: (i, 0)), out_shape=jax.ShapeDtypeStruct((M, D), F32),
        compiler_params=_params(1), name="out_ln")(*a_list, *w_list, x, g.reshape(1, D), b.reshape(1, D))


def _mem_kernel(x_ref, wq_ref, mk_ref, mv_ref, wo_ref, g_ref, b_ref, o_ref):
    x = x_ref[...]
    q = jnp.dot(x.astype(BF16), wq_ref[...], preferred_element_type=F32)
    mk = mk_ref[0].astype(BF16)
    mv = mv_ref[0].astype(BF16)
    heads = []
    for h in range(N_MEM_HEADS):
        sl = slice(h * MEM_HEAD_DIM, (h + 1) * MEM_HEAD_DIM)
        qh = (q[:, sl] * MEM_SCALE).astype(BF16)
        s = lax.dot_general(qh, mk[:, sl], NT_DIMS, preferred_element_type=F32)
        e = jnp.exp(s - jnp.max(s, axis=-1, keepdims=True))
        l = jnp.sum(e, axis=-1, keepdims=True)
        heads.append(jnp.dot(e.astype(BF16), mv[:, sl], preferred_element_type=F32) / l)
    o = jnp.concatenate(heads, axis=-1)
    mix = jnp.dot(o.astype(BF16), wo_ref[...], preferred_element_type=F32)
    o_ref[...] = _layer_norm(DN_ALPHA * x + mix, g_ref[...], b_ref[...])


def _mem_block(x, wq, mk, mv, wo, g, b, tm, seq_len):
    M, D = x.shape
    n_mem = mk.shape[1]
    tiles_per_seq = seq_len // tm
    row = pl.BlockSpec((tm, D), lambda i: (i, 0))
    mem = pl.BlockSpec((1, n_mem, D), lambda i: (i // tiles_per_seq, 0, 0))
    return pl.pallas_call(
        _mem_kernel, grid=(M // tm,),
        in_specs=[row, _const_spec((D, D)), mem, mem, _const_spec((D, D)),
                  _const_spec((1, D)), _const_spec((1, D))],
        out_specs=row, out_shape=jax.ShapeDtypeStruct((M, D), F32),
        compiler_params=_params(1), name="mem_block")(x, wq, mk, mv, wo, g.reshape(1, D), b.reshape(1, D))


def _gated(ha, hg):
    return ha * (0.5 * hg * (1.0 + lax.erf(hg * (2.0 ** -0.5))))


def _ffn_prompt_kernel(x_ref, win_ref, cw_ref, cb_ref, wout_ref, g_ref, b_ref, o_ref, st_ref, carry_ref,
                       *, tiles_per_seq):
    i = pl.program_id(0)

    @pl.when(i % tiles_per_seq == 0)
    def _():
        carry_ref[...] = jnp.zeros_like(carry_ref)

    x = x_ref[...]
    xb = x.astype(BF16)
    tm = x.shape[0]
    row = lax.broadcasted_iota(jnp.int32, (tm, FFN_CHUNK), 0)
    acc = jnp.zeros((tm, D_MODEL), F32)
    for c in range(D_FF // FFN_CHUNK):
        hs = []
        for off in (c * FFN_CHUNK, D_FF + c * FFN_CHUNK):
            sl = slice(off, off + FFN_CHUNK)
            u = jnp.dot(xb, win_ref[:, sl], preferred_element_type=F32)
            prev2 = carry_ref[0:1, sl]
            prev1 = carry_ref[1:2, sl]
            u1 = jnp.where(row == 0, prev1, pltpu.roll(u, 1, 0))
            u2 = jnp.where(row == 0, prev2, jnp.where(row == 1, prev1, pltpu.roll(u, 2, 0)))
            last = u[tm - 2:tm, :]
            carry_ref[0:2, sl] = last
            st_ref[0, :, sl] = last
            hs.append(cb_ref[:, sl] + cw_ref[0:1, sl] * u2 + cw_ref[1:2, sl] * u1 + cw_ref[2:3, sl] * u)
        act = _gated(hs[0], hs[1])
        acc = acc + jnp.dot(act.astype(BF16), wout_ref[c * FFN_CHUNK:(c + 1) * FFN_CHUNK, :],
                            preferred_element_type=F32)
    o_ref[...] = _layer_norm(DN_ALPHA * x + acc, g_ref[...], b_ref[...])


def _ffn_prompt(x, win, cw, cb, wout, g, b, tm, seq_len):
    M, D = x.shape
    F2 = win.shape[1]
    tiles_per_seq = seq_len // tm
    row = pl.BlockSpec((tm, D), lambda i: (i, 0))
    return pl.pallas_call(
        functools.partial(_ffn_prompt_kernel, tiles_per_seq=tiles_per_seq), grid=(M // tm,),
        in_specs=[row, _const_spec((D, F2)), _const_spec((3, F2)), _const_spec((1, F2)),
                  _const_spec((F2 // 2, D)), _const_spec((1, D)), _const_spec((1, D))],
        out_specs=[row, pl.BlockSpec((1, 2, F2), lambda i: (i // tiles_per_seq, 0, 0))],
        out_shape=[jax.ShapeDtypeStruct((M, D), F32), jax.ShapeDtypeStruct((M // seq_len, 2, F2), F32)],
        scratch_shapes=[pltpu.VMEM((8, F2), F32)],
        compiler_params=_params(1), name="ffn_prompt")(x, win, cw, cb.reshape(1, F2), wout,
                                                       g.reshape(1, D), b.reshape(1, D))


def _ffn_sample_kernel(x_ref, s0_ref, s1_ref, win_ref, cw_ref, cb_ref, wout_ref, g_ref, b_ref, o_ref, u_ref):
    x = x_ref[...]
    xb = x.astype(BF16)
    acc = jnp.zeros(x.shape, F32)
    for c in range(D_FF // FFN_CHUNK):
        hs = []
        for off in (c * FFN_CHUNK, D_FF + c * FFN_CHUNK):
            sl = slice(off, off + FFN_CHUNK)
            u = jnp.dot(xb, win_ref[:, sl], preferred_element_type=F32)
            u_ref[:, sl] = u
            hs.append(cb_ref[:, sl] + cw_ref[0:1, sl] * s0_ref[:, sl] + cw_ref[1:2, sl] * s1_ref[:, sl]
                      + cw_ref[2:3, sl] * u)
        act = _gated(hs[0], hs[1])
        acc = acc + jnp.dot(act.astype(BF16), wout_ref[c * FFN_CHUNK:(c + 1) * FFN_CHUNK, :],
                            preferred_element_type=F32)
    o_ref[...] = _layer_norm(DN_ALPHA * x + acc, g_ref[...], b_ref[...])


def _ffn_sample(x, s0, s1, win, cw, cb, wout, g, b):
    M, D = x.shape
    F2 = win.shape[1]
    return pl.pallas_call(
        _ffn_sample_kernel, grid=(1,),
        in_specs=[_const_spec((M, D)), _const_spec((M, F2)), _const_spec((M, F2)), _const_spec((D, F2)),
                  _const_spec((3, F2)), _const_spec((1, F2)), _const_spec((F2 // 2, D)),
                  _const_spec((1, D)), _const_spec((1, D))],
        out_specs=[pl.BlockSpec((M, D), lambda i: (0, 0)), pl.BlockSpec((M, F2), lambda i: (0, 0))],
        out_shape=[jax.ShapeDtypeStruct((M, D), F32), jax.ShapeDtypeStruct((M, F2), F32)],
        compiler_params=_params(1), name="ffn_sample")(x, s0, s1, win, cw, cb.reshape(1, F2), wout,
                                                       g.reshape(1, D), b.reshape(1, D))


def _lane_masks(rows):
    lane = lax.broadcasted_iota(jnp.int32, (rows, LANES), 1)
    is_h0 = lane < HEAD_DIM
    return lane, (is_h0, jnp.logical_not(is_h0))


def _flash_core(ka_refs, vb_ref, s_ref, q_aug, qi, tq, tk):
    assert tq == 2 * tk
    qs = qi * tq

    def scores(ks, slot):
        for h in range(HEADS_PER_VREG):
            s_ref[slot, h] = lax.dot_general(ka_refs[h][pl.ds(ks, tk), :], q_aug[h], NT_DIMS,
                                             preferred_element_type=F32)

    def consume(ks, slot, states, mask):
        new = []
        for h in range(HEADS_PER_VREG):
            m, l, acc = states[h]
            s = s_ref[slot, h]
            if mask is not None:
                s = jnp.where(mask, s, -jnp.inf)
            m_new = jnp.maximum(m, jnp.max(s, axis=0, keepdims=True))
            alpha = jnp.exp2(m - m_new)
            p = jnp.exp2(s - m_new)
            l = alpha * l + jnp.sum(p, axis=0, keepdims=True)
            vt = vb_ref[h * HEAD_DIM:(h + 1) * HEAD_DIM, pl.ds(ks, tk)]
            acc = alpha * acc + jnp.dot(vt, p.astype(BF16), preferred_element_type=F32)
            new.append((m_new, l, acc))
        return tuple(new)

    def chunk(j):
        return pl.multiple_of(j * tk, tk)

    def pair(jj, states):
        scores(chunk(2 * jj + 1), 1)
        states = consume(chunk(2 * jj), 0, states, None)
        scores(chunk(2 * jj + 2), 0)
        return consume(chunk(2 * jj + 1), 1, states, None)

    init = (jnp.full((1, tq), -jnp.inf, F32), jnp.zeros((1, tq), F32), jnp.zeros((HEAD_DIM, tq), F32))
    scores(0, 0)
    states = lax.fori_loop(0, qi, pair, (init, init))
    kpos = lax.broadcasted_iota(jnp.int32, (tk, tq), 0)
    qpos = lax.broadcasted_iota(jnp.int32, (tk, tq), 1)
    scores(chunk(2 * qi + 1), 1)
    states = consume(chunk(2 * qi), 0, states, kpos <= qpos)
    states = consume(chunk(2 * qi + 1), 1, states, kpos + tk <= qpos)
    (_, l0, a0), (_, l1, a1) = states
    return jnp.concatenate([a0 / l0, a1 / l1], axis=0).T


def _split3(x):
    hi = x.astype(BF16).astype(F32)
    r = x - hi
    mid = r.astype(BF16).astype(F32)
    lo = (r - mid).astype(BF16).astype(F32)
    return hi, mid, lo


def _place(lane, base, cols):
    out = jnp.zeros(lane.shape, F32)
    for i, v in enumerate(cols):
        out = jnp.where(lane == base + i, v, out)
    return out


def _fox_kernel(q_ref, k_ref, vt_ref, cc_ref, o_ref, ka0_ref, ka1_ref, vb_ref, s_ref, *, tq, tk):
    qi = pl.program_id(2)
    S = k_ref.shape[1]
    fill = MOBA_BLOCK
    ka_refs = (ka0_ref, ka1_ref)

    @pl.when(qi == 0)
    def _():
        vb_ref[...] = vt_ref[0].astype(BF16)
        lane, in_head = _lane_masks(fill)

        def body(j, _):
            rs = pl.multiple_of(j * fill, fill)
            kc = k_ref[0, pl.ds(rs, fill), :]
            cc = cc_ref[0, 0, pl.ds(rs, fill), :] * LOG2E
            for h in range(HEADS_PER_VREG):
                hi, mid, lo = _split3(cc[:, h:h + 1])
                one = jnp.ones_like(hi)
                extra = _place(lane, (1 - h) * HEAD_DIM, (-hi, -mid, -lo, one, one, one))
                ka_refs[h][pl.ds(rs, fill), :] = jnp.where(in_head[h], kc, extra).astype(BF16)
            return 0

        lax.fori_loop(0, S // fill, body, 0)

    lane, in_head = _lane_masks(tq)
    q = q_ref[0] * (ATTN_SCALE * LOG2E)
    cq = cc_ref[0, 0, pl.ds(pl.multiple_of(qi * tq, tq), tq), :] * LOG2E
    q_aug = []
    for h in range(HEADS_PER_VREG):
        hi, mid, lo = _split3(cq[:, h:h + 1])
        one = jnp.ones_like(hi)
        extra = _place(lane, (1 - h) * HEAD_DIM, (one, one, one, hi, mid, lo))
        q_aug.append(jnp.where(in_head[h], q, extra).astype(BF16))
    o_ref[0] = _flash_core(ka_refs, vb_ref, s_ref, q_aug, qi, tq, tk)


def _fox_prompt(qk, vt, c, tq, tk):
    B, S, W2 = qk.shape
    W = W2 // 2
    n_pairs = W // LANES
    cc = c.reshape(B, S, n_pairs, HEADS_PER_VREG).transpose(0, 2, 1, 3)
    one = pl.Buffered(1)
    return pl.pallas_call(
        functools.partial(_fox_kernel, tq=tq, tk=tk), grid=(B, n_pairs, S // tq),
        in_specs=[pl.BlockSpec((1, tq, LANES), lambda b, p, i: (b, i, p)),
                  pl.BlockSpec((1, S, LANES), lambda b, p, i: (b, 0, n_pairs + p), pipeline_mode=one),
                  pl.BlockSpec((1, LANES, S), lambda b, p, i: (b, p, 0), pipeline_mode=one),
                  pl.BlockSpec((1, 1, S, HEADS_PER_VREG), lambda b, p, i: (b, p, 0, 0), pipeline_mode=one)],
        out_specs=pl.BlockSpec((1, tq, LANES), lambda b, p, i: (b, i, p)),
        out_shape=jax.ShapeDtypeStruct((B, S, W), F32),
        scratch_shapes=[pltpu.VMEM((S, LANES), BF16), pltpu.VMEM((S, LANES), BF16), pltpu.VMEM((LANES, S), BF16),
                        pltpu.VMEM((2, HEADS_PER_VREG, tk, tq), F32)],
        compiler_params=_params(3), name="fox_prompt")(qk, qk, vt, cc)


def _moba_kernel(q_ref, k_ref, vt_ref, o_ref, ka0_ref, ka1_ref, vb_ref, kmp_ref, s_ref, *, tq, tk):
    qi = pl.program_id(2)
    S = k_ref.shape[1]
    blk = MOBA_BLOCK
    ka_refs = (ka0_ref, ka1_ref)

    @pl.when(qi == 0)
    def _():
        vb_ref[...] = vt_ref[0].astype(BF16)
        kmp_ref[...] = jnp.zeros_like(kmp_ref)
        lane, in_head = _lane_masks(blk)
        blk_lane = lane & (HEAD_DIM - 1)

        def body(j, _):
            rs = pl.multiple_of(j * blk, blk)
            kc = k_ref[0, pl.ds(rs, blk), :]
            onehot = jnp.where(blk_lane == j, 1.0, 0.0)
            for h in range(HEADS_PER_VREG):
                ka_refs[h][pl.ds(rs, blk), :] = jnp.where(in_head[h], kc, onehot).astype(BF16)
            kmean = jnp.sum(kc, axis=0, keepdims=True) * (1.0 / blk)
            h0row = in_head[0][0:1, :]
            kmp_ref[pl.ds(HEAD_DIM + j, 1), :] = jnp.where(h0row, kmean, 0.0)
            kmp_ref[pl.ds(j, 1), :] = jnp.where(h0row, 0.0, kmean)
            return 0

        lax.fori_loop(0, S // blk, body, 0)

    lane, in_head = _lane_masks(tq)
    blk_lane = lane & (HEAD_DIM - 1)
    lane_f = lane.astype(F32)
    q = q_ref[0]
    gate = lax.dot_general(q, kmp_ref[...], NT_DIMS, precision=HIGHEST, preferred_element_type=F32)
    qpos = qi * tq + lax.broadcasted_iota(jnp.int32, (tq, LANES), 0)
    own = lax.shift_right_logical(qpos, MOBA_BLOCK_SHIFT)
    qsc = q * (ATTN_SCALE * LOG2E)
    q_aug = []
    for h in range(HEADS_PER_VREG):
        spare = in_head[1 - h]
        valid = spare & (blk_lane < own)
        g = jnp.where(valid, gate, -jnp.inf)
        sel = spare & (blk_lane == own)
        for _ in range(MOBA_TOPK):
            mx = jnp.max(g, axis=-1, keepdims=True)
            first = jnp.min(jnp.where(g == mx, lane_f, float(LANES)), axis=-1, keepdims=True)
            pick = (lane_f == first) & valid
            sel = sel | pick
            g = jnp.where(pick, -jnp.inf, g)
        bias = jnp.where(sel, 0.0, NEG_BIG)
        q_aug.append(jnp.where(in_head[h], qsc, bias).astype(BF16))
    o_ref[0] = _flash_core(ka_refs, vb_ref, s_ref, q_aug, qi, tq, tk)


def _moba_prompt(qk, vt, tq, tk):
    B, S, W2 = qk.shape
    W = W2 // 2
    n_pairs = W // LANES
    assert S % MOBA_BLOCK == 0 and S // MOBA_BLOCK <= HEAD_DIM
    assert tk % MOBA_BLOCK == 0 and tq % tk == 0
    one = pl.Buffered(1)
    return pl.pallas_call(
        functools.partial(_moba_kernel, tq=tq, tk=tk), grid=(B, n_pairs, S // tq),
        in_specs=[pl.BlockSpec((1, tq, LANES), lambda b, p, i: (b, i, p)),
                  pl.BlockSpec((1, S, LANES), lambda b, p, i: (b, 0, n_pairs + p), pipeline_mode=one),
                  pl.BlockSpec((1, LANES, S), lambda b, p, i: (b, p, 0), pipeline_mode=one)],
        out_specs=pl.BlockSpec((1, tq, LANES), lambda b, p, i: (b, i, p)),
        out_shape=jax.ShapeDtypeStruct((B, S, W), F32),
        scratch_shapes=[pltpu.VMEM((S, LANES), BF16), pltpu.VMEM((S, LANES), BF16),
                        pltpu.VMEM((LANES, S), BF16), pltpu.VMEM((LANES, LANES), F32),
                        pltpu.VMEM((2, HEADS_PER_VREG, tk, tq), F32)],
        compiler_params=_params(3), name="moba_prompt")(qk, qk, vt)


def _class_rows(ref, dil, cls, first, count):
    if dil == 1:
        return ref[0, first:first + count, :]
    return ref[0, pl.ds(cls + dil * first, count, stride=dil), :]


def _dil_kernel(q_ref, kp_ref, kc_ref, vp_ref, vc_ref, o_ref, ob_ref, eb_ref):
    t = pl.program_id(2)
    tile = q_ref.shape[1]
    sub = DIL_KEYS
    _, (is_h0, _) = _lane_masks(sub)
    a = lax.broadcasted_iota(jnp.int32, (sub, 2 * sub), 0)
    c = lax.broadcasted_iota(jnp.int32, (sub, 2 * sub), 1)
    band = (c >= a) & (c <= a + sub)
    band_first = band & (c >= jnp.where(t > 0, 0, sub))
    for bi, (_, dil) in enumerate(DIL_PAIRS):
        n = tile // dil
        for cls in range(dil):
            for u in range(n // sub):
                q = _class_rows(q_ref, dil, cls, u * sub, sub) * ATTN_SCALE
                if u == 0:
                    kk = jnp.concatenate([_class_rows(kp_ref, dil, cls, n - sub, sub),
                                          _class_rows(kc_ref, dil, cls, 0, sub)], axis=0)
                    vv = jnp.concatenate([_class_rows(vp_ref, dil, cls, n - sub, sub),
                                          _class_rows(vc_ref, dil, cls, 0, sub)], axis=0)
                    mask = band_first
                else:
                    kk = _class_rows(kc_ref, dil, cls, (u - 1) * sub, 2 * sub)
                    vv = _class_rows(vc_ref, dil, cls, (u - 1) * sub, 2 * sub)
                    mask = band
                kk = kk.astype(BF16)
                vv = vv.astype(BF16)
                outs, lses = [], []
                for h in range(HEADS_PER_VREG):
                    qh = jnp.where(is_h0, q, 0.0) if h == 0 else jnp.where(is_h0, 0.0, q)
                    s = lax.dot_general(qh.astype(BF16), kk, NT_DIMS, preferred_element_type=F32)
                    s = jnp.where(mask, s, -jnp.inf)
                    m = jnp.max(s, axis=-1, keepdims=True)
                    p = jnp.exp(s - m)
                    l = jnp.sum(p, axis=-1, keepdims=True)
                    outs.append(jnp.dot(p.astype(BF16), vv, preferred_element_type=F32) / l)
                    lses.append(m + jnp.log(l))
                o_blk = jnp.where(is_h0, outs[0], outs[1])
                e_blk = jnp.where(is_h0, lses[0], lses[1])
                if dil == 1:
                    ob_ref[bi, u * sub:(u + 1) * sub, :] = o_blk
                    eb_ref[bi, u * sub:(u + 1) * sub, :] = e_blk
                else:
                    ob_ref[bi, pl.ds(cls + dil * u * sub, sub, stride=dil), :] = o_blk
                    eb_ref[bi, pl.ds(cls + dil * u * sub, sub, stride=dil), :] = e_blk
    n_br = len(DIL_PAIRS)
    es = [eb_ref[bi] for bi in range(n_br)]
    m = functools.reduce(jnp.maximum, es)
    ws = [jnp.exp(e - m) for e in es]
    o_ref[0] = sum(w * ob_ref[bi] for bi, w in enumerate(ws)) / sum(ws)


def _dil_prompt(qk, v, tile):
    B, S, W = v.shape
    n_pairs = W // LANES
    assert all(tile % (dil * DIL_KEYS) == 0 and win <= tile for win, dil in DIL_PAIRS) and S % tile == 0

    def cur(offset):
        return pl.BlockSpec((1, tile, LANES), lambda b, p, t: (b, t, offset + p))

    def prev(offset):
        return pl.BlockSpec((1, tile, LANES), lambda b, p, t: (b, jnp.maximum(t - 1, 0), offset + p))

    scratch = pltpu.VMEM((len(DIL_PAIRS), tile, LANES), F32)
    return pl.pallas_call(
        _dil_kernel, grid=(B, n_pairs, S // tile),
        in_specs=[cur(0), prev(n_pairs), cur(n_pairs), prev(0), cur(0)],
        out_specs=cur(0), out_shape=jax.ShapeDtypeStruct((B, S, W), F32), scratch_shapes=[scratch, scratch],
        compiler_params=_params(3), name="dil_prompt")(qk, qk, qk, v, v)


def _segment_matrix(width, seg, dtype):
    r = jnp.arange(width) // seg
    return (r[:, None] == r[None, :]).astype(dtype)


def _mem_sample_kernel(q_ref, seg_ref, k_ref, v_ref, o_ref):
    qs = q_ref[0] * MEM_SCALE
    s = jnp.dot((k_ref[0] * qs).astype(BF16), seg_ref[...], preferred_element_type=F32)
    p = jnp.exp(s - jnp.max(s, axis=0, keepdims=True))
    o_ref[0] = jnp.sum(p * v_ref[0], axis=0, keepdims=True) / jnp.sum(p, axis=0, keepdims=True)


def _mem_sample(q, mk, mv):
    NS, D = q.shape
    n_mem = mk.shape[1]
    seg = _segment_matrix(D, MEM_HEAD_DIM, BF16)
    vec = pl.BlockSpec((1, 1, D), lambda n: (n, 0, 0))
    blk = pl.BlockSpec((1, n_mem, D), lambda n: (n, 0, 0))
    out = pl.pallas_call(
        _mem_sample_kernel, grid=(NS,), in_specs=[vec, _const_spec((D, D)), blk, blk], out_specs=vec,
        out_shape=jax.ShapeDtypeStruct((NS, 1, D), F32),
        compiler_params=_params(1), name="mem_sample")(q.reshape(NS, 1, D), seg, mk, mv)
    return out.reshape(NS, D)


def _head_columns(x, n_heads):
    return x.reshape(x.shape[0], n_heads, HEAD_DIM, 1)


def _as_page(cols, page):
    pad = [(0, 0)] * (cols.ndim - 1) + [(0, page - 1)]
    return jnp.pad(cols, pad)


def _page_scores(kt, q_cols):
    return jnp.concatenate([jnp.sum(kt[h] * q_cols[h], axis=0, keepdims=True) for h in range(len(q_cols))], axis=0)


def _fox_sample_kernel(pt_ref, q_ref, *refs, pp):
    del pt_ref
    k_refs, v_refs, lf_refs = refs[:pp + 1], refs[pp + 1:2 * pp + 2], refs[2 * pp + 2:3 * pp + 3]
    o_ref, m_ref, l_ref, acc_ref, c_ref = refs[3 * pp + 3:]
    g = pl.program_id(1)
    H = q_ref.shape[1]
    R = k_refs[0].shape[-1]

    @pl.when(g == 0)
    def _():
        m_ref[...] = jnp.full_like(m_ref, -jnp.inf)
        l_ref[...] = jnp.zeros_like(l_ref)
        acc_ref[...] = jnp.zeros_like(acc_ref)
        c_ref[...] = jnp.zeros_like(c_ref)

    q_cols = [q_ref[0, h] * ATTN_SCALE for h in range(H)]
    row = lax.broadcasted_iota(jnp.int32, (R, R), 0)
    col = lax.broadcasted_iota(jnp.int32, (R, R), 1)
    upper = jnp.where(row <= col, 1.0, 0.0).astype(F32)

    def update(ks, vs, lfs, valid):
        c_run = c_ref[:, 0:1]
        ss = []
        for k_ref, lf_ref in zip(ks, lfs):
            c = jnp.dot(lf_ref[...], upper, precision=HIGHEST, preferred_element_type=F32) + c_run
            c_run = c[:, R - 1:R]
            s = _page_scores(k_ref[...], q_cols) - c
            ss.append(s if valid is None else jnp.where(valid, s, -jnp.inf))
        m_old = m_ref[:, 0:1]
        m_new = functools.reduce(jnp.maximum, [jnp.max(s, axis=1, keepdims=True) for s in ss], m_old)
        alpha = jnp.exp(m_old - m_new)
        ps = [jnp.exp(s - m_new) for s in ss]
        m_ref[...] = jnp.broadcast_to(m_new, m_ref.shape)
        c_ref[...] = jnp.broadcast_to(c_run, c_ref.shape)
        for h in range(H):
            l_ref[h:h + 1, :] = alpha[h:h + 1, :] * l_ref[h:h + 1, :] + sum(p[h:h + 1, :] for p in ps)
            acc_ref[h] = alpha[h:h + 1, :] * acc_ref[h] + sum(v_ref[h] * p[h:h + 1, :] for v_ref, p in zip(vs, ps))

    update(k_refs[:pp], v_refs[:pp], lf_refs[:pp], None)

    @pl.when(g == pl.num_programs(1) - 1)
    def _():
        update(k_refs[pp:], v_refs[pp:], lf_refs[pp:], lax.broadcasted_iota(jnp.int32, (H, R), 1) < 1)
        for h in range(H):
            o_ref[0, h] = (jnp.sum(acc_ref[h], axis=1, keepdims=True)
                           / jnp.sum(l_ref[h:h + 1, :], axis=1, keepdims=True))


def _fox_sample(q, k_new, v_new, lf_new, page_table, layer, pool_kt, pool_vt, pool_lft, pp):
    NS, W = q.shape
    n_pages = page_table.shape[1]
    _, _, H, _, page = pool_kt.shape

    def paged(tail, i):
        zeros = (0,) * len(tail)
        return pl.BlockSpec((None, None) + tail, lambda n, g, pt, i=i: (layer, pt[n, g * pp + i]) + zeros)

    kv_new = pl.BlockSpec((None, H, HEAD_DIM, page), lambda n, g, pt: (n, 0, 0, 0))
    lf_new_spec = pl.BlockSpec((None, H, page), lambda n, g, pt: (n, 0, 0))
    cols = pl.BlockSpec((1, H, HEAD_DIM, 1), lambda n, g, pt: (n, 0, 0, 0))
    in_specs = [cols]
    in_specs += [paged((H, HEAD_DIM, page), i) for i in range(pp)] + [kv_new]
    in_specs += [paged((H, HEAD_DIM, page), i) for i in range(pp)] + [kv_new]
    in_specs += [paged((H, page), i) for i in range(pp)] + [lf_new_spec]
    stat = pltpu.VMEM((H, page), F32)
    out = pl.pallas_call(
        functools.partial(_fox_sample_kernel, pp=pp),
        grid_spec=pltpu.PrefetchScalarGridSpec(
            num_scalar_prefetch=1, grid=(NS, n_pages // pp), in_specs=in_specs, out_specs=cols,
            scratch_shapes=[stat, stat, pltpu.VMEM((H, HEAD_DIM, page), F32), stat]),
        out_shape=jax.ShapeDtypeStruct((NS, H, HEAD_DIM, 1), F32),
        compiler_params=_params(2), name="fox_sample")(
            page_table, _head_columns(q, H),
            *([pool_kt] * pp), _as_page(_head_columns(k_new, H), page),
            *([pool_vt] * pp), _as_page(_head_columns(v_new, H), page),
            *([pool_lft] * pp), _as_page(lf_new.reshape(NS, H, 1), page))
    return out.reshape(NS, W)


def _moba_sample_kernel(pt_ref, q_ref, kn_ref, vn_ref, *refs, bps, n_blocks):
    del pt_ref
    k_refs, v_refs = refs[:2 * bps], refs[2 * bps:4 * bps]
    o_ref, m_ref, l_ref, gate_ref, acc_ref = refs[4 * bps:]
    g = pl.program_id(1)
    H = q_ref.shape[1]
    R = k_refs[0].shape[-1]
    lane = lax.broadcasted_iota(jnp.int32, (H, R), 1)
    lane_d = lax.broadcasted_iota(jnp.int32, (HEAD_DIM, R), 1)

    @pl.when(g == 0)
    def _():
        m_ref[...] = jnp.zeros_like(m_ref)
        l_ref[...] = jnp.zeros_like(l_ref)
        gate_ref[...] = jnp.zeros_like(gate_ref)
        acc_ref[...] = jnp.zeros_like(acc_ref)

    q_cols = [q_ref[0, h] * ATTN_SCALE for h in range(H)]
    ms, ls, gates, probs = m_ref[...], l_ref[...], gate_ref[...], []
    for i in range(bps):
        sa = _page_scores(k_refs[2 * i][...], q_cols)
        sb = _page_scores(k_refs[2 * i + 1][...], q_cols)
        m = jnp.maximum(jnp.max(sa, axis=1, keepdims=True), jnp.max(sb, axis=1, keepdims=True))
        pa, pb = jnp.exp(sa - m), jnp.exp(sb - m)
        l = jnp.sum(pa, axis=1, keepdims=True) + jnp.sum(pb, axis=1, keepdims=True)
        gate = (jnp.sum(sa, axis=1, keepdims=True) + jnp.sum(sb, axis=1, keepdims=True)) * (
            1.0 / (ATTN_SCALE * MOBA_BLOCK))
        hit = lane == g * bps + i
        ms, ls, gates = jnp.where(hit, m, ms), jnp.where(hit, l, ls), jnp.where(hit, gate, gates)
        probs.append((pa, pb))
    m_ref[...], l_ref[...], gate_ref[...] = ms, ls, gates
    for h in range(H):
        acc = acc_ref[h]
        for i, (pa, pb) in enumerate(probs):
            pv = v_refs[2 * i][h] * pa[h:h + 1, :] + v_refs[2 * i + 1][h] * pb[h:h + 1, :]
            acc = jnp.where(lane_d == g * bps + i, jnp.sum(pv, axis=1, keepdims=True), acc)
        acc_ref[h] = acc

    @pl.when(g == pl.num_programs(1) - 1)
    def _():
        lane_f = lane.astype(F32)
        valid = lane < n_blocks
        gates = jnp.where(valid, gate_ref[...], -jnp.inf)
        sel = jnp.zeros((H, R), jnp.bool_)
        for _ in range(MOBA_TOPK):
            mx = jnp.max(gates, axis=1, keepdims=True)
            first = jnp.min(jnp.where(gates == mx, lane_f, float(R)), axis=1, keepdims=True)
            pick = (lane_f == first) & valid
            sel = sel | pick
            gates = jnp.where(pick, -jnp.inf, gates)
        s_new = _page_scores(kn_ref[...], q_cols)[:, 0:1]
        ms = m_ref[...]
        m_fin = jnp.maximum(jnp.max(jnp.where(sel, ms, -jnp.inf), axis=1, keepdims=True), s_new)
        w = jnp.where(sel, jnp.exp(ms - m_fin), 0.0)
        p_new = jnp.exp(s_new - m_fin)
        l = jnp.sum(w * l_ref[...], axis=1, keepdims=True) + p_new
        for h in range(H):
            num = jnp.sum(acc_ref[h] * w[h:h + 1, :], axis=1, keepdims=True) + p_new[h:h + 1, :] * vn_ref[h][:, 0:1]
            o_ref[0, h] = num / l[h:h + 1, :]


def _moba_sample(q, k_new, v_new, page_table, layer, pool_kt, pool_vt, bps):
    NS, W = q.shape
    n_pages = page_table.shape[1]
    _, _, H, _, page = pool_kt.shape
    assert MOBA_BLOCK == 2 * page and n_pages % (2 * bps) == 0
    n_blocks = n_pages // 2
    assert MOBA_TOPK <= n_blocks <= page
    pages = [pl.BlockSpec((None, None, H, HEAD_DIM, page),
                          lambda n, g, pt, i=i: (layer, pt[n, g * 2 * bps + i], 0, 0, 0)) for i in range(2 * bps)]
    kv_new = pl.BlockSpec((None, H, HEAD_DIM, page), lambda n, g, pt: (n, 0, 0, 0))
    cols = pl.BlockSpec((1, H, HEAD_DIM, 1), lambda n, g, pt: (n, 0, 0, 0))
    stat = pltpu.VMEM((H, page), F32)
    out = pl.pallas_call(
        functools.partial(_moba_sample_kernel, bps=bps, n_blocks=n_blocks),
        grid_spec=pltpu.PrefetchScalarGridSpec(
            num_scalar_prefetch=1, grid=(NS, n_blocks // bps),
            in_specs=[cols, kv_new, kv_new] + pages * 2, out_specs=cols,
            scratch_shapes=[stat, stat, stat, pltpu.VMEM((H, HEAD_DIM, page), F32)]),
        out_shape=jax.ShapeDtypeStruct((NS, H, HEAD_DIM, 1), F32),
        compiler_params=_params(2), name="moba_sample")(
            page_table, _head_columns(q, H), _as_page(_head_columns(k_new, H), page),
            _as_page(_head_columns(v_new, H), page), *([pool_kt] * (2 * bps)), *([pool_vt] * (2 * bps)))
    return out.reshape(NS, W)


def _dil_sample_kernel(q_ref, kn_ref, vn_ref, k_ref, v_ref, o_ref):
    hb, _, L = k_ref.shape
    dist = L - lax.broadcasted_iota(jnp.int32, (1, L), 1)
    mult = jnp.zeros((1, L), F32)
    for win, dil in DIL_PAIRS:
        mult = mult + jnp.where(((dist & (dil - 1)) == 0) & (dist <= win), 1.0, 0.0)
    valid = mult > 0.0
    n_br = float(len(DIL_PAIRS))
    for h in range(hb):
        qc = q_ref[0, h] * ATTN_SCALE
        s = jnp.sum(k_ref[h] * qc, axis=0, keepdims=True)
        s_new = jnp.sum(kn_ref[0, h] * qc, axis=0, keepdims=True)
        m = jnp.maximum(jnp.max(jnp.where(valid, s, -jnp.inf), axis=1, keepdims=True), s_new)
        p = jnp.where(valid, jnp.exp(s - m), 0.0) * mult
        p_new = n_br * jnp.exp(s_new - m)
        l = jnp.sum(p, axis=1, keepdims=True) + p_new
        o_ref[0, h] = (jnp.sum(v_ref[h] * p, axis=1, keepdims=True) + p_new * vn_ref[0, h]) / l


def _dil_sample(q, k_new, v_new, layer, buf_kt, buf_vt, hb):
    NS, W = q.shape
    _, _, H, _, L = buf_kt.shape
    assert all(dil & (dil - 1) == 0 and win == DIL_KEYS * dil and win <= L for win, dil in DIL_PAIRS)
    cols = pl.BlockSpec((1, hb, HEAD_DIM, 1), lambda n, j: (n, j, 0, 0))
    buf = pl.BlockSpec((None, None, hb, HEAD_DIM, L), lambda n, j: (layer, n, j, 0, 0))
    out = pl.pallas_call(
        _dil_sample_kernel, grid=(NS, H // hb), in_specs=[cols, cols, cols, buf, buf], out_specs=cols,
        out_shape=jax.ShapeDtypeStruct((NS, H, HEAD_DIM, 1), F32),
        compiler_params=_params(2), name="dil_sample")(
            _head_columns(q, H), _head_columns(k_new, H), _head_columns(v_new, H), buf_kt, buf_vt)
    return out.reshape(NS, W)


TM_PROJ = 512
TN_PROJ = 512
TM_ROW = 256
TM_FFN = 512
TQ_FLASH = 1024
TK_FLASH = 512
DIL_TILE = 2048
FOX_PAGES_PER_STEP = 8
MOBA_BLOCKS_PER_STEP = 4
DIL_HEADS_PER_STEP = 4


def _positions_last(x):
    n = x.ndim
    return jnp.transpose(x, tuple(range(n - 3)) + (n - 2, n - 1, n - 3))


def _heads_from_t(xt, n_heads):
    B, _, T = xt.shape
    return jnp.transpose(xt.reshape(B, n_heads, HEAD_DIM, T), (0, 3, 1, 2))


def kernel(x_prompt, x_sample, mem_prompt, page_table, cache_fox_k, cache_fox_v, cache_fox_logf, cache_moba_k, cache_moba_v, cache_dwin_k, cache_dwin_v, cache_mem_k, cache_mem_v, state_conv, w_in_even, b_forget, w_out_even, w_in_odd, w_out_odd, w_mem_q, w_mem_k, w_mem_v, w_mem_o, w_ffn_in, ffn_conv_w, ffn_conv_b, w_ffn_out, ln_g, ln_b):
    B, S, D = x_prompt.shape
    NS, T, _ = x_sample.shape
    assert T == 1 and D == D_MODEL
    page = cache_fox_k.shape[2]
    P = page_table.shape[1] * page
    H8 = b_forget.shape[1]
    W8 = H8 * HEAD_DIM
    H16 = D // HEAD_DIM
    n_mem = mem_prompt.shape[1]
    L_C = cache_dwin_k.shape[2]
    assert P >= L_C
    keep = min(DIL_PAIRS[-1][0], S)

    pos_p = jnp.arange(S, dtype=jnp.int32)
    rope_p = _rope_tables(pos_p)
    rope_pt = _rope_tables_t(pos_p)
    rope_s = tuple(jnp.broadcast_to(t, (NS, LANES)) for t in _rope_tables(jnp.full((1,), P, jnp.int32)))

    fox_kt, fox_vt, moba_kt, moba_vt, dwin_kt, dwin_vt = (
        _positions_last(t) for t in (cache_fox_k, cache_fox_v, cache_moba_k, cache_moba_v, cache_dwin_k, cache_dwin_v))
    fox_lft = jnp.swapaxes(cache_fox_logf, -1, -2)

    hp = x_prompt.reshape(B * S, D)
    hs = x_sample.reshape(NS, D)
    mem2 = mem_prompt.reshape(B * n_mem, D)
    out = {name: [] for name in (
        "fox_k_p", "fox_v_p", "fox_l_p", "moba_k_p", "moba_v_p", "dwin_k_p", "dwin_v_p", "mem_k_p", "mem_v_p",
        "conv_p", "fox_k_s", "fox_v_s", "fox_l_s", "moba_k_s", "moba_v_s", "dwin_k_s", "dwin_v_s", "conv_s")}

    for l in range(DEPTH):
        g, bta = ln_g[l], ln_b[l]
        if l % 2 == 0:
            e = l // 2
            w = w_in_even[e]
            cuts = [0, W8, 2 * W8, 3 * W8, 3 * W8 + H8, 4 * W8 + H8, 5 * W8 + H8, 6 * W8 + H8]
            w_qa, w_ka, w_va, w_f, w_qb, w_kb, w_vb = (w[:, cuts[i]:cuts[i + 1]] for i in range(7))
            w_fox = jnp.concatenate([w_qa, w_ka], axis=1).astype(BF16)
            w_moba = jnp.concatenate([w_qb, w_kb], axis=1).astype(BF16)
            wo = w_out_even[e].astype(BF16)
            qk_a = _proj(hp, w_fox, TM_PROJ, TN_PROJ)
            kt_a = _proj_t(hp, w_ka.T.astype(BF16), S, S, TM_PROJ, TN_PROJ)
            vt_a = _proj_t(hp, w_va.T.astype(BF16), S, S, TM_PROJ, TN_PROJ)
            lf, c = _forget(hp, w_f, b_forget[e], TM_ROW, S)
            qk_b = _proj(hp, w_moba, TM_PROJ, TN_PROJ, rope=rope_p)
            kt_b = _proj_t(hp, w_kb.T.astype(BF16), S, S, TM_PROJ, TN_PROJ, rope_t=rope_pt)
            vt_b = _proj_t(hp, w_vb.T.astype(BF16), S, S, TM_PROJ, TN_PROJ)
            oa = _fox_prompt(qk_a.reshape(B, S, 2 * W8), vt_a, c[:, :H8].reshape(B, S, H8), TQ_FLASH, TK_FLASH)
            ob = _moba_prompt(qk_b.reshape(B, S, 2 * W8), vt_b, TQ_FLASH, TK_FLASH)
            hp = _out_ln([oa.reshape(B * S, W8), ob.reshape(B * S, W8)], [wo[:W8], wo[W8:]], hp,
                         g[0], bta[0], TM_ROW)
            out["fox_k_p"].append(_heads_from_t(kt_a, H8))
            out["fox_v_p"].append(_heads_from_t(vt_a, H8))
            out["fox_l_p"].append(lf[:, :H8].reshape(B, S, H8))
            out["moba_k_p"].append(_heads_from_t(kt_b, H8))
            out["moba_v_p"].append(_heads_from_t(vt_b, H8))
            fox = _proj(hs, jnp.concatenate([w_fox, w_va.astype(BF16)], axis=1), NS, TN_PROJ)
            qa, ka, va = (fox[:, i * W8:(i + 1) * W8] for i in range(3))
            lf, _ = _forget(hs, w_f, b_forget[e], NS, NS)
            lf = lf[:, :H8]
            mqk = _proj(hs, w_moba, NS, TN_PROJ, rope=rope_s)
            qb, kb = (mqk[:, i * W8:(i + 1) * W8] for i in range(2))
            vb = _proj(hs, w_vb.astype(BF16), NS, TN_PROJ)
            oa = _fox_sample(qa, ka, va, lf, page_table, e, fox_kt, fox_vt, fox_lft, FOX_PAGES_PER_STEP)
            ob = _moba_sample(qb, kb, vb, page_table, e, moba_kt, moba_vt, MOBA_BLOCKS_PER_STEP)
            hs = _out_ln([oa, ob], [wo[:W8], wo[W8:]], hs, g[0], bta[0], NS)
            out["fox_k_s"].append(ka.reshape(NS, 1, H8, HEAD_DIM))
            out["fox_v_s"].append(va.reshape(NS, 1, H8, HEAD_DIM))
            out["fox_l_s"].append(lf.reshape(NS, 1, H8))
            out["moba_k_s"].append(kb.reshape(NS, 1, H8, HEAD_DIM))
            out["moba_v_s"].append(vb.reshape(NS, 1, H8, HEAD_DIM))
        else:
            od = l // 2
            w = w_in_odd[od]
            w_qk = w[:, :2 * D].astype(BF16)
            w_k, w_v = w[:, D:2 * D], w[:, 2 * D:]
            wo = w_out_odd[od].astype(BF16)
            qk = _proj(hp, w_qk, TM_PROJ, TN_PROJ, rope=rope_p).reshape(B, S, 2 * D)
            v = _proj(hp, w_v.astype(BF16), TM_PROJ, TN_PROJ).reshape(B, S, D)
            kt = _proj_t(hp, w_k.T.astype(BF16), S, keep, TM_PROJ, TN_PROJ, rope_t=rope_pt)
            vt = _proj_t(hp, w_v.T.astype(BF16), S, keep, TM_PROJ, TN_PROJ)
            o = _dil_prompt(qk, v, DIL_TILE)
            hp = _out_ln([o.reshape(B * S, D)], [wo], hp, g[0], bta[0], TM_ROW)
            out["dwin_k_p"].append(_heads_from_t(kt, H16))
            out["dwin_v_p"].append(_heads_from_t(vt, H16))
            qk = _proj(hs, w_qk, NS, TN_PROJ, rope=rope_s)
            q, k = qk[:, :D], qk[:, D:]
            v = _proj(hs, w_v.astype(BF16), NS, TN_PROJ)
            o = _dil_sample(q, k, v, od, dwin_kt, dwin_vt, DIL_HEADS_PER_STEP)
            hs = _out_ln([o], [wo], hs, g[0], bta[0], NS)
            out["dwin_k_s"].append(k.reshape(NS, 1, H16, HEAD_DIM))
            out["dwin_v_s"].append(v.reshape(NS, 1, H16, HEAD_DIM))
        wq, wk, wv, wo = (t[l].astype(BF16) for t in (w_mem_q, w_mem_k, w_mem_v, w_mem_o))
        mk = _proj(mem2, wk, TM_ROW, TN_PROJ).reshape(B, n_mem, D)
        mv = _proj(mem2, wv, TM_ROW, TN_PROJ).reshape(B, n_mem, D)
        out["mem_k_p"].append(mk.reshape(B, n_mem, N_MEM_HEADS, MEM_HEAD_DIM))
        out["mem_v_p"].append(mv.reshape(B, n_mem, N_MEM_HEADS, MEM_HEAD_DIM))
        hp = _mem_block(hp, wq, mk, mv, wo, g[1], bta[1], TM_ROW, S)
        qs_ = _proj(hs, wq, NS, TN_PROJ)
        os_ = _mem_sample(qs_, cache_mem_k[l].reshape(NS, n_mem, D), cache_mem_v[l].reshape(NS, n_mem, D))
        hs = _out_ln([os_], [wo], hs, g[1], bta[1], NS)
        win, wout = w_ffn_in[l].astype(BF16), w_ffn_out[l].astype(BF16)
        hp, cp = _ffn_prompt(hp, win, ffn_conv_w[l], ffn_conv_b[l], wout, g[2], bta[2], TM_FFN, S)
        st = state_conv[l]
        hs, u = _ffn_sample(hs, st[:, 0], st[:, 1], win, ffn_conv_w[l], ffn_conv_b[l], wout, g[2], bta[2])
        out["conv_p"].append(cp)
        out["conv_s"].append(jnp.stack([st[:, 1], u], axis=1))

    st = jnp.stack
    return (hp.reshape(B, S, D), hs.reshape(NS, 1, D),
            st(out["fox_k_p"]), st(out["fox_v_p"]), st(out["fox_l_p"]), st(out["moba_k_p"]), st(out["moba_v_p"]),
            st(out["dwin_k_p"]), st(out["dwin_v_p"]), st(out["mem_k_p"]), st(out["mem_v_p"]), st(out["conv_p"]),
            st(out["fox_k_s"]), st(out["fox_v_s"]), st(out["fox_l_s"]), st(out["moba_k_s"]), st(out["moba_v_s"]),
            st(out["dwin_k_s"]), st(out["dwin_v_s"]), st(out["conv_s"]))
```

```python
import functools
import math

import jax
import jax.numpy as jnp
from jax import lax
from jax.experimental import pallas as pl
from jax.experimental.pallas import tpu as pltpu

F32 = jnp.float32
BF16 = jnp.bfloat16

LANES = 128

D_MODEL = 1024
HEAD_DIM = 64
HEADS_PER_VREG = LANES // HEAD_DIM
N_MEM_HEADS = 4
MEM_HEAD_DIM = D_MODEL // N_MEM_HEADS
D_FF = 2816
FFN_CHUNK = 256
MOBA_BLOCK = 256
MOBA_BLOCK_SHIFT = 8
MOBA_TOPK = 3
DIL_PAIRS = ((128, 1), (512, 4), (2048, 16))
DIL_KEYS = 128
ROPE_THETA = 10000.0
LN_EPS = 1e-5
DEPTH = 4
DN_ALPHA = (2 * DEPTH) ** 0.25
ATTN_SCALE = HEAD_DIM ** -0.5
MEM_SCALE = MEM_HEAD_DIM ** -0.5
LOG2E = math.log2(math.e)
NEG_BIG = -1e30
VMEM_LIMIT = 56 * 2 ** 20

NT_DIMS = (((1,), (1,)), ((), ()))
HIGHEST = lax.Precision.HIGHEST


def _params(n_axes):
    return pltpu.CompilerParams(dimension_semantics=("arbitrary",) * n_axes,
                                vmem_limit_bytes=VMEM_LIMIT)


def _const_spec(shape):
    zeros = (0,) * len(shape)
    return pl.BlockSpec(shape, lambda *_: zeros, pipeline_mode=pl.Buffered(1))


def _layer_norm(y, g, b):
    mu = jnp.mean(y, axis=-1, keepdims=True)
    yc = y - mu
    var = jnp.mean(yc * yc, axis=-1, keepdims=True)
    return yc * lax.rsqrt(var + LN_EPS) * g + b


def _rope_angles(pos):
    half = HEAD_DIM // 2
    inv = ROPE_THETA ** (-jnp.arange(half, dtype=F32) / half)
    ang = pos.astype(F32)[:, None] * inv[None, :]
    return jnp.cos(ang), jnp.sin(ang)


def _rope_tables(pos):
    cos, sin = _rope_angles(pos)
    cos2 = jnp.tile(jnp.concatenate([cos, cos], axis=-1), (1, HEADS_PER_VREG))
    sin2 = jnp.tile(jnp.concatenate([-sin, sin], axis=-1), (1, HEADS_PER_VREG))
    return cos2, sin2


def _rope_tables_t(pos):
    cos, sin = _rope_angles(pos)
    return jnp.concatenate([cos, cos], axis=-1).T, jnp.concatenate([-sin, sin], axis=-1).T


def _proj_kernel(x_ref, w_ref, o_ref):
    o_ref[...] = jnp.dot(x_ref[...].astype(BF16), w_ref[...], preferred_element_type=F32)


def _proj_rope_kernel(x_ref, w_ref, cos_ref, sin_ref, o_ref):
    z = jnp.dot(x_ref[...].astype(BF16), w_ref[...], preferred_element_type=F32)
    tm, tn = z.shape
    cos, sin = cos_ref[...], sin_ref[...]
    lane = lax.broadcasted_iota(jnp.int32, (tm, LANES), 1)
    first_half = (lane & (HEAD_DIM - 1)) < HEAD_DIM // 2
    for c in range(tn // LANES):
        zc = z[:, c * LANES:(c + 1) * LANES]
        partner = jnp.where(first_half, pltpu.roll(zc, LANES - HEAD_DIM // 2, 1),
                            pltpu.roll(zc, HEAD_DIM // 2, 1))
        o_ref[:, c * LANES:(c + 1) * LANES] = zc * cos + partner * sin


def _proj(x, w, tm, rope=None):
    M, K = x.shape
    N = w.shape[1]
    x_spec = pl.BlockSpec((tm, K), lambda i: (i, 0))
    o_spec = pl.BlockSpec((tm, N), lambda i: (i, 0))
    out_shape = jax.ShapeDtypeStruct((M, N), F32)
    if rope is None:
        return pl.pallas_call(_proj_kernel, grid=(M // tm,), in_specs=[x_spec, _const_spec((K, N))],
                              out_specs=o_spec, out_shape=out_shape, compiler_params=_params(1), name="proj")(x, w)
    cos2, sin2 = rope
    pos_blocks = cos2.shape[0] // tm
    t_spec = pl.BlockSpec((tm, LANES), lambda i: (i % pos_blocks, 0))
    return pl.pallas_call(_proj_rope_kernel, grid=(M // tm,), in_specs=[x_spec, _const_spec((K, N)), t_spec, t_spec],
                          out_specs=o_spec, out_shape=out_shape, compiler_params=_params(1),
                          name="proj_rope")(x, w, cos2, sin2)


def _proj_t_kernel(*refs, ropes):
    n = len(ropes)
    x_ref, wt_refs, cos_ref, sin_ref, o_refs = refs[0], refs[1:1 + n], refs[1 + n], refs[2 + n], refs[3 + n:]
    xb = x_ref[...].astype(BF16)
    half = HEAD_DIM // 2
    for wt_ref, o_ref, rope in zip(wt_refs, o_refs, ropes):
        z = lax.dot_general(wt_ref[...], xb, NT_DIMS, preferred_element_type=F32)
        if not rope:
            o_ref[0] = z
            continue
        cos, sin = cos_ref[...], sin_ref[...]
        for g in range(z.shape[0] // HEAD_DIM):
            zg = z[g * HEAD_DIM:(g + 1) * HEAD_DIM]
            partner = jnp.concatenate([zg[half:], zg[:half]], axis=0)
            o_ref[0, g * HEAD_DIM:(g + 1) * HEAD_DIM, :] = zg * cos + partner * sin


def _proj_t(x, wts, ropes, rope_t, seq_len, keep, tm):
    M, K = x.shape
    B = M // seq_len
    tiles_in, tiles_out = seq_len // tm, keep // tm
    first = tiles_in - tiles_out
    x_spec = pl.BlockSpec((tm, K), lambda i: ((i // tiles_out) * tiles_in + first + i % tiles_out, 0))
    t_spec = pl.BlockSpec((HEAD_DIM, tm), lambda i: (0, first + i % tiles_out))
    o_specs = [pl.BlockSpec((1, wt.shape[0], tm), lambda i: (i // tiles_out, 0, i % tiles_out)) for wt in wts]
    return pl.pallas_call(
        functools.partial(_proj_t_kernel, ropes=tuple(ropes)), grid=(B * tiles_out,),
        in_specs=[x_spec] + [_const_spec(wt.shape) for wt in wts] + [t_spec, t_spec], out_specs=o_specs,
        out_shape=[jax.ShapeDtypeStruct((B, wt.shape[0], keep), F32) for wt in wts],
        compiler_params=_params(1), name="proj_t")(x, *wts, *rope_t)


def _forget_kernel(x_ref, w_ref, b_ref, lf_ref, c_ref, carry_ref, *, tiles_per_seq):
    i = pl.program_id(0)
    z = jnp.dot(x_ref[...].astype(BF16), w_ref[...], preferred_element_type=F32) + b_ref[...]
    lf = -(jnp.maximum(-z, 0.0) + jnp.log1p(jnp.exp(-jnp.abs(z))))
    lf_ref[...] = lf

    @pl.when(i % tiles_per_seq == 0)
    def _():
        carry_ref[...] = jnp.zeros_like(carry_ref)

    tm = lf.shape[0]
    row = lax.broadcasted_iota(jnp.int32, (tm, tm), 0)
    col = lax.broadcasted_iota(jnp.int32, (tm, tm), 1)
    tri = jnp.where(col <= row, 1.0, 0.0).astype(F32)
    c = jnp.dot(tri, lf, precision=HIGHEST, preferred_element_type=F32) + carry_ref[0:1, :]
    c_ref[...] = c
    carry_ref[0:1, :] = c[tm - 1:tm, :]


def _forget(x, w_f, b_f, tm, seq_len):
    M, K = x.shape
    n_heads = w_f.shape[1]
    w = jnp.zeros((K, LANES), BF16).at[:, :n_heads].set(w_f.astype(BF16))
    b = jnp.zeros((1, LANES), F32).at[0, :n_heads].set(b_f)
    kern = functools.partial(_forget_kernel, tiles_per_seq=seq_len // tm)
    out = jax.ShapeDtypeStruct((M, LANES), F32)
    return pl.pallas_call(
        kern, grid=(M // tm,),
        in_specs=[pl.BlockSpec((tm, K), lambda i: (i, 0)), _const_spec((K, LANES)), _const_spec((1, LANES))],
        out_specs=[pl.BlockSpec((tm, LANES), lambda i: (i, 0))] * 2,
        out_shape=[out, out], scratch_shapes=[pltpu.VMEM((8, LANES), F32)],
        compiler_params=_params(1), name="forget")(x, w, b)


def _out_ln_kernel(*refs, n_in):
    a_refs, w_refs = refs[:n_in], refs[n_in:2 * n_in]
    x_ref, g_ref, b_ref, o_ref = refs[2 * n_in:]
    acc = None
    for a_ref, w_ref in zip(a_refs, w_refs):
        d = jnp.dot(a_ref[...].astype(BF16), w_ref[...], preferred_element_type=F32)
        acc = d if acc is None else acc + d
    o_ref[...] = _layer_norm(DN_ALPHA * x_ref[...] + acc, g_ref[...], b_ref[...])


def _out_ln(a_list, w_list, x, g, b, tm):
    M, D = x.shape
    n_in = len(a_list)
    in_specs = [pl.BlockSpec((tm, a.shape[1]), lambda i: (i, 0)) for a in a_list]
    in_specs += [_const_spec(w.shape) for w in w_list]
    in_specs += [pl.BlockSpec((tm, D), lambda i: (i, 0)), _const_spec((1, D)), _const_spec((1, D))]
    return pl.pallas_call(
        functools.partial(_out_ln_kernel, n_in=n_in), grid=(M // tm,), in_specs=in_specs,
        out_specs=pl.BlockSpec((tm, D), lambda i: (i, 0)), out_shape=jax.ShapeDtypeStruct((M, D), F32),
        compiler_params=_params(1), name="out_ln")(*a_list, *w_list, x, g.reshape(1, D), b.reshape(1, D))


def _mem_kernel(x_ref, wq_ref, mk_ref, mv_ref, wo_ref, g_ref, b_ref, o_ref):
    x = x_ref[...]
    q = jnp.dot(x.astype(BF16), wq_ref[...], preferred_element_type=F32)
    mk = mk_ref[0].astype(BF16)
    mv = mv_ref[0].astype(BF16)
    heads = []
    for h in range(N_MEM_HEADS):
        sl = slice(h * MEM_HEAD_DIM, (h + 1) * MEM_HEAD_DIM)
        qh = (q[:, sl] * MEM_SCALE).astype(BF16)
        s = lax.dot_general(qh, mk[:, sl], NT_DIMS, preferred_element_type=F32)
        e = jnp.exp(s - jnp.max(s, axis=-1, keepdims=True))
        l = jnp.sum(e, axis=-1, keepdims=True)
        heads.append(jnp.dot(e.astype(BF16), mv[:, sl], preferred_element_type=F32) / l)
    o = jnp.concatenate(heads, axis=-1)
    mix = jnp.dot(o.astype(BF16), wo_ref[...], preferred_element_type=F32)
    o_ref[...] = _layer_norm(DN_ALPHA * x + mix, g_ref[...], b_ref[...])


def _mem_block(x, wq, mk, mv, wo, g, b, tm, seq_len):
    M, D = x.shape
    n_mem = mk.shape[1]
    tiles_per_seq = seq_len // tm
    row = pl.BlockSpec((tm, D), lambda i: (i, 0))
    mem = pl.BlockSpec((1, n_mem, D), lambda i: (i // tiles_per_seq, 0, 0))
    return pl.pallas_call(
        _mem_kernel, grid=(M // tm,),
        in_specs=[row, _const_spec((D, D)), mem, mem, _const_spec((D, D)),
                  _const_spec((1, D)), _const_spec((1, D))],
        out_specs=row, out_shape=jax.ShapeDtypeStruct((M, D), F32),
        compiler_params=_params(1), name="mem_block")(x, wq, mk, mv, wo, g.reshape(1, D), b.reshape(1, D))


def _gated(ha, hg):
    return ha * (0.5 * hg * (1.0 + lax.erf(hg * (2.0 ** -0.5))))


def _ffn_prompt_kernel(x_ref, win_ref, cw_ref, cb_ref, wout_ref, g_ref, b_ref, o_ref, st_ref, carry_ref, act_ref,
                       *, tiles_per_seq):
    i = pl.program_id(0)

    @pl.when(i % tiles_per_seq == 0)
    def _():
        carry_ref[...] = jnp.zeros_like(carry_ref)

    x = x_ref[...]
    xb = x.astype(BF16)
    tm = x.shape[0]
    row = lax.broadcasted_iota(jnp.int32, (tm, FFN_CHUNK), 0)
    for c in range(D_FF // FFN_CHUNK):
        hs = []
        for off in (c * FFN_CHUNK, D_FF + c * FFN_CHUNK):
            sl = slice(off, off + FFN_CHUNK)
            u = jnp.dot(xb, win_ref[:, sl], preferred_element_type=F32)
            prev2 = carry_ref[0:1, sl]
            prev1 = carry_ref[1:2, sl]
            u1 = jnp.where(row == 0, prev1, pltpu.roll(u, 1, 0))
            u2 = jnp.where(row == 0, prev2, jnp.where(row == 1, prev1, pltpu.roll(u, 2, 0)))
            last = u[tm - 2:tm, :]
            carry_ref[0:2, sl] = last
            st_ref[0, :, sl] = last
            hs.append(cb_ref[:, sl] + cw_ref[0:1, sl] * u2 + cw_ref[1:2, sl] * u1 + cw_ref[2:3, sl] * u)
        act_ref[:, c * FFN_CHUNK:(c + 1) * FFN_CHUNK] = _gated(hs[0], hs[1]).astype(BF16)
    mix = jnp.dot(act_ref[...], wout_ref[...], preferred_element_type=F32)
    o_ref[...] = _layer_norm(DN_ALPHA * x + mix, g_ref[...], b_ref[...])


def _ffn_prompt(x, win, cw, cb, wout, g, b, tm, seq_len):
    M, D = x.shape
    F2 = win.shape[1]
    tiles_per_seq = seq_len // tm
    row = pl.BlockSpec((tm, D), lambda i: (i, 0))
    return pl.pallas_call(
        functools.partial(_ffn_prompt_kernel, tiles_per_seq=tiles_per_seq), grid=(M // tm,),
        in_specs=[row, _const_spec((D, F2)), _const_spec((3, F2)), _const_spec((1, F2)),
                  _const_spec((F2 // 2, D)), _const_spec((1, D)), _const_spec((1, D))],
        out_specs=[row, pl.BlockSpec((1, 2, F2), lambda i: (i // tiles_per_seq, 0, 0))],
        out_shape=[jax.ShapeDtypeStruct((M, D), F32), jax.ShapeDtypeStruct((M // seq_len, 2, F2), F32)],
        scratch_shapes=[pltpu.VMEM((8, F2), F32), pltpu.VMEM((tm, F2 // 2), BF16)],
        compiler_params=_params(1), name="ffn_prompt")(x, win, cw, cb.reshape(1, F2), wout,
                                                       g.reshape(1, D), b.reshape(1, D))


def _ffn_sample_kernel(x_ref, s0_ref, s1_ref, win_ref, cw_ref, cb_ref, wout_ref, g_ref, b_ref, o_ref, u_ref):
    x = x_ref[...]
    xb = x.astype(BF16)
    acc = jnp.zeros(x.shape, F32)
    for c in range(D_FF // FFN_CHUNK):
        hs = []
        for off in (c * FFN_CHUNK, D_FF + c * FFN_CHUNK):
            sl = slice(off, off + FFN_CHUNK)
            u = jnp.dot(xb, win_ref[:, sl], preferred_element_type=F32)
            u_ref[:, sl] = u
            hs.append(cb_ref[:, sl] + cw_ref[0:1, sl] * s0_ref[:, sl] + cw_ref[1:2, sl] * s1_ref[:, sl]
                      + cw_ref[2:3, sl] * u)
        act = _gated(hs[0], hs[1])
        acc = acc + jnp.dot(act.astype(BF16), wout_ref[c * FFN_CHUNK:(c + 1) * FFN_CHUNK, :],
                            preferred_element_type=F32)
    o_ref[...] = _layer_norm(DN_ALPHA * x + acc, g_ref[...], b_ref[...])


def _ffn_sample(x, s0, s1, win, cw, cb, wout, g, b):
    M, D = x.shape
    F2 = win.shape[1]
    return pl.pallas_call(
        _ffn_sample_kernel, grid=(1,),
        in_specs=[_const_spec((M, D)), _const_spec((M, F2)), _const_spec((M, F2)), _const_spec((D, F2)),
                  _const_spec((3, F2)), _const_spec((1, F2)), _const_spec((F2 // 2, D)),
                  _const_spec((1, D)), _const_spec((1, D))],
        out_specs=[pl.BlockSpec((M, D), lambda i: (0, 0)), pl.BlockSpec((M, F2), lambda i: (0, 0))],
        out_shape=[jax.ShapeDtypeStruct((M, D), F32), jax.ShapeDtypeStruct((M, F2), F32)],
        compiler_params=_params(1), name="ffn_sample")(x, s0, s1, win, cw, cb.reshape(1, F2), wout,
                                                       g.reshape(1, D), b.reshape(1, D))


def _lane_masks(rows):
    lane = lax.broadcasted_iota(jnp.int32, (rows, LANES), 1)
    is_h0 = lane < HEAD_DIM
    return lane, (is_h0, jnp.logical_not(is_h0))


def _flash_core(ka_refs, vb_ref, s_ref, q_aug, qi, tq, tk):
    assert tq == 2 * tk
    qs = qi * tq

    def scores(ks, slot):
        for h in range(HEADS_PER_VREG):
            s_ref[slot, h] = lax.dot_general(ka_refs[h][pl.ds(ks, tk), :], q_aug[h], NT_DIMS,
                                             preferred_element_type=F32)

    def consume(ks, slot, states, mask):
        new = []
        for h in range(HEADS_PER_VREG):
            m, l, acc = states[h]
            s = s_ref[slot, h]
            if mask is not None:
                s = jnp.where(mask, s, -jnp.inf)
            m_new = jnp.maximum(m, jnp.max(s, axis=0, keepdims=True))
            alpha = jnp.exp2(m - m_new)
            p = jnp.exp2(s - m_new)
            l = alpha * l + jnp.sum(p, axis=0, keepdims=True)
            vt = vb_ref[h * HEAD_DIM:(h + 1) * HEAD_DIM, pl.ds(ks, tk)]
            acc = alpha * acc + jnp.dot(vt, p.astype(BF16), preferred_element_type=F32)
            new.append((m_new, l, acc))
        return tuple(new)

    def chunk(j):
        return pl.multiple_of(j * tk, tk)

    def pair(jj, states):
        scores(chunk(2 * jj + 1), 1)
        states = consume(chunk(2 * jj), 0, states, None)
        scores(chunk(2 * jj + 2), 0)
        return consume(chunk(2 * jj + 1), 1, states, None)

    init = (jnp.full((1, tq), -jnp.inf, F32), jnp.zeros((1, tq), F32), jnp.zeros((HEAD_DIM, tq), F32))
    scores(0, 0)
    states = lax.fori_loop(0, qi, pair, (init, init))
    kpos = lax.broadcasted_iota(jnp.int32, (tk, tq), 0)
    qpos = lax.broadcasted_iota(jnp.int32, (tk, tq), 1)
    scores(chunk(2 * qi + 1), 1)
    states = consume(chunk(2 * qi), 0, states, kpos <= qpos)
    states = consume(chunk(2 * qi + 1), 1, states, kpos + tk <= qpos)
    (_, l0, a0), (_, l1, a1) = states
    return jnp.concatenate([a0 / l0, a1 / l1], axis=0).T


def _split3(x):
    hi = x.astype(BF16).astype(F32)
    r = x - hi
    mid = r.astype(BF16).astype(F32)
    lo = (r - mid).astype(BF16).astype(F32)
    return hi, mid, lo


def _place(lane, base, cols):
    out = jnp.zeros(lane.shape, F32)
    for i, v in enumerate(cols):
        out = jnp.where(lane == base + i, v, out)
    return out


def _fox_kernel(q_ref, k_ref, vt_ref, cc_ref, o_ref, ka0_ref, ka1_ref, vb_ref, s_ref, *, tq, tk):
    qi = pl.program_id(2)
    S = k_ref.shape[1]
    fill = MOBA_BLOCK
    ka_refs = (ka0_ref, ka1_ref)

    @pl.when(qi == 0)
    def _():
        vb_ref[...] = vt_ref[0].astype(BF16)
        lane, in_head = _lane_masks(fill)

        def body(j, _):
            rs = pl.multiple_of(j * fill, fill)
            kc = k_ref[0, pl.ds(rs, fill), :]
            cc = cc_ref[0, 0, pl.ds(rs, fill), :] * LOG2E
            for h in range(HEADS_PER_VREG):
                hi, mid, lo = _split3(cc[:, h:h + 1])
                one = jnp.ones_like(hi)
                extra = _place(lane, (1 - h) * HEAD_DIM, (-hi, -mid, -lo, one, one, one))
                ka_refs[h][pl.ds(rs, fill), :] = jnp.where(in_head[h], kc, extra).astype(BF16)
            return 0

        lax.fori_loop(0, S // fill, body, 0)

    lane, in_head = _lane_masks(tq)
    q = q_ref[0] * (ATTN_SCALE * LOG2E)
    cq = cc_ref[0, 0, pl.ds(pl.multiple_of(qi * tq, tq), tq), :] * LOG2E
    q_aug = []
    for h in range(HEADS_PER_VREG):
        hi, mid, lo = _split3(cq[:, h:h + 1])
        one = jnp.ones_like(hi)
        extra = _place(lane, (1 - h) * HEAD_DIM, (one, one, one, hi, mid, lo))
        q_aug.append(jnp.where(in_head[h], q, extra).astype(BF16))
    o_ref[0] = _flash_core(ka_refs, vb_ref, s_ref, q_aug, qi, tq, tk)


def _fox_prompt(qk, vt, c, tq, tk):
    B, S, W2 = qk.shape
    W = W2 // 2
    n_pairs = W // LANES
    cc = c.reshape(B, S, n_pairs, HEADS_PER_VREG).transpose(0, 2, 1, 3)
    one = pl.Buffered(1)
    return pl.pallas_call(
        functools.partial(_fox_kernel, tq=tq, tk=tk), grid=(B, n_pairs, S // tq),
        in_specs=[pl.BlockSpec((1, tq, LANES), lambda b, p, i: (b, i, p)),
                  pl.BlockSpec((1, S, LANES), lambda b, p, i: (b, 0, n_pairs + p), pipeline_mode=one),
                  pl.BlockSpec((1, LANES, S), lambda b, p, i: (b, p, 0), pipeline_mode=one),
                  pl.BlockSpec((1, 1, S, HEADS_PER_VREG), lambda b, p, i: (b, p, 0, 0), pipeline_mode=one)],
        out_specs=pl.BlockSpec((1, tq, LANES), lambda b, p, i: (b, i, p)),
        out_shape=jax.ShapeDtypeStruct((B, S, W), F32),
        scratch_shapes=[pltpu.VMEM((S, LANES), BF16), pltpu.VMEM((S, LANES), BF16), pltpu.VMEM((LANES, S), BF16),
                        pltpu.VMEM((2, HEADS_PER_VREG, tk, tq), F32)],
        compiler_params=_params(3), name="fox_prompt")(qk, qk, vt, cc)


def _moba_kernel(q_ref, k_ref, vt_ref, o_ref, ka0_ref, ka1_ref, vb_ref, kmp_ref, s_ref, *, tq, tk):
    qi = pl.program_id(2)
    S = k_ref.shape[1]
    blk = MOBA_BLOCK
    ka_refs = (ka0_ref, ka1_ref)

    @pl.when(qi == 0)
    def _():
        vb_ref[...] = vt_ref[0].astype(BF16)
        kmp_ref[...] = jnp.zeros_like(kmp_ref)
        lane, in_head = _lane_masks(blk)
        blk_lane = lane & (HEAD_DIM - 1)

        def body(j, _):
            rs = pl.multiple_of(j * blk, blk)
            kc = k_ref[0, pl.ds(rs, blk), :]
            onehot = jnp.where(blk_lane == j, 1.0, 0.0)
            for h in range(HEADS_PER_VREG):
                ka_refs[h][pl.ds(rs, blk), :] = jnp.where(in_head[h], kc, onehot).astype(BF16)
            kmean = jnp.sum(kc, axis=0, keepdims=True) * (1.0 / blk)
            h0row = in_head[0][0:1, :]
            kmp_ref[pl.ds(HEAD_DIM + j, 1), :] = jnp.where(h0row, kmean, 0.0)
            kmp_ref[pl.ds(j, 1), :] = jnp.where(h0row, 0.0, kmean)
            return 0

        lax.fori_loop(0, S // blk, body, 0)

    lane, in_head = _lane_masks(tq)
    blk_lane = lane & (HEAD_DIM - 1)
    lane_f = lane.astype(F32)
    q = q_ref[0]
    gate = lax.dot_general(q, kmp_ref[...], NT_DIMS, precision=HIGHEST, preferred_element_type=F32)
    qpos = qi * tq + lax.broadcasted_iota(jnp.int32, (tq, LANES), 0)
    own = lax.shift_right_logical(qpos, MOBA_BLOCK_SHIFT)
    qsc = q * (ATTN_SCALE * LOG2E)
    q_aug = []
    for h in range(HEADS_PER_VREG):
        spare = in_head[1 - h]
        valid = spare & (blk_lane < own)
        g = jnp.where(valid, gate, -jnp.inf)
        sel = spare & (blk_lane == own)
        for _ in range(MOBA_TOPK):
            mx = jnp.max(g, axis=-1, keepdims=True)
            first = jnp.min(jnp.where(g == mx, lane_f, float(LANES)), axis=-1, keepdims=True)
            pick = (lane_f == first) & valid
            sel = sel | pick
            g = jnp.where(pick, -jnp.inf, g)
        bias = jnp.where(sel, 0.0, NEG_BIG)
        q_aug.append(jnp.where(in_head[h], qsc, bias).astype(BF16))
    o_ref[0] = _flash_core(ka_refs, vb_ref, s_ref, q_aug, qi, tq, tk)


def _moba_prompt(qk, vt, tq, tk):
    B, S, W2 = qk.shape
    W = W2 // 2
    n_pairs = W // LANES
    assert S % MOBA_BLOCK == 0 and S // MOBA_BLOCK <= HEAD_DIM
    assert tk % MOBA_BLOCK == 0 and tq % tk == 0
    one = pl.Buffered(1)
    return pl.pallas_call(
        functools.partial(_moba_kernel, tq=tq, tk=tk), grid=(B, n_pairs, S // tq),
        in_specs=[pl.BlockSpec((1, tq, LANES), lambda b, p, i: (b, i, p)),
                  pl.BlockSpec((1, S, LANES), lambda b, p, i: (b, 0, n_pairs + p), pipeline_mode=one),
                  pl.BlockSpec((1, LANES, S), lambda b, p, i: (b, p, 0), pipeline_mode=one)],
        out_specs=pl.BlockSpec((1, tq, LANES), lambda b, p, i: (b, i, p)),
        out_shape=jax.ShapeDtypeStruct((B, S, W), F32),
        scratch_shapes=[pltpu.VMEM((S, LANES), BF16), pltpu.VMEM((S, LANES), BF16),
                        pltpu.VMEM((LANES, S), BF16), pltpu.VMEM((LANES, LANES), F32),
                        pltpu.VMEM((2, HEADS_PER_VREG, tk, tq), F32)],
        compiler_params=_params(3), name="moba_prompt")(qk, qk, vt)


def _class_rows(ref, dil, cls, first, count):
    if dil == 1:
        return ref[0, first:first + count, :]
    return ref[0, pl.ds(cls + dil * first, count, stride=dil), :]


def _dil_kernel(q_ref, kp_ref, kc_ref, vp_ref, vc_ref, o_ref, ob_ref, eb_ref):
    t = pl.program_id(2)
    tile = q_ref.shape[1]
    sub = DIL_KEYS
    _, (is_h0, _) = _lane_masks(sub)
    a = lax.broadcasted_iota(jnp.int32, (sub, 2 * sub), 0)
    c = lax.broadcasted_iota(jnp.int32, (sub, 2 * sub), 1)
    band = (c >= a) & (c <= a + sub)
    band_first = band & (c >= jnp.where(t > 0, 0, sub))
    for bi, (_, dil) in enumerate(DIL_PAIRS):
        n = tile // dil
        for cls in range(dil):
            for u in range(n // sub):
                q = _class_rows(q_ref, dil, cls, u * sub, sub) * ATTN_SCALE
                if u == 0:
                    kk = jnp.concatenate([_class_rows(kp_ref, dil, cls, n - sub, sub),
                                          _class_rows(kc_ref, dil, cls, 0, sub)], axis=0)
                    vv = jnp.concatenate([_class_rows(vp_ref, dil, cls, n - sub, sub),
                                          _class_rows(vc_ref, dil, cls, 0, sub)], axis=0)
                    mask = band_first
                else:
                    kk = _class_rows(kc_ref, dil, cls, (u - 1) * sub, 2 * sub)
                    vv = _class_rows(vc_ref, dil, cls, (u - 1) * sub, 2 * sub)
                    mask = band
                kk = kk.astype(BF16)
                vv = vv.astype(BF16)
                outs, lses = [], []
                for h in range(HEADS_PER_VREG):
                    qh = jnp.where(is_h0, q, 0.0) if h == 0 else jnp.where(is_h0, 0.0, q)
                    s = lax.dot_general(qh.astype(BF16), kk, NT_DIMS, preferred_element_type=F32)
                    s = jnp.where(mask, s, -jnp.inf)
                    m = jnp.max(s, axis=-1, keepdims=True)
                    p = jnp.exp(s - m)
                    l = jnp.sum(p, axis=-1, keepdims=True)
                    outs.append(jnp.dot(p.astype(BF16), vv, preferred_element_type=F32) / l)
                    lses.append(m + jnp.log(l))
                o_blk = jnp.where(is_h0, outs[0], outs[1])
                e_blk = jnp.where(is_h0, lses[0], lses[1])
                if dil == 1:
                    ob_ref[bi, u * sub:(u + 1) * sub, :] = o_blk
                    eb_ref[bi, u * sub:(u + 1) * sub, :] = e_blk
                else:
                    ob_ref[bi, pl.ds(cls + dil * u * sub, sub, stride=dil), :] = o_blk
                    eb_ref[bi, pl.ds(cls + dil * u * sub, sub, stride=dil), :] = e_blk
    n_br = len(DIL_PAIRS)
    es = [eb_ref[bi] for bi in range(n_br)]
    m = functools.reduce(jnp.maximum, es)
    ws = [jnp.exp(e - m) for e in es]
    o_ref[0] = sum(w * ob_ref[bi] for bi, w in enumerate(ws)) / sum(ws)


def _dil_prompt(qk, v, tile):
    B, S, W = v.shape
    n_pairs = W // LANES
    assert all(tile % (dil * DIL_KEYS) == 0 and win <= tile for win, dil in DIL_PAIRS) and S % tile == 0

    def cur(offset):
        return pl.BlockSpec((1, tile, LANES), lambda b, p, t: (b, t, offset + p))

    def prev(offset):
        return pl.BlockSpec((1, tile, LANES), lambda b, p, t: (b, jnp.maximum(t - 1, 0), offset + p))

    scratch = pltpu.VMEM((len(DIL_PAIRS), tile, LANES), F32)
    return pl.pallas_call(
        _dil_kernel, grid=(B, n_pairs, S // tile),
        in_specs=[cur(0), prev(n_pairs), cur(n_pairs), prev(0), cur(0)],
        out_specs=cur(0), out_shape=jax.ShapeDtypeStruct((B, S, W), F32), scratch_shapes=[scratch, scratch],
        compiler_params=_params(3), name="dil_prompt")(qk, qk, qk, v, v)


def _segment_matrix(width, seg, dtype):
    r = jnp.arange(width) // seg
    return (r[:, None] == r[None, :]).astype(dtype)


def _mem_sample_kernel(q_ref, seg_ref, k_ref, v_ref, o_ref):
    qs = q_ref[0] * MEM_SCALE
    s = jnp.dot((k_ref[0] * qs).astype(BF16), seg_ref[...], preferred_element_type=F32)
    p = jnp.exp(s - jnp.max(s, axis=0, keepdims=True))
    o_ref[0] = jnp.sum(p * v_ref[0], axis=0, keepdims=True) / jnp.sum(p, axis=0, keepdims=True)


def _mem_sample(q, mk, mv):
    NS, D = q.shape
    n_mem = mk.shape[1]
    seg = _segment_matrix(D, MEM_HEAD_DIM, BF16)
    vec = pl.BlockSpec((1, 1, D), lambda n: (n, 0, 0))
    blk = pl.BlockSpec((1, n_mem, D), lambda n: (n, 0, 0))
    out = pl.pallas_call(
        _mem_sample_kernel, grid=(NS,), in_specs=[vec, _const_spec((D, D)), blk, blk], out_specs=vec,
        out_shape=jax.ShapeDtypeStruct((NS, 1, D), F32),
        compiler_params=_params(1), name="mem_sample")(q.reshape(NS, 1, D), seg, mk, mv)
    return out.reshape(NS, D)


def _head_columns(x, n_heads):
    return x.reshape(x.shape[0], n_heads, HEAD_DIM, 1)


def _as_page(cols, page):
    pad = [(0, 0)] * (cols.ndim - 1) + [(0, page - 1)]
    return jnp.pad(cols, pad)


def _page_scores(kt, q_cols):
    return jnp.concatenate([jnp.sum(kt[h] * q_cols[h], axis=0, keepdims=True) for h in range(len(q_cols))], axis=0)


def _fox_sample_kernel(pt_ref, q_ref, *refs, pp):
    del pt_ref
    k_refs, v_refs, lf_refs = refs[:pp + 1], refs[pp + 1:2 * pp + 2], refs[2 * pp + 2:3 * pp + 3]
    o_ref, m_ref, l_ref, acc_ref, c_ref = refs[3 * pp + 3:]
    g = pl.program_id(1)
    H = q_ref.shape[1]
    R = k_refs[0].shape[-1]

    @pl.when(g == 0)
    def _():
        m_ref[...] = jnp.full_like(m_ref, -jnp.inf)
        l_ref[...] = jnp.zeros_like(l_ref)
        acc_ref[...] = jnp.zeros_like(acc_ref)
        c_ref[...] = jnp.zeros_like(c_ref)

    q_cols = [q_ref[0, h] * ATTN_SCALE for h in range(H)]
    row = lax.broadcasted_iota(jnp.int32, (R, R), 0)
    col = lax.broadcasted_iota(jnp.int32, (R, R), 1)
    upper = jnp.where(row <= col, 1.0, 0.0).astype(BF16)

    def update(ks, vs, lfs, valid):
        lf_all = jnp.concatenate([lf_ref[...] for lf_ref in lfs], axis=0)
        pieces = jnp.concatenate(_split3(lf_all), axis=0).astype(BF16)
        sums = jnp.dot(pieces, upper, preferred_element_type=F32)
        n_rows = lf_all.shape[0]
        within = sums[0:n_rows] + sums[n_rows:2 * n_rows] + sums[2 * n_rows:3 * n_rows]
        c_run = c_ref[:, 0:1]
        ss = []
        for i, k_ref in enumerate(ks):
            c = within[i * H:(i + 1) * H] + c_run
            c_run = c[:, R - 1:R]
            s = _page_scores(k_ref[...], q_cols) - c
            ss.append(s if valid is None else jnp.where(valid, s, -jnp.inf))
        m_old = m_ref[:, 0:1]
        m_new = functools.reduce(jnp.maximum, [jnp.max(s, axis=1, keepdims=True) for s in ss], m_old)
        alpha = jnp.exp(m_old - m_new)
        ps = [jnp.exp(s - m_new) for s in ss]
        m_ref[...] = jnp.broadcast_to(m_new, m_ref.shape)
        c_ref[...] = jnp.broadcast_to(c_run, c_ref.shape)
        for h in range(H):
            l_ref[h:h + 1, :] = alpha[h:h + 1, :] * l_ref[h:h + 1, :] + sum(p[h:h + 1, :] for p in ps)
            acc_ref[h] = alpha[h:h + 1, :] * acc_ref[h] + sum(v_ref[h] * p[h:h + 1, :] for v_ref, p in zip(vs, ps))

    update(k_refs[:pp], v_refs[:pp], lf_refs[:pp], None)

    @pl.when(g == pl.num_programs(1) - 1)
    def _():
        update(k_refs[pp:], v_refs[pp:], lf_refs[pp:], lax.broadcasted_iota(jnp.int32, (H, R), 1) < 1)
        for h in range(H):
            o_ref[0, h] = (jnp.sum(acc_ref[h], axis=1, keepdims=True)
                           / jnp.sum(l_ref[h:h + 1, :], axis=1, keepdims=True))


def _fox_sample(q, k_new, v_new, lf_new, page_table, layer, pool_kt, pool_vt, pool_lft, pp):
    NS, W = q.shape
    n_pages = page_table.shape[1]
    _, _, H, _, page = pool_kt.shape

    def paged(tail, i):
        zeros = (0,) * len(tail)
        return pl.BlockSpec((None, None) + tail, lambda n, g, pt, i=i: (layer, pt[n, g * pp + i]) + zeros)

    kv_new = pl.BlockSpec((None, H, HEAD_DIM, page), lambda n, g, pt: (n, 0, 0, 0))
    lf_new_spec = pl.BlockSpec((None, H, page), lambda n, g, pt: (n, 0, 0))
    cols = pl.BlockSpec((1, H, HEAD_DIM, 1), lambda n, g, pt: (n, 0, 0, 0))
    in_specs = [cols]
    in_specs += [paged((H, HEAD_DIM, page), i) for i in range(pp)] + [kv_new]
    in_specs += [paged((H, HEAD_DIM, page), i) for i in range(pp)] + [kv_new]
    in_specs += [paged((H, page), i) for i in range(pp)] + [lf_new_spec]
    stat = pltpu.VMEM((H, page), F32)
    out = pl.pallas_call(
        functools.partial(_fox_sample_kernel, pp=pp),
        grid_spec=pltpu.PrefetchScalarGridSpec(
            num_scalar_prefetch=1, grid=(NS, n_pages // pp), in_specs=in_specs, out_specs=cols,
            scratch_shapes=[stat, stat, pltpu.VMEM((H, HEAD_DIM, page), F32), stat]),
        out_shape=jax.ShapeDtypeStruct((NS, H, HEAD_DIM, 1), F32),
        compiler_params=_params(2), name="fox_sample")(
            page_table, _head_columns(q, H),
            *([pool_kt] * pp), _as_page(_head_columns(k_new, H), page),
            *([pool_vt] * pp), _as_page(_head_columns(v_new, H), page),
            *([pool_lft] * pp), _as_page(lf_new.reshape(NS, H, 1), page))
    return out.reshape(NS, W)


def _moba_sample_kernel(pt_ref, q_ref, kn_ref, vn_ref, *refs, bps, n_blocks):
    del pt_ref
    k_refs, v_refs = refs[:2 * bps], refs[2 * bps:4 * bps]
    o_ref, m_ref, l_ref, gate_ref, acc_ref = refs[4 * bps:]
    g = pl.program_id(1)
    H = q_ref.shape[1]
    R = k_refs[0].shape[-1]
    lane = lax.broadcasted_iota(jnp.int32, (H, R), 1)
    lane_d = lax.broadcasted_iota(jnp.int32, (HEAD_DIM, R), 1)

    @pl.when(g == 0)
    def _():
        m_ref[...] = jnp.zeros_like(m_ref)
        l_ref[...] = jnp.zeros_like(l_ref)
        gate_ref[...] = jnp.zeros_like(gate_ref)
        acc_ref[...] = jnp.zeros_like(acc_ref)

    q_cols = [q_ref[0, h] * ATTN_SCALE for h in range(H)]
    ms, ls, gates, probs = m_ref[...], l_ref[...], gate_ref[...], []
    for i in range(bps):
        sa = _page_scores(k_refs[2 * i][...], q_cols)
        sb = _page_scores(k_refs[2 * i + 1][...], q_cols)
        m = jnp.maximum(jnp.max(sa, axis=1, keepdims=True), jnp.max(sb, axis=1, keepdims=True))
        pa, pb = jnp.exp(sa - m), jnp.exp(sb - m)
        l = jnp.sum(pa, axis=1, keepdims=True) + jnp.sum(pb, axis=1, keepdims=True)
        gate = (jnp.sum(sa, axis=1, keepdims=True) + jnp.sum(sb, axis=1, keepdims=True)) * (
            1.0 / (ATTN_SCALE * MOBA_BLOCK))
        hit = lane == g * bps + i
        ms, ls, gates = jnp.where(hit, m, ms), jnp.where(hit, l, ls), jnp.where(hit, gate, gates)
        probs.append((pa, pb))
    m_ref[...], l_ref[...], gate_ref[...] = ms, ls, gates
    for h in range(H):
        acc = acc_ref[h]
        for i, (pa, pb) in enumerate(probs):
            pv = v_refs[2 * i][h] * pa[h:h + 1, :] + v_refs[2 * i + 1][h] * pb[h:h + 1, :]
            acc = jnp.where(lane_d == g * bps + i, jnp.sum(pv, axis=1, keepdims=True), acc)
        acc_ref[h] = acc

    @pl.when(g == pl.num_programs(1) - 1)
    def _():
        lane_f = lane.astype(F32)
        valid = lane < n_blocks
        gates = jnp.where(valid, gate_ref[...], -jnp.inf)
        sel = jnp.zeros((H, R), jnp.bool_)
        for _ in range(MOBA_TOPK):
            mx = jnp.max(gates, axis=1, keepdims=True)
            first = jnp.min(jnp.where(gates == mx, lane_f, float(R)), axis=1, keepdims=True)
            pick = (lane_f == first) & valid
            sel = sel | pick
            gates = jnp.where(pick, -jnp.inf, gates)
        s_new = _page_scores(kn_ref[...], q_cols)[:, 0:1]
        ms = m_ref[...]
        m_fin = jnp.maximum(jnp.max(jnp.where(sel, ms, -jnp.inf), axis=1, keepdims=True), s_new)
        w = jnp.where(sel, jnp.exp(ms - m_fin), 0.0)
        p_new = jnp.exp(s_new - m_fin)
        l = jnp.sum(w * l_ref[...], axis=1, keepdims=True) + p_new
        for h in range(H):
            num = jnp.sum(acc_ref[h] * w[h:h + 1, :], axis=1, keepdims=True) + p_new[h:h + 1, :] * vn_ref[h][:, 0:1]
            o_ref[0, h] = num / l[h:h + 1, :]


def _moba_sample(q, k_new, v_new, page_table, layer, pool_kt, pool_vt, bps):
    NS, W = q.shape
    n_pages = page_table.shape[1]
    _, _, H, _, page = pool_kt.shape
    assert MOBA_BLOCK == 2 * page and n_pages % (2 * bps) == 0
    n_blocks = n_pages // 2
    assert MOBA_TOPK <= n_blocks <= page
    pages = [pl.BlockSpec((None, None, H, HEAD_DIM, page),
                          lambda n, g, pt, i=i: (layer, pt[n, g * 2 * bps + i], 0, 0, 0)) for i in range(2 * bps)]
    kv_new = pl.BlockSpec((None, H, HEAD_DIM, page), lambda n, g, pt: (n, 0, 0, 0))
    cols = pl.BlockSpec((1, H, HEAD_DIM, 1), lambda n, g, pt: (n, 0, 0, 0))
    stat = pltpu.VMEM((H, page), F32)
    out = pl.pallas_call(
        functools.partial(_moba_sample_kernel, bps=bps, n_blocks=n_blocks),
        grid_spec=pltpu.PrefetchScalarGridSpec(
            num_scalar_prefetch=1, grid=(NS, n_blocks // bps),
            in_specs=[cols, kv_new, kv_new] + pages * 2, out_specs=cols,
            scratch_shapes=[stat, stat, stat, pltpu.VMEM((H, HEAD_DIM, page), F32)]),
        out_shape=jax.ShapeDtypeStruct((NS, H, HEAD_DIM, 1), F32),
        compiler_params=_params(2), name="moba_sample")(
            page_table, _head_columns(q, H), _as_page(_head_columns(k_new, H), page),
            _as_page(_head_columns(v_new, H), page), *([pool_kt] * (2 * bps)), *([pool_vt] * (2 * bps)))
    return out.reshape(NS, W)


def _dil_sample_kernel(q_ref, kn_ref, vn_ref, k_ref, v_ref, o_ref):
    hb, _, L = k_ref.shape
    dist = L - lax.broadcasted_iota(jnp.int32, (1, L), 1)
    mult = jnp.zeros((1, L), F32)
    for win, dil in DIL_PAIRS:
        mult = mult + jnp.where(((dist & (dil - 1)) == 0) & (dist <= win), 1.0, 0.0)
    valid = mult > 0.0
    n_br = float(len(DIL_PAIRS))
    for h in range(hb):
        qc = q_ref[0, h] * ATTN_SCALE
        s = jnp.sum(k_ref[h] * qc, axis=0, keepdims=True)
        s_new = jnp.sum(kn_ref[0, h] * qc, axis=0, keepdims=True)
        m = jnp.maximum(jnp.max(jnp.where(valid, s, -jnp.inf), axis=1, keepdims=True), s_new)
        p = jnp.where(valid, jnp.exp(s - m), 0.0) * mult
        p_new = n_br * jnp.exp(s_new - m)
        l = jnp.sum(p, axis=1, keepdims=True) + p_new
        o_ref[0, h] = (jnp.sum(v_ref[h] * p, axis=1, keepdims=True) + p_new * vn_ref[0, h]) / l


def _dil_sample(q, k_new, v_new, layer, buf_kt, buf_vt, hb):
    NS, W = q.shape
    _, _, H, _, L = buf_kt.shape
    assert all(dil & (dil - 1) == 0 and win == DIL_KEYS * dil and win <= L for win, dil in DIL_PAIRS)
    cols = pl.BlockSpec((1, hb, HEAD_DIM, 1), lambda n, j: (n, j, 0, 0))
    buf = pl.BlockSpec((None, None, hb, HEAD_DIM, L), lambda n, j: (layer, n, j, 0, 0))
    out = pl.pallas_call(
        _dil_sample_kernel, grid=(NS, H // hb), in_specs=[cols, cols, cols, buf, buf], out_specs=cols,
        out_shape=jax.ShapeDtypeStruct((NS, H, HEAD_DIM, 1), F32),
        compiler_params=_params(2), name="dil_sample")(
            _head_columns(q, H), _head_columns(k_new, H), _head_columns(v_new, H), buf_kt, buf_vt)
    return out.reshape(NS, W)


TM_PROJ = 512
TM_ROW = 256
TM_FFN = 512
TQ_FLASH = 1024
TK_FLASH = 512
DIL_TILE = 2048
FOX_PAGES_PER_STEP = 8
MOBA_BLOCKS_PER_STEP = 4
DIL_HEADS_PER_STEP = 4


def _positions_last(x):
    n = x.ndim
    return jnp.transpose(x, tuple(range(n - 3)) + (n - 2, n - 1, n - 3))


def _heads_from_t(xt, n_heads):
    B, _, T = xt.shape
    return jnp.transpose(xt.reshape(B, n_heads, HEAD_DIM, T), (0, 3, 1, 2))


def kernel(x_prompt, x_sample, mem_prompt, page_table, cache_fox_k, cache_fox_v, cache_fox_logf, cache_moba_k, cache_moba_v, cache_dwin_k, cache_dwin_v, cache_mem_k, cache_mem_v, state_conv, w_in_even, b_forget, w_out_even, w_in_odd, w_out_odd, w_mem_q, w_mem_k, w_mem_v, w_mem_o, w_ffn_in, ffn_conv_w, ffn_conv_b, w_ffn_out, ln_g, ln_b):
    B, S, D = x_prompt.shape
    NS, T, _ = x_sample.shape
    assert T == 1 and D == D_MODEL
    page = cache_fox_k.shape[2]
    P = page_table.shape[1] * page
    H8 = b_forget.shape[1]
    W8 = H8 * HEAD_DIM
    H16 = D // HEAD_DIM
    n_mem = mem_prompt.shape[1]
    L_C = cache_dwin_k.shape[2]
    assert P >= L_C
    keep = min(DIL_PAIRS[-1][0], S)

    pos_p = jnp.arange(S, dtype=jnp.int32)
    rope_p = _rope_tables(pos_p)
    rope_pt = _rope_tables_t(pos_p)
    rope_s = tuple(jnp.broadcast_to(t, (NS, LANES)) for t in _rope_tables(jnp.full((1,), P, jnp.int32)))

    fox_kt, fox_vt, moba_kt, moba_vt, dwin_kt, dwin_vt = (
        _positions_last(t) for t in (cache_fox_k, cache_fox_v, cache_moba_k, cache_moba_v, cache_dwin_k, cache_dwin_v))
    fox_lft = jnp.swapaxes(cache_fox_logf, -1, -2)

    hp = x_prompt.reshape(B * S, D)
    hs = x_sample.reshape(NS, D)
    mem2 = mem_prompt.reshape(B * n_mem, D)
    out = {name: [] for name in (
        "fox_k_p", "fox_v_p", "fox_l_p", "moba_k_p", "moba_v_p", "dwin_k_p", "dwin_v_p", "mem_k_p", "mem_v_p",
        "conv_p", "fox_k_s", "fox_v_s", "fox_l_s", "moba_k_s", "moba_v_s", "dwin_k_s", "dwin_v_s", "conv_s")}

    for l in range(DEPTH):
        g, bta = ln_g[l], ln_b[l]
        if l % 2 == 0:
            e = l // 2
            w = w_in_even[e]
            cuts = [0, W8, 2 * W8, 3 * W8, 3 * W8 + H8, 4 * W8 + H8, 5 * W8 + H8, 6 * W8 + H8]
            w_qa, w_ka, w_va, w_f, w_qb, w_kb, w_vb = (w[:, cuts[i]:cuts[i + 1]] for i in range(7))
            w_fox = jnp.concatenate([w_qa, w_ka], axis=1).astype(BF16)
            w_moba = jnp.concatenate([w_qb, w_kb], axis=1).astype(BF16)
            wo = w_out_even[e].astype(BF16)
            qk_a = _proj(hp, w_fox, TM_PROJ)
            lf, c = _forget(hp, w_f, b_forget[e], TM_ROW, S)
            qk_b = _proj(hp, w_moba, TM_PROJ, rope=rope_p)
            kt_a, vt_a, kt_b, vt_b = _proj_t(hp, [t.T.astype(BF16) for t in (w_ka, w_va, w_kb, w_vb)],
                                             (False, False, True, False), rope_pt, S, S, TM_PROJ)
            oa = _fox_prompt(qk_a.reshape(B, S, 2 * W8), vt_a, c[:, :H8].reshape(B, S, H8), TQ_FLASH, TK_FLASH)
            ob = _moba_prompt(qk_b.reshape(B, S, 2 * W8), vt_b, TQ_FLASH, TK_FLASH)
            hp = _out_ln([oa.reshape(B * S, W8), ob.reshape(B * S, W8)], [wo[:W8], wo[W8:]], hp,
                         g[0], bta[0], TM_ROW)
            out["fox_k_p"].append(_heads_from_t(kt_a, H8))
            out["fox_v_p"].append(_heads_from_t(vt_a, H8))
            out["fox_l_p"].append(lf[:, :H8].reshape(B, S, H8))
            out["moba_k_p"].append(_heads_from_t(kt_b, H8))
            out["moba_v_p"].append(_heads_from_t(vt_b, H8))
            fox = _proj(hs, jnp.concatenate([w_fox, w_va.astype(BF16)], axis=1), NS)
            qa, ka, va = (fox[:, i * W8:(i + 1) * W8] for i in range(3))
            lf, _ = _forget(hs, w_f, b_forget[e], NS, NS)
            lf = lf[:, :H8]
            mqk = _proj(hs, w_moba, NS, rope=rope_s)
            qb, kb = (mqk[:, i * W8:(i + 1) * W8] for i in range(2))
            vb = _proj(hs, w_vb.astype(BF16), NS)
            oa = _fox_sample(qa, ka, va, lf, page_table, e, fox_kt, fox_vt, fox_lft, FOX_PAGES_PER_STEP)
            ob = _moba_sample(qb, kb, vb, page_table, e, moba_kt, moba_vt, MOBA_BLOCKS_PER_STEP)
            hs = _out_ln([oa, ob], [wo[:W8], wo[W8:]], hs, g[0], bta[0], NS)
            out["fox_k_s"].append(ka.reshape(NS, 1, H8, HEAD_DIM))
            out["fox_v_s"].append(va.reshape(NS, 1, H8, HEAD_DIM))
            out["fox_l_s"].append(lf.reshape(NS, 1, H8))
            out["moba_k_s"].append(kb.reshape(NS, 1, H8, HEAD_DIM))
            out["moba_v_s"].append(vb.reshape(NS, 1, H8, HEAD_DIM))
        else:
            od = l // 2
            w = w_in_odd[od]
            w_qk = w[:, :2 * D].astype(BF16)
            w_k, w_v = w[:, D:2 * D], w[:, 2 * D:]
            wo = w_out_odd[od].astype(BF16)
            qk = _proj(hp, w_qk, TM_PROJ, rope=rope_p).reshape(B, S, 2 * D)
            v = _proj(hp, w_v.astype(BF16), TM_PROJ).reshape(B, S, D)
            kt, vt = _proj_t(hp, [w_k.T.astype(BF16), w_v.T.astype(BF16)], (True, False), rope_pt, S, keep, TM_PROJ)
            o = _dil_prompt(qk, v, DIL_TILE)
            hp = _out_ln([o.reshape(B * S, D)], [wo], hp, g[0], bta[0], TM_ROW)
            out["dwin_k_p"].append(_heads_from_t(kt, H16))
            out["dwin_v_p"].append(_heads_from_t(vt, H16))
            qk = _proj(hs, w_qk, NS, rope=rope_s)
            q, k = qk[:, :D], qk[:, D:]
            v = _proj(hs, w_v.astype(BF16), NS)
            o = _dil_sample(q, k, v, od, dwin_kt, dwin_vt, DIL_HEADS_PER_STEP)
            hs = _out_ln([o], [wo], hs, g[0], bta[0], NS)
            out["dwin_k_s"].append(k.reshape(NS, 1, H16, HEAD_DIM))
            out["dwin_v_s"].append(v.reshape(NS, 1, H16, HEAD_DIM))
        wq, wk, wv, wo = (t[l].astype(BF16) for t in (w_mem_q, w_mem_k, w_mem_v, w_mem_o))
        mk = _proj(mem2, wk, TM_ROW).reshape(B, n_mem, D)
        mv = _proj(mem2, wv, TM_ROW).reshape(B, n_mem, D)
        out["mem_k_p"].append(mk.reshape(B, n_mem, N_MEM_HEADS, MEM_HEAD_DIM))
        out["mem_v_p"].append(mv.reshape(B, n_mem, N_MEM_HEADS, MEM_HEAD_DIM))
        hp = _mem_block(hp, wq, mk, mv, wo, g[1], bta[1], TM_ROW, S)
        qs_ = _proj(hs, wq, NS)
        os_ = _mem_sample(qs_, cache_mem_k[l].reshape(NS, n_mem, D), cache_mem_v[l].reshape(NS, n_mem, D))
        hs = _out_ln([os_], [wo], hs, g[1], bta[1], NS)
        win, wout = w_ffn_in[l].astype(BF16), w_ffn_out[l].astype(BF16)
        hp, cp = _ffn_prompt(hp, win, ffn_conv_w[l], ffn_conv_b[l], wout, g[2], bta[2], TM_FFN, S)
        st = state_conv[l]
        hs, u = _ffn_sample(hs, st[:, 0], st[:, 1], win, ffn_conv_w[l], ffn_conv_b[l], wout, g[2], bta[2])
        out["conv_p"].append(cp)
        out["conv_s"].append(jnp.stack([st[:, 1], u], axis=1))

    st = jnp.stack
    return (hp.reshape(B, S, D), hs.reshape(NS, 1, D),
            st(out["fox_k_p"]), st(out["fox_v_p"]), st(out["fox_l_p"]), st(out["moba_k_p"]), st(out["moba_v_p"]),
            st(out["dwin_k_p"]), st(out["dwin_v_p"]), st(out["mem_k_p"]), st(out["mem_v_p"]), st(out["conv_p"]),
            st(out["fox_k_s"]), st(out["fox_v_s"]), st(out["fox_l_s"]), st(out["moba_k_s"]), st(out["moba_v_s"]),
            st(out["dwin_k_s"]), st(out["dwin_v_s"]), st(out["conv_s"]))
```

```python
import functools
import math

import jax
import jax.numpy as jnp
from jax import lax
from jax.experimental import pallas as pl
from jax.experimental.pallas import tpu as pltpu

F32 = jnp.float32
BF16 = jnp.bfloat16

LANES = 128

D_MODEL = 1024
HEAD_DIM = 64
HEADS_PER_VREG = LANES // HEAD_DIM
N_MEM_HEADS = 4
MEM_HEAD_DIM = D_MODEL // N_MEM_HEADS
D_FF = 2816
FFN_CHUNK = 256
MOBA_BLOCK = 256
MOBA_BLOCK_SHIFT = 8
MOBA_TOPK = 3
DIL_PAIRS = ((128, 1), (512, 4), (2048, 16))
DIL_KEYS = 128
ROPE_THETA = 10000.0
LN_EPS = 1e-5
DEPTH = 4
DN_ALPHA = (2 * DEPTH) ** 0.25
ATTN_SCALE = HEAD_DIM ** -0.5
MEM_SCALE = MEM_HEAD_DIM ** -0.5
LOG2E = math.log2(math.e)
NEG_BIG = -1e30
VMEM_LIMIT = 56 * 2 ** 20

NT_DIMS = (((1,), (1,)), ((), ()))
HIGHEST = lax.Precision.HIGHEST


def _params(n_axes):
    return pltpu.CompilerParams(dimension_semantics=("arbitrary",) * n_axes,
                                vmem_limit_bytes=VMEM_LIMIT)


def _const_spec(shape):
    zeros = (0,) * len(shape)
    return pl.BlockSpec(shape, lambda *_: zeros, pipeline_mode=pl.Buffered(1))


def _layer_norm(y, g, b):
    mu = jnp.mean(y, axis=-1, keepdims=True)
    yc = y - mu
    var = jnp.mean(yc * yc, axis=-1, keepdims=True)
    return yc * lax.rsqrt(var + LN_EPS) * g + b


def _rope_angles(pos):
    half = HEAD_DIM // 2
    inv = ROPE_THETA ** (-jnp.arange(half, dtype=F32) / half)
    ang = pos.astype(F32)[:, None] * inv[None, :]
    return jnp.cos(ang), jnp.sin(ang)


def _rope_tables(pos):
    cos, sin = _rope_angles(pos)
    cos2 = jnp.tile(jnp.concatenate([cos, cos], axis=-1), (1, HEADS_PER_VREG))
    sin2 = jnp.tile(jnp.concatenate([-sin, sin], axis=-1), (1, HEADS_PER_VREG))
    return cos2, sin2


def _rope_tables_t(pos):
    cos, sin = _rope_angles(pos)
    return jnp.concatenate([cos, cos], axis=-1).T, jnp.concatenate([-sin, sin], axis=-1).T


def _proj_kernel(x_ref, w_ref, o_ref):
    o_ref[...] = jnp.dot(x_ref[...].astype(BF16), w_ref[...], preferred_element_type=F32)


def _proj_rope_kernel(x_ref, w_ref, cos_ref, sin_ref, o_ref):
    z = jnp.dot(x_ref[...].astype(BF16), w_ref[...], preferred_element_type=F32)
    tm, tn = z.shape
    cos, sin = cos_ref[...], sin_ref[...]
    lane = lax.broadcasted_iota(jnp.int32, (tm, LANES), 1)
    first_half = (lane & (HEAD_DIM - 1)) < HEAD_DIM // 2
    for c in range(tn // LANES):
        zc = z[:, c * LANES:(c + 1) * LANES]
        partner = jnp.where(first_half, pltpu.roll(zc, LANES - HEAD_DIM // 2, 1),
                            pltpu.roll(zc, HEAD_DIM // 2, 1))
        o_ref[:, c * LANES:(c + 1) * LANES] = zc * cos + partner * sin


def _proj(x, w, tm, rope=None):
    M, K = x.shape
    N = w.shape[1]
    x_spec = pl.BlockSpec((tm, K), lambda i: (i, 0))
    o_spec = pl.BlockSpec((tm, N), lambda i: (i, 0))
    out_shape = jax.ShapeDtypeStruct((M, N), F32)
    if rope is None:
        return pl.pallas_call(_proj_kernel, grid=(M // tm,), in_specs=[x_spec, _const_spec((K, N))],
                              out_specs=o_spec, out_shape=out_shape, compiler_params=_params(1), name="proj")(x, w)
    cos2, sin2 = rope
    pos_blocks = cos2.shape[0] // tm
    t_spec = pl.BlockSpec((tm, LANES), lambda i: (i % pos_blocks, 0))
    return pl.pallas_call(_proj_rope_kernel, grid=(M // tm,), in_specs=[x_spec, _const_spec((K, N)), t_spec, t_spec],
                          out_specs=o_spec, out_shape=out_shape, compiler_params=_params(1),
                          name="proj_rope")(x, w, cos2, sin2)


def _proj_t_kernel(*refs, ropes):
    n = len(ropes)
    x_ref, wt_refs, cos_ref, sin_ref, o_refs = refs[0], refs[1:1 + n], refs[1 + n], refs[2 + n], refs[3 + n:]
    xb = x_ref[...].astype(BF16)
    half = HEAD_DIM // 2
    for wt_ref, o_ref, rope in zip(wt_refs, o_refs, ropes):
        z = lax.dot_general(wt_ref[...], xb, NT_DIMS, preferred_element_type=F32)
        if not rope:
            o_ref[0] = z
            continue
        cos, sin = cos_ref[...], sin_ref[...]
        for g in range(z.shape[0] // HEAD_DIM):
            zg = z[g * HEAD_DIM:(g + 1) * HEAD_DIM]
            partner = jnp.concatenate([zg[half:], zg[:half]], axis=0)
            o_ref[0, g * HEAD_DIM:(g + 1) * HEAD_DIM, :] = zg * cos + partner * sin


def _proj_t(x, wts, ropes, rope_t, seq_len, keep, tm):
    M, K = x.shape
    B = M // seq_len
    tiles_in, tiles_out = seq_len // tm, keep // tm
    first = tiles_in - tiles_out
    x_spec = pl.BlockSpec((tm, K), lambda i: ((i // tiles_out) * tiles_in + first + i % tiles_out, 0))
    t_spec = pl.BlockSpec((HEAD_DIM, tm), lambda i: (0, first + i % tiles_out))
    o_specs = [pl.BlockSpec((1, wt.shape[0], tm), lambda i: (i // tiles_out, 0, i % tiles_out)) for wt in wts]
    return pl.pallas_call(
        functools.partial(_proj_t_kernel, ropes=tuple(ropes)), grid=(B * tiles_out,),
        in_specs=[x_spec] + [_const_spec(wt.shape) for wt in wts] + [t_spec, t_spec], out_specs=o_specs,
        out_shape=[jax.ShapeDtypeStruct((B, wt.shape[0], keep), F32) for wt in wts],
        compiler_params=_params(1), name="proj_t")(x, *wts, *rope_t)


def _forget_kernel(x_ref, w_ref, b_ref, lf_ref, c_ref, carry_ref, *, tiles_per_seq):
    i = pl.program_id(0)
    z = jnp.dot(x_ref[...].astype(BF16), w_ref[...], preferred_element_type=F32) + b_ref[...]
    lf = -(jnp.maximum(-z, 0.0) + jnp.log1p(jnp.exp(-jnp.abs(z))))
    lf_ref[...] = lf

    @pl.when(i % tiles_per_seq == 0)
    def _():
        carry_ref[...] = jnp.zeros_like(carry_ref)

    tm = lf.shape[0]
    row = lax.broadcasted_iota(jnp.int32, (tm, tm), 0)
    col = lax.broadcasted_iota(jnp.int32, (tm, tm), 1)
    tri = jnp.where(col <= row, 1.0, 0.0).astype(F32)
    c = jnp.dot(tri, lf, precision=HIGHEST, preferred_element_type=F32) + carry_ref[0:1, :]
    c_ref[...] = c
    carry_ref[0:1, :] = c[tm - 1:tm, :]


def _forget(x, w_f, b_f, tm, seq_len):
    M, K = x.shape
    n_heads = w_f.shape[1]
    w = jnp.zeros((K, LANES), BF16).at[:, :n_heads].set(w_f.astype(BF16))
    b = jnp.zeros((1, LANES), F32).at[0, :n_heads].set(b_f)
    kern = functools.partial(_forget_kernel, tiles_per_seq=seq_len // tm)
    out = jax.ShapeDtypeStruct((M, LANES), F32)
    return pl.pallas_call(
        kern, grid=(M // tm,),
        in_specs=[pl.BlockSpec((tm, K), lambda i: (i, 0)), _const_spec((K, LANES)), _const_spec((1, LANES))],
        out_specs=[pl.BlockSpec((tm, LANES), lambda i: (i, 0))] * 2,
        out_shape=[out, out], scratch_shapes=[pltpu.VMEM((8, LANES), F32)],
        compiler_params=_params(1), name="forget")(x, w, b)


def _out_ln_kernel(*refs, n_in):
    a_refs, w_refs = refs[:n_in], refs[n_in:2 * n_in]
    x_ref, g_ref, b_ref, o_ref = refs[2 * n_in:]
    acc = None
    for a_ref, w_ref in zip(a_refs, w_refs):
        d = jnp.dot(a_ref[...].astype(BF16), w_ref[...], preferred_element_type=F32)
        acc = d if acc is None else acc + d
    o_ref[...] = _layer_norm(DN_ALPHA * x_ref[...] + acc, g_ref[...], b_ref[...])


def _out_ln(a_list, w_list, x, g, b, tm):
    M, D = x.shape
    n_in = len(a_list)
    in_specs = [pl.BlockSpec((tm, a.shape[1]), lambda i: (i, 0)) for a in a_list]
    in_specs += [_const_spec(w.shape) for w in w_list]
    in_specs += [pl.BlockSpec((tm, D), lambda i: (i, 0)), _const_spec((1, D)), _const_spec((1, D))]
    return pl.pallas_call(
        functools.partial(_out_ln_kernel, n_in=n_in), grid=(M // tm,), in_specs=in_specs,
        out_specs=pl.BlockSpec((tm, D), lambda i: (i, 0)), out_shape=jax.ShapeDtypeStruct((M, D), F32),
        compiler_params=_params(1), name="out_ln")(*a_list, *w_list, x, g.reshape(1, D), b.reshape(1, D))


def _mem_kernel(x_ref, wq_ref, mk_ref, mv_ref, wo_ref, g_ref, b_ref, o_ref):
    x = x_ref[...]
    q = jnp.dot(x.astype(BF16), wq_ref[...], preferred_element_type=F32)
    mk = mk_ref[0].astype(BF16)
    mv = mv_ref[0].astype(BF16)
    heads = []
    for h in range(N_MEM_HEADS):
        sl = slice(h * MEM_HEAD_DIM, (h + 1) * MEM_HEAD_DIM)
        qh = (q[:, sl] * MEM_SCALE).astype(BF16)
        s = lax.dot_general(qh, mk[:, sl], NT_DIMS, preferred_element_type=F32)
        e = jnp.exp(s - jnp.max(s, axis=-1, keepdims=True))
        l = jnp.sum(e, axis=-1, keepdims=True)
        heads.append(jnp.dot(e.astype(BF16), mv[:, sl], preferred_element_type=F32) / l)
    o = jnp.concatenate(heads, axis=-1)
    mix = jnp.dot(o.astype(BF16), wo_ref[...], preferred_element_type=F32)
    o_ref[...] = _layer_norm(DN_ALPHA * x + mix, g_ref[...], b_ref[...])


def _mem_block(x, wq, mk, mv, wo, g, b, tm, seq_len):
    M, D = x.shape
    n_mem = mk.shape[1]
    tiles_per_seq = seq_len // tm
    row = pl.BlockSpec((tm, D), lambda i: (i, 0))
    mem = pl.BlockSpec((1, n_mem, D), lambda i: (i // tiles_per_seq, 0, 0))
    return pl.pallas_call(
        _mem_kernel, grid=(M // tm,),
        in_specs=[row, _const_spec((D, D)), mem, mem, _const_spec((D, D)),
                  _const_spec((1, D)), _const_spec((1, D))],
        out_specs=row, out_shape=jax.ShapeDtypeStruct((M, D), F32),
        compiler_params=_params(1), name="mem_block")(x, wq, mk, mv, wo, g.reshape(1, D), b.reshape(1, D))


def _gated(ha, hg):
    return ha * (0.5 * hg * (1.0 + lax.erf(hg * (2.0 ** -0.5))))


def _ffn_prompt_kernel(x_ref, win_ref, cw_ref, cb_ref, wout_ref, g_ref, b_ref, o_ref, st_ref, carry_ref, act_ref,
                       *, tiles_per_seq):
    i = pl.program_id(0)

    @pl.when(i % tiles_per_seq == 0)
    def _():
        carry_ref[...] = jnp.zeros_like(carry_ref)

    x = x_ref[...]
    xb = x.astype(BF16)
    tm = x.shape[0]
    row = lax.broadcasted_iota(jnp.int32, (tm, FFN_CHUNK), 0)
    for c in range(D_FF // FFN_CHUNK):
        hs = []
        for off in (c * FFN_CHUNK, D_FF + c * FFN_CHUNK):
            sl = slice(off, off + FFN_CHUNK)
            u = jnp.dot(xb, win_ref[:, sl], preferred_element_type=F32)
            prev2 = carry_ref[0:1, sl]
            prev1 = carry_ref[1:2, sl]
            u1 = jnp.where(row == 0, prev1, pltpu.roll(u, 1, 0))
            u2 = jnp.where(row == 0, prev2, jnp.where(row == 1, prev1, pltpu.roll(u, 2, 0)))
            last = u[tm - 2:tm, :]
            carry_ref[0:2, sl] = last
            st_ref[0, :, sl] = last
            hs.append(cb_ref[:, sl] + cw_ref[0:1, sl] * u2 + cw_ref[1:2, sl] * u1 + cw_ref[2:3, sl] * u)
        act_ref[:, c * FFN_CHUNK:(c + 1) * FFN_CHUNK] = _gated(hs[0], hs[1]).astype(BF16)
    mix = jnp.dot(act_ref[...], wout_ref[...], preferred_element_type=F32)
    o_ref[...] = _layer_norm(DN_ALPHA * x + mix, g_ref[...], b_ref[...])


def _ffn_prompt(x, win, cw, cb, wout, g, b, tm, seq_len):
    M, D = x.shape
    F2 = win.shape[1]
    tiles_per_seq = seq_len // tm
    row = pl.BlockSpec((tm, D), lambda i: (i, 0))
    return pl.pallas_call(
        functools.partial(_ffn_prompt_kernel, tiles_per_seq=tiles_per_seq), grid=(M // tm,),
        in_specs=[row, _const_spec((D, F2)), _const_spec((3, F2)), _const_spec((1, F2)),
                  _const_spec((F2 // 2, D)), _const_spec((1, D)), _const_spec((1, D))],
        out_specs=[row, pl.BlockSpec((1, 2, F2), lambda i: (i // tiles_per_seq, 0, 0))],
        out_shape=[jax.ShapeDtypeStruct((M, D), F32), jax.ShapeDtypeStruct((M // seq_len, 2, F2), F32)],
        scratch_shapes=[pltpu.VMEM((8, F2), F32), pltpu.VMEM((tm, F2 // 2), BF16)],
        compiler_params=_params(1), name="ffn_prompt")(x, win, cw, cb.reshape(1, F2), wout,
                                                       g.reshape(1, D), b.reshape(1, D))


def _ffn_sample_kernel(x_ref, s0_ref, s1_ref, win_ref, cw_ref, cb_ref, wout_ref, g_ref, b_ref, o_ref, u_ref):
    x = x_ref[...]
    xb = x.astype(BF16)
    acc = jnp.zeros(x.shape, F32)
    for c in range(D_FF // FFN_CHUNK):
        hs = []
        for off in (c * FFN_CHUNK, D_FF + c * FFN_CHUNK):
            sl = slice(off, off + FFN_CHUNK)
            u = jnp.dot(xb, win_ref[:, sl], preferred_element_type=F32)
            u_ref[:, sl] = u
            hs.append(cb_ref[:, sl] + cw_ref[0:1, sl] * s0_ref[:, sl] + cw_ref[1:2, sl] * s1_ref[:, sl]
                      + cw_ref[2:3, sl] * u)
        act = _gated(hs[0], hs[1])
        acc = acc + jnp.dot(act.astype(BF16), wout_ref[c * FFN_CHUNK:(c + 1) * FFN_CHUNK, :],
                            preferred_element_type=F32)
    o_ref[...] = _layer_norm(DN_ALPHA * x + acc, g_ref[...], b_ref[...])


def _ffn_sample(x, s0, s1, win, cw, cb, wout, g, b):
    M, D = x.shape
    F2 = win.shape[1]
    return pl.pallas_call(
        _ffn_sample_kernel, grid=(1,),
        in_specs=[_const_spec((M, D)), _const_spec((M, F2)), _const_spec((M, F2)), _const_spec((D, F2)),
                  _const_spec((3, F2)), _const_spec((1, F2)), _const_spec((F2 // 2, D)),
                  _const_spec((1, D)), _const_spec((1, D))],
        out_specs=[pl.BlockSpec((M, D), lambda i: (0, 0)), pl.BlockSpec((M, F2), lambda i: (0, 0))],
        out_shape=[jax.ShapeDtypeStruct((M, D), F32), jax.ShapeDtypeStruct((M, F2), F32)],
        compiler_params=_params(1), name="ffn_sample")(x, s0, s1, win, cw, cb.reshape(1, F2), wout,
                                                       g.reshape(1, D), b.reshape(1, D))


def _lane_masks(rows):
    lane = lax.broadcasted_iota(jnp.int32, (rows, LANES), 1)
    is_h0 = lane < HEAD_DIM
    return lane, (is_h0, jnp.logical_not(is_h0))


def _flash_core(ka_refs, vb_ref, s_ref, q_aug, qi, tq, tk):
    assert tq == 2 * tk
    qs = qi * tq

    def scores(ks, slot):
        for h in range(HEADS_PER_VREG):
            s_ref[slot, h] = lax.dot_general(ka_refs[h][pl.ds(ks, tk), :], q_aug[h], NT_DIMS,
                                             preferred_element_type=F32)

    def consume(ks, slot, states, mask):
        new = []
        for h in range(HEADS_PER_VREG):
            m, l, acc = states[h]
            s = s_ref[slot, h]
            if mask is not None:
                s = jnp.where(mask, s, -jnp.inf)
            m_new = jnp.maximum(m, jnp.max(s, axis=0, keepdims=True))
            alpha = jnp.exp2(m - m_new)
            p = jnp.exp2(s - m_new)
            l = alpha * l + jnp.sum(p, axis=0, keepdims=True)
            vt = vb_ref[h * HEAD_DIM:(h + 1) * HEAD_DIM, pl.ds(ks, tk)]
            acc = alpha * acc + jnp.dot(vt, p.astype(BF16), preferred_element_type=F32)
            new.append((m_new, l, acc))
        return tuple(new)

    def chunk(j):
        return pl.multiple_of(j * tk, tk)

    def pair(jj, states):
        scores(chunk(2 * jj + 1), 1)
        states = consume(chunk(2 * jj), 0, states, None)
        scores(chunk(2 * jj + 2), 0)
        return consume(chunk(2 * jj + 1), 1, states, None)

    init = (jnp.full((1, tq), -jnp.inf, F32), jnp.zeros((1, tq), F32), jnp.zeros((HEAD_DIM, tq), F32))
    scores(0, 0)
    states = lax.fori_loop(0, qi, pair, (init, init))
    kpos = lax.broadcasted_iota(jnp.int32, (tk, tq), 0)
    qpos = lax.broadcasted_iota(jnp.int32, (tk, tq), 1)
    scores(chunk(2 * qi + 1), 1)
    states = consume(chunk(2 * qi), 0, states, kpos <= qpos)
    states = consume(chunk(2 * qi + 1), 1, states, kpos + tk <= qpos)
    (_, l0, a0), (_, l1, a1) = states
    return jnp.concatenate([a0 / l0, a1 / l1], axis=0).T


def _split3(x):
    hi = x.astype(BF16).astype(F32)
    r = x - hi
    mid = r.astype(BF16).astype(F32)
    lo = (r - mid).astype(BF16).astype(F32)
    return hi, mid, lo


def _place(lane, base, cols):
    out = jnp.zeros(lane.shape, F32)
    for i, v in enumerate(cols):
        out = jnp.where(lane == base + i, v, out)
    return out


def _fox_kernel(q_ref, k_ref, vt_ref, cc_ref, o_ref, ka0_ref, ka1_ref, vb_ref, s_ref, *, tq, tk):
    qi = pl.program_id(2)
    S = k_ref.shape[1]
    fill = MOBA_BLOCK
    ka_refs = (ka0_ref, ka1_ref)

    @pl.when(qi == 0)
    def _():
        vb_ref[...] = vt_ref[0].astype(BF16)
        lane, in_head = _lane_masks(fill)

        def body(j, _):
            rs = pl.multiple_of(j * fill, fill)
            kc = k_ref[0, pl.ds(rs, fill), :]
            cc = cc_ref[0, 0, pl.ds(rs, fill), :] * LOG2E
            for h in range(HEADS_PER_VREG):
                hi, mid, lo = _split3(cc[:, h:h + 1])
                one = jnp.ones_like(hi)
                extra = _place(lane, (1 - h) * HEAD_DIM, (-hi, -mid, -lo, one, one, one))
                ka_refs[h][pl.ds(rs, fill), :] = jnp.where(in_head[h], kc, extra).astype(BF16)
            return 0

        lax.fori_loop(0, S // fill, body, 0)

    lane, in_head = _lane_masks(tq)
    q = q_ref[0] * (ATTN_SCALE * LOG2E)
    cq = cc_ref[0, 0, pl.ds(pl.multiple_of(qi * tq, tq), tq), :] * LOG2E
    q_aug = []
    for h in range(HEADS_PER_VREG):
        hi, mid, lo = _split3(cq[:, h:h + 1])
        one = jnp.ones_like(hi)
        extra = _place(lane, (1 - h) * HEAD_DIM, (one, one, one, hi, mid, lo))
        q_aug.append(jnp.where(in_head[h], q, extra).astype(BF16))
    o_ref[0] = _flash_core(ka_refs, vb_ref, s_ref, q_aug, qi, tq, tk)


def _fox_prompt(qk, vt, c, tq, tk):
    B, S, W2 = qk.shape
    W = W2 // 2
    n_pairs = W // LANES
    cc = c.reshape(B, S, n_pairs, HEADS_PER_VREG).transpose(0, 2, 1, 3)
    one = pl.Buffered(1)
    return pl.pallas_call(
        functools.partial(_fox_kernel, tq=tq, tk=tk), grid=(B, n_pairs, S // tq),
        in_specs=[pl.BlockSpec((1, tq, LANES), lambda b, p, i: (b, i, p)),
                  pl.BlockSpec((1, S, LANES), lambda b, p, i: (b, 0, n_pairs + p), pipeline_mode=one),
                  pl.BlockSpec((1, LANES, S), lambda b, p, i: (b, p, 0), pipeline_mode=one),
                  pl.BlockSpec((1, 1, S, HEADS_PER_VREG), lambda b, p, i: (b, p, 0, 0), pipeline_mode=one)],
        out_specs=pl.BlockSpec((1, tq, LANES), lambda b, p, i: (b, i, p)),
        out_shape=jax.ShapeDtypeStruct((B, S, W), F32),
        scratch_shapes=[pltpu.VMEM((S, LANES), BF16), pltpu.VMEM((S, LANES), BF16), pltpu.VMEM((LANES, S), BF16),
                        pltpu.VMEM((2, HEADS_PER_VREG, tk, tq), F32)],
        compiler_params=_params(3), name="fox_prompt")(qk, qk, vt, cc)


def _moba_kernel(q_ref, k_ref, vt_ref, o_ref, ka0_ref, ka1_ref, vb_ref, kmp_ref, s_ref, *, tq, tk):
    qi = pl.program_id(2)
    S = k_ref.shape[1]
    blk = MOBA_BLOCK
    ka_refs = (ka0_ref, ka1_ref)

    @pl.when(qi == 0)
    def _():
        vb_ref[...] = vt_ref[0].astype(BF16)
        kmp_ref[...] = jnp.zeros_like(kmp_ref)
        lane, in_head = _lane_masks(blk)
        blk_lane = lane & (HEAD_DIM - 1)

        def body(j, _):
            rs = pl.multiple_of(j * blk, blk)
            kc = k_ref[0, pl.ds(rs, blk), :]
            onehot = jnp.where(blk_lane == j, 1.0, 0.0)
            for h in range(HEADS_PER_VREG):
                ka_refs[h][pl.ds(rs, blk), :] = jnp.where(in_head[h], kc, onehot).astype(BF16)
            kmean = jnp.sum(kc, axis=0, keepdims=True) * (1.0 / blk)
            h0row = in_head[0][0:1, :]
            kmp_ref[pl.ds(HEAD_DIM + j, 1), :] = jnp.where(h0row, kmean, 0.0)
            kmp_ref[pl.ds(j, 1), :] = jnp.where(h0row, 0.0, kmean)
            return 0

        lax.fori_loop(0, S // blk, body, 0)

    lane, in_head = _lane_masks(tq)
    blk_lane = lane & (HEAD_DIM - 1)
    lane_f = lane.astype(F32)
    q = q_ref[0]
    gate = lax.dot_general(q, kmp_ref[...], NT_DIMS, precision=HIGHEST, preferred_element_type=F32)
    qpos = qi * tq + lax.broadcasted_iota(jnp.int32, (tq, LANES), 0)
    own = lax.shift_right_logical(qpos, MOBA_BLOCK_SHIFT)
    qsc = q * (ATTN_SCALE * LOG2E)
    q_aug = []
    for h in range(HEADS_PER_VREG):
        spare = in_head[1 - h]
        valid = spare & (blk_lane < own)
        g = jnp.where(valid, gate, -jnp.inf)
        sel = spare & (blk_lane == own)
        for _ in range(MOBA_TOPK):
            mx = jnp.max(g, axis=-1, keepdims=True)
            first = jnp.min(jnp.where(g == mx, lane_f, float(LANES)), axis=-1, keepdims=True)
            pick = (lane_f == first) & valid
            sel = sel | pick
            g = jnp.where(pick, -jnp.inf, g)
        bias = jnp.where(sel, 0.0, NEG_BIG)
        q_aug.append(jnp.where(in_head[h], qsc, bias).astype(BF16))
    o_ref[0] = _flash_core(ka_refs, vb_ref, s_ref, q_aug, qi, tq, tk)


def _moba_prompt(qk, vt, tq, tk):
    B, S, W2 = qk.shape
    W = W2 // 2
    n_pairs = W // LANES
    assert S % MOBA_BLOCK == 0 and S // MOBA_BLOCK <= HEAD_DIM
    assert tk % MOBA_BLOCK == 0 and tq % tk == 0
    one = pl.Buffered(1)
    return pl.pallas_call(
        functools.partial(_moba_kernel, tq=tq, tk=tk), grid=(B, n_pairs, S // tq),
        in_specs=[pl.BlockSpec((1, tq, LANES), lambda b, p, i: (b, i, p)),
                  pl.BlockSpec((1, S, LANES), lambda b, p, i: (b, 0, n_pairs + p), pipeline_mode=one),
                  pl.BlockSpec((1, LANES, S), lambda b, p, i: (b, p, 0), pipeline_mode=one)],
        out_specs=pl.BlockSpec((1, tq, LANES), lambda b, p, i: (b, i, p)),
        out_shape=jax.ShapeDtypeStruct((B, S, W), F32),
        scratch_shapes=[pltpu.VMEM((S, LANES), BF16), pltpu.VMEM((S, LANES), BF16),
                        pltpu.VMEM((LANES, S), BF16), pltpu.VMEM((LANES, LANES), F32),
                        pltpu.VMEM((2, HEADS_PER_VREG, tk, tq), F32)],
        compiler_params=_params(3), name="moba_prompt")(qk, qk, vt)


def _class_rows(ref, dil, cls, first, count):
    if dil == 1:
        return ref[0, first:first + count, :]
    return ref[0, pl.ds(cls + dil * first, count, stride=dil), :]


def _dil_kernel(q_ref, kp_ref, kc_ref, vp_ref, vc_ref, o_ref, ob_ref, eb_ref):
    t = pl.program_id(2)
    tile = q_ref.shape[1]
    sub = DIL_KEYS
    _, (is_h0, _) = _lane_masks(sub)
    a = lax.broadcasted_iota(jnp.int32, (sub, 2 * sub), 0)
    c = lax.broadcasted_iota(jnp.int32, (sub, 2 * sub), 1)
    band = (c >= a) & (c <= a + sub)
    band_first = band & (c >= jnp.where(t > 0, 0, sub))
    for bi, (_, dil) in enumerate(DIL_PAIRS):
        n = tile // dil
        for cls in range(dil):
            for u in range(n // sub):
                q = _class_rows(q_ref, dil, cls, u * sub, sub) * ATTN_SCALE
                if u == 0:
                    kk = jnp.concatenate([_class_rows(kp_ref, dil, cls, n - sub, sub),
                                          _class_rows(kc_ref, dil, cls, 0, sub)], axis=0)
                    vv = jnp.concatenate([_class_rows(vp_ref, dil, cls, n - sub, sub),
                                          _class_rows(vc_ref, dil, cls, 0, sub)], axis=0)
                    mask = band_first
                else:
                    kk = _class_rows(kc_ref, dil, cls, (u - 1) * sub, 2 * sub)
                    vv = _class_rows(vc_ref, dil, cls, (u - 1) * sub, 2 * sub)
                    mask = band
                kk = kk.astype(BF16)
                vv = vv.astype(BF16)
                outs, lses = [], []
                for h in range(HEADS_PER_VREG):
                    qh = jnp.where(is_h0, q, 0.0) if h == 0 else jnp.where(is_h0, 0.0, q)
                    s = lax.dot_general(qh.astype(BF16), kk, NT_DIMS, preferred_element_type=F32)
                    s = jnp.where(mask, s, -jnp.inf)
                    m = jnp.max(s, axis=-1, keepdims=True)
                    p = jnp.exp(s - m)
                    l = jnp.sum(p, axis=-1, keepdims=True)
                    outs.append(jnp.dot(p.astype(BF16), vv, preferred_element_type=F32) / l)
                    lses.append(m + jnp.log(l))
                o_blk = jnp.where(is_h0, outs[0], outs[1])
                e_blk = jnp.where(is_h0, lses[0], lses[1])
                if dil == 1:
                    ob_ref[bi, u * sub:(u + 1) * sub, :] = o_blk
                    eb_ref[bi, u * sub:(u + 1) * sub, :] = e_blk
                else:
                    ob_ref[bi, pl.ds(cls + dil * u * sub, sub, stride=dil), :] = o_blk
                    eb_ref[bi, pl.ds(cls + dil * u * sub, sub, stride=dil), :] = e_blk
    n_br = len(DIL_PAIRS)
    es = [eb_ref[bi] for bi in range(n_br)]
    m = functools.reduce(jnp.maximum, es)
    ws = [jnp.exp(e - m) for e in es]
    o_ref[0] = sum(w * ob_ref[bi] for bi, w in enumerate(ws)) / sum(ws)


def _dil_prompt(qk, v, tile):
    B, S, W = v.shape
    n_pairs = W // LANES
    assert all(tile % (dil * DIL_KEYS) == 0 and win <= tile for win, dil in DIL_PAIRS) and S % tile == 0

    def cur(offset):
        return pl.BlockSpec((1, tile, LANES), lambda b, p, t: (b, t, offset + p))

    def prev(offset):
        return pl.BlockSpec((1, tile, LANES), lambda b, p, t: (b, jnp.maximum(t - 1, 0), offset + p))

    scratch = pltpu.VMEM((len(DIL_PAIRS), tile, LANES), F32)
    return pl.pallas_call(
        _dil_kernel, grid=(B, n_pairs, S // tile),
        in_specs=[cur(0), prev(n_pairs), cur(n_pairs), prev(0), cur(0)],
        out_specs=cur(0), out_shape=jax.ShapeDtypeStruct((B, S, W), F32), scratch_shapes=[scratch, scratch],
        compiler_params=_params(3), name="dil_prompt")(qk, qk, qk, v, v)


def _mem_sample_kernel(q_ref, k_ref, v_ref, o_ref):
    q = q_ref[0] * MEM_SCALE
    s = jnp.sum(k_ref[...] * q[None], axis=-1, keepdims=True)
    p = jnp.exp(s - jnp.max(s, axis=0, keepdims=True))
    o_ref[0] = jnp.sum(p * v_ref[...], axis=0) / jnp.sum(p, axis=0)


def _mem_sample(q, layer, cache_k, cache_v):
    NS, D = q.shape
    _, _, n_mem, H, Dh = cache_k.shape
    vec = pl.BlockSpec((1, H, Dh), lambda n: (n, 0, 0))
    blk = pl.BlockSpec((None, None, n_mem, H, Dh), lambda n: (layer, n, 0, 0, 0))
    out = pl.pallas_call(
        _mem_sample_kernel, grid=(NS,), in_specs=[vec, blk, blk], out_specs=vec,
        out_shape=jax.ShapeDtypeStruct((NS, H, Dh), F32),
        compiler_params=_params(1), name="mem_sample")(q.reshape(NS, H, Dh), cache_k, cache_v)
    return out.reshape(NS, D)


def _head_columns(x, n_heads):
    return x.reshape(x.shape[0], n_heads, HEAD_DIM, 1)


def _as_page(cols, page):
    pad = [(0, 0)] * (cols.ndim - 1) + [(0, page - 1)]
    return jnp.pad(cols, pad)


def _page_scores(kt, q_cols):
    return jnp.concatenate([jnp.sum(kt[h] * q_cols[h], axis=0, keepdims=True) for h in range(len(q_cols))], axis=0)


def _fox_sample_kernel(pt_ref, q_ref, *refs, pp):
    del pt_ref
    k_refs, v_refs, lf_refs = refs[:pp + 1], refs[pp + 1:2 * pp + 2], refs[2 * pp + 2:3 * pp + 3]
    o_ref, m_ref, l_ref, acc_ref, c_ref = refs[3 * pp + 3:]
    g = pl.program_id(1)
    H = q_ref.shape[1]
    R = k_refs[0].shape[-1]

    @pl.when(g == 0)
    def _():
        m_ref[...] = jnp.full_like(m_ref, -jnp.inf)
        l_ref[...] = jnp.zeros_like(l_ref)
        acc_ref[...] = jnp.zeros_like(acc_ref)
        c_ref[...] = jnp.zeros_like(c_ref)

    q_cols = [q_ref[0, h] * ATTN_SCALE for h in range(H)]
    row = lax.broadcasted_iota(jnp.int32, (R, R), 0)
    col = lax.broadcasted_iota(jnp.int32, (R, R), 1)
    upper = jnp.where(row <= col, 1.0, 0.0).astype(BF16)

    def update(ks, vs, lfs, valid):
        lf_all = jnp.concatenate([lf_ref[...] for lf_ref in lfs], axis=0)
        pieces = jnp.concatenate(_split3(lf_all), axis=0).astype(BF16)
        sums = jnp.dot(pieces, upper, preferred_element_type=F32)
        n_rows = lf_all.shape[0]
        within = sums[0:n_rows] + sums[n_rows:2 * n_rows] + sums[2 * n_rows:3 * n_rows]
        c_run = c_ref[:, 0:1]
        ss = []
        for i, k_ref in enumerate(ks):
            c = within[i * H:(i + 1) * H] + c_run
            c_run = c[:, R - 1:R]
            s = _page_scores(k_ref[...], q_cols) - c
            ss.append(s if valid is None else jnp.where(valid, s, -jnp.inf))
        m_old = m_ref[:, 0:1]
        m_new = functools.reduce(jnp.maximum, [jnp.max(s, axis=1, keepdims=True) for s in ss], m_old)
        alpha = jnp.exp(m_old - m_new)
        ps = [jnp.exp(s - m_new) for s in ss]
        m_ref[...] = jnp.broadcast_to(m_new, m_ref.shape)
        c_ref[...] = jnp.broadcast_to(c_run, c_ref.shape)
        for h in range(H):
            l_ref[h:h + 1, :] = alpha[h:h + 1, :] * l_ref[h:h + 1, :] + sum(p[h:h + 1, :] for p in ps)
            acc_ref[h] = alpha[h:h + 1, :] * acc_ref[h] + sum(v_ref[h] * p[h:h + 1, :] for v_ref, p in zip(vs, ps))

    update(k_refs[:pp], v_refs[:pp], lf_refs[:pp], None)

    @pl.when(g == pl.num_programs(1) - 1)
    def _():
        update(k_refs[pp:], v_refs[pp:], lf_refs[pp:], lax.broadcasted_iota(jnp.int32, (H, R), 1) < 1)
        for h in range(H):
            o_ref[0, h] = (jnp.sum(acc_ref[h], axis=1, keepdims=True)
                           / jnp.sum(l_ref[h:h + 1, :], axis=1, keepdims=True))


def _fox_sample(q, k_new, v_new, lf_new, page_table, layer, pool_kt, pool_vt, pool_lft, pp):
    NS, W = q.shape
    n_pages = page_table.shape[1]
    _, _, H, _, page = pool_kt.shape

    def paged(tail, i):
        zeros = (0,) * len(tail)
        return pl.BlockSpec((None, None) + tail, lambda n, g, pt, i=i: (layer, pt[n, g * pp + i]) + zeros)

    kv_new = pl.BlockSpec((None, H, HEAD_DIM, page), lambda n, g, pt: (n, 0, 0, 0))
    lf_new_spec = pl.BlockSpec((None, H, page), lambda n, g, pt: (n, 0, 0))
    cols = pl.BlockSpec((1, H, HEAD_DIM, 1), lambda n, g, pt: (n, 0, 0, 0))
    in_specs = [cols]
    in_specs += [paged((H, HEAD_DIM, page), i) for i in range(pp)] + [kv_new]
    in_specs += [paged((H, HEAD_DIM, page), i) for i in range(pp)] + [kv_new]
    in_specs += [paged((H, page), i) for i in range(pp)] + [lf_new_spec]
    stat = pltpu.VMEM((H, page), F32)
    out = pl.pallas_call(
        functools.partial(_fox_sample_kernel, pp=pp),
        grid_spec=pltpu.PrefetchScalarGridSpec(
            num_scalar_prefetch=1, grid=(NS, n_pages // pp), in_specs=in_specs, out_specs=cols,
            scratch_shapes=[stat, stat, pltpu.VMEM((H, HEAD_DIM, page), F32), stat]),
        out_shape=jax.ShapeDtypeStruct((NS, H, HEAD_DIM, 1), F32),
        compiler_params=_params(2), name="fox_sample")(
            page_table, _head_columns(q, H),
            *([pool_kt] * pp), _as_page(_head_columns(k_new, H), page),
            *([pool_vt] * pp), _as_page(_head_columns(v_new, H), page),
            *([pool_lft] * pp), _as_page(lf_new.reshape(NS, H, 1), page))
    return out.reshape(NS, W)


def _moba_sample_kernel(pt_ref, q_ref, kn_ref, vn_ref, *refs, bps, n_blocks):
    del pt_ref
    k_refs, v_refs = refs[:2 * bps], refs[2 * bps:4 * bps]
    o_ref, m_ref, l_ref, gate_ref, acc_ref = refs[4 * bps:]
    g = pl.program_id(1)
    H = q_ref.shape[1]
    R = k_refs[0].shape[-1]
    lane = lax.broadcasted_iota(jnp.int32, (H, R), 1)
    lane_d = lax.broadcasted_iota(jnp.int32, (HEAD_DIM, R), 1)

    @pl.when(g == 0)
    def _():
        m_ref[...] = jnp.zeros_like(m_ref)
        l_ref[...] = jnp.zeros_like(l_ref)
        gate_ref[...] = jnp.zeros_like(gate_ref)
        acc_ref[...] = jnp.zeros_like(acc_ref)

    q_cols = [q_ref[0, h] * ATTN_SCALE for h in range(H)]
    ms, ls, gates, probs = m_ref[...], l_ref[...], gate_ref[...], []
    for i in range(bps):
        sa = _page_scores(k_refs[2 * i][...], q_cols)
        sb = _page_scores(k_refs[2 * i + 1][...], q_cols)
        m = jnp.maximum(jnp.max(sa, axis=1, keepdims=True), jnp.max(sb, axis=1, keepdims=True))
        pa, pb = jnp.exp(sa - m), jnp.exp(sb - m)
        l = jnp.sum(pa, axis=1, keepdims=True) + jnp.sum(pb, axis=1, keepdims=True)
        gate = (jnp.sum(sa, axis=1, keepdims=True) + jnp.sum(sb, axis=1, keepdims=True)) * (
            1.0 / (ATTN_SCALE * MOBA_BLOCK))
        hit = lane == g * bps + i
        ms, ls, gates = jnp.where(hit, m, ms), jnp.where(hit, l, ls), jnp.where(hit, gate, gates)
        probs.append((pa, pb))
    m_ref[...], l_ref[...], gate_ref[...] = ms, ls, gates
    for h in range(H):
        acc = acc_ref[h]
        for i, (pa, pb) in enumerate(probs):
            pv = v_refs[2 * i][h] * pa[h:h + 1, :] + v_refs[2 * i + 1][h] * pb[h:h + 1, :]
            acc = jnp.where(lane_d == g * bps + i, jnp.sum(pv, axis=1, keepdims=True), acc)
        acc_ref[h] = acc

    @pl.when(g == pl.num_programs(1) - 1)
    def _():
        lane_f = lane.astype(F32)
        valid = lane < n_blocks
        gates = jnp.where(valid, gate_ref[...], -jnp.inf)
        sel = jnp.zeros((H, R), jnp.bool_)
        for _ in range(MOBA_TOPK):
            mx = jnp.max(gates, axis=1, keepdims=True)
            first = jnp.min(jnp.where(gates == mx, lane_f, float(R)), axis=1, keepdims=True)
            pick = (lane_f == first) & valid
            sel = sel | pick
            gates = jnp.where(pick, -jnp.inf, gates)
        s_new = _page_scores(kn_ref[...], q_cols)[:, 0:1]
        ms = m_ref[...]
        m_fin = jnp.maximum(jnp.max(jnp.where(sel, ms, -jnp.inf), axis=1, keepdims=True), s_new)
        w = jnp.where(sel, jnp.exp(ms - m_fin), 0.0)
        p_new = jnp.exp(s_new - m_fin)
        l = jnp.sum(w * l_ref[...], axis=1, keepdims=True) + p_new
        for h in range(H):
            num = jnp.sum(acc_ref[h] * w[h:h + 1, :], axis=1, keepdims=True) + p_new[h:h + 1, :] * vn_ref[h][:, 0:1]
            o_ref[0, h] = num / l[h:h + 1, :]


def _moba_sample(q, k_new, v_new, page_table, layer, pool_kt, pool_vt, bps):
    NS, W = q.shape
    n_pages = page_table.shape[1]
    _, _, H, _, page = pool_kt.shape
    assert MOBA_BLOCK == 2 * page and n_pages % (2 * bps) == 0
    n_blocks = n_pages // 2
    assert MOBA_TOPK <= n_blocks <= page
    pages = [pl.BlockSpec((None, None, H, HEAD_DIM, page),
                          lambda n, g, pt, i=i: (layer, pt[n, g * 2 * bps + i], 0, 0, 0)) for i in range(2 * bps)]
    kv_new = pl.BlockSpec((None, H, HEAD_DIM, page), lambda n, g, pt: (n, 0, 0, 0))
    cols = pl.BlockSpec((1, H, HEAD_DIM, 1), lambda n, g, pt: (n, 0, 0, 0))
    stat = pltpu.VMEM((H, page), F32)
    out = pl.pallas_call(
        functools.partial(_moba_sample_kernel, bps=bps, n_blocks=n_blocks),
        grid_spec=pltpu.PrefetchScalarGridSpec(
            num_scalar_prefetch=1, grid=(NS, n_blocks // bps),
            in_specs=[cols, kv_new, kv_new] + pages * 2, out_specs=cols,
            scratch_shapes=[stat, stat, stat, pltpu.VMEM((H, HEAD_DIM, page), F32)]),
        out_shape=jax.ShapeDtypeStruct((NS, H, HEAD_DIM, 1), F32),
        compiler_params=_params(2), name="moba_sample")(
            page_table, _head_columns(q, H), _as_page(_head_columns(k_new, H), page),
            _as_page(_head_columns(v_new, H), page), *([pool_kt] * (2 * bps)), *([pool_vt] * (2 * bps)))
    return out.reshape(NS, W)


def _dil_sample_kernel(q_ref, kn_ref, vn_ref, k_ref, v_ref, o_ref):
    hb, _, L = k_ref.shape
    dist = L - lax.broadcasted_iota(jnp.int32, (1, L), 1)
    mult = jnp.zeros((1, L), F32)
    for win, dil in DIL_PAIRS:
        mult = mult + jnp.where(((dist & (dil - 1)) == 0) & (dist <= win), 1.0, 0.0)
    valid = mult > 0.0
    n_br = float(len(DIL_PAIRS))
    for h in range(hb):
        qc = q_ref[0, h] * ATTN_SCALE
        s = jnp.sum(k_ref[h] * qc, axis=0, keepdims=True)
        s_new = jnp.sum(kn_ref[0, h] * qc, axis=0, keepdims=True)
        m = jnp.maximum(jnp.max(jnp.where(valid, s, -jnp.inf), axis=1, keepdims=True), s_new)
        p = jnp.where(valid, jnp.exp(s - m), 0.0) * mult
        p_new = n_br * jnp.exp(s_new - m)
        l = jnp.sum(p, axis=1, keepdims=True) + p_new
        o_ref[0, h] = (jnp.sum(v_ref[h] * p, axis=1, keepdims=True) + p_new * vn_ref[0, h]) / l


def _dil_sample(q, k_new, v_new, layer, buf_kt, buf_vt, hb):
    NS, W = q.shape
    _, _, H, _, L = buf_kt.shape
    assert all(dil & (dil - 1) == 0 and win == DIL_KEYS * dil and win <= L for win, dil in DIL_PAIRS)
    cols = pl.BlockSpec((1, hb, HEAD_DIM, 1), lambda n, j: (n, j, 0, 0))
    buf = pl.BlockSpec((None, None, hb, HEAD_DIM, L), lambda n, j: (layer, n, j, 0, 0))
    out = pl.pallas_call(
        _dil_sample_kernel, grid=(NS, H // hb), in_specs=[cols, cols, cols, buf, buf], out_specs=cols,
        out_shape=jax.ShapeDtypeStruct((NS, H, HEAD_DIM, 1), F32),
        compiler_params=_params(2), name="dil_sample")(
            _head_columns(q, H), _head_columns(k_new, H), _head_columns(v_new, H), buf_kt, buf_vt)
    return out.reshape(NS, W)


TM_PROJ = 512
TM_ROW = 256
TM_FFN = 512
TQ_FLASH = 1024
TK_FLASH = 512
DIL_TILE = 2048
FOX_PAGES_PER_STEP = 8
MOBA_BLOCKS_PER_STEP = 4
DIL_HEADS_PER_STEP = 4


def _positions_last(x):
    n = x.ndim
    return jnp.transpose(x, tuple(range(n - 3)) + (n - 2, n - 1, n - 3))


def _heads_from_t(xt, n_heads):
    B, _, T = xt.shape
    return jnp.transpose(xt.reshape(B, n_heads, HEAD_DIM, T), (0, 3, 1, 2))


def kernel(x_prompt, x_sample, mem_prompt, page_table, cache_fox_k, cache_fox_v, cache_fox_logf, cache_moba_k, cache_moba_v, cache_dwin_k, cache_dwin_v, cache_mem_k, cache_mem_v, state_conv, w_in_even, b_forget, w_out_even, w_in_odd, w_out_odd, w_mem_q, w_mem_k, w_mem_v, w_mem_o, w_ffn_in, ffn_conv_w, ffn_conv_b, w_ffn_out, ln_g, ln_b):
    B, S, D = x_prompt.shape
    NS, T, _ = x_sample.shape
    assert T == 1 and D == D_MODEL
    page = cache_fox_k.shape[2]
    P = page_table.shape[1] * page
    H8 = b_forget.shape[1]
    W8 = H8 * HEAD_DIM
    H16 = D // HEAD_DIM
    n_mem = mem_prompt.shape[1]
    L_C = cache_dwin_k.shape[2]
    assert P >= L_C
    keep = min(DIL_PAIRS[-1][0], S)

    pos_p = jnp.arange(S, dtype=jnp.int32)
    rope_p = _rope_tables(pos_p)
    rope_pt = _rope_tables_t(pos_p)
    rope_s = tuple(jnp.broadcast_to(t, (NS, LANES)) for t in _rope_tables(jnp.full((1,), P, jnp.int32)))

    fox_kt, fox_vt, moba_kt, moba_vt, dwin_kt, dwin_vt = (
        _positions_last(t) for t in (cache_fox_k, cache_fox_v, cache_moba_k, cache_moba_v, cache_dwin_k, cache_dwin_v))
    fox_lft = jnp.swapaxes(cache_fox_logf, -1, -2)

    hp = x_prompt.reshape(B * S, D)
    hs = x_sample.reshape(NS, D)
    mem2 = mem_prompt.reshape(B * n_mem, D)
    out = {name: [] for name in (
        "fox_k_p", "fox_v_p", "fox_l_p", "moba_k_p", "moba_v_p", "dwin_k_p", "dwin_v_p", "mem_k_p", "mem_v_p",
        "conv_p", "fox_k_s", "fox_v_s", "fox_l_s", "moba_k_s", "moba_v_s", "dwin_k_s", "dwin_v_s", "conv_s")}

    for l in range(DEPTH):
        g, bta = ln_g[l], ln_b[l]
        if l % 2 == 0:
            e = l // 2
            w = w_in_even[e]
            cuts = [0, W8, 2 * W8, 3 * W8, 3 * W8 + H8, 4 * W8 + H8, 5 * W8 + H8, 6 * W8 + H8]
            w_qa, w_ka, w_va, w_f, w_qb, w_kb, w_vb = (w[:, cuts[i]:cuts[i + 1]] for i in range(7))
            w_fox = jnp.concatenate([w_qa, w_ka], axis=1).astype(BF16)
            w_moba = jnp.concatenate([w_qb, w_kb], axis=1).astype(BF16)
            wo = w_out_even[e].astype(BF16)
            qk_a = _proj(hp, w_fox, TM_PROJ)
            lf, c = _forget(hp, w_f, b_forget[e], TM_ROW, S)
            qk_b = _proj(hp, w_moba, TM_PROJ, rope=rope_p)
            kt_a, vt_a, kt_b, vt_b = _proj_t(hp, [t.T.astype(BF16) for t in (w_ka, w_va, w_kb, w_vb)],
                                             (False, False, True, False), rope_pt, S, S, TM_PROJ)
            oa = _fox_prompt(qk_a.reshape(B, S, 2 * W8), vt_a, c[:, :H8].reshape(B, S, H8), TQ_FLASH, TK_FLASH)
            ob = _moba_prompt(qk_b.reshape(B, S, 2 * W8), vt_b, TQ_FLASH, TK_FLASH)
            hp = _out_ln([oa.reshape(B * S, W8), ob.reshape(B * S, W8)], [wo[:W8], wo[W8:]], hp,
                         g[0], bta[0], TM_ROW)
            out["fox_k_p"].append(_heads_from_t(kt_a, H8))
            out["fox_v_p"].append(_heads_from_t(vt_a, H8))
            out["fox_l_p"].append(lf[:, :H8].reshape(B, S, H8))
            out["moba_k_p"].append(_heads_from_t(kt_b, H8))
            out["moba_v_p"].append(_heads_from_t(vt_b, H8))
            fox = _proj(hs, jnp.concatenate([w_fox, w_va.astype(BF16)], axis=1), NS)
            qa, ka, va = (fox[:, i * W8:(i + 1) * W8] for i in range(3))
            lf, _ = _forget(hs, w_f, b_forget[e], NS, NS)
            lf = lf[:, :H8]
            mqk = _proj(hs, w_moba, NS, rope=rope_s)
            qb, kb = (mqk[:, i * W8:(i + 1) * W8] for i in range(2))
            vb = _proj(hs, w_vb.astype(BF16), NS)
            oa = _fox_sample(qa, ka, va, lf, page_table, e, fox_kt, fox_vt, fox_lft, FOX_PAGES_PER_STEP)
            ob = _moba_sample(qb, kb, vb, page_table, e, moba_kt, moba_vt, MOBA_BLOCKS_PER_STEP)
            hs = _out_ln([oa, ob], [wo[:W8], wo[W8:]], hs, g[0], bta[0], NS)
            out["fox_k_s"].append(ka.reshape(NS, 1, H8, HEAD_DIM))
            out["fox_v_s"].append(va.reshape(NS, 1, H8, HEAD_DIM))
            out["fox_l_s"].append(lf.reshape(NS, 1, H8))
            out["moba_k_s"].append(kb.reshape(NS, 1, H8, HEAD_DIM))
            out["moba_v_s"].append(vb.reshape(NS, 1, H8, HEAD_DIM))
        else:
            od = l // 2
            w = w_in_odd[od]
            w_qk = w[:, :2 * D].astype(BF16)
            w_k, w_v = w[:, D:2 * D], w[:, 2 * D:]
            wo = w_out_odd[od].astype(BF16)
            qk = _proj(hp, w_qk, TM_PROJ, rope=rope_p).reshape(B, S, 2 * D)
            v = _proj(hp, w_v.astype(BF16), TM_PROJ).reshape(B, S, D)
            kt, vt = _proj_t(hp, [w_k.T.astype(BF16), w_v.T.astype(BF16)], (True, False), rope_pt, S, keep, TM_PROJ)
            o = _dil_prompt(qk, v, DIL_TILE)
            hp = _out_ln([o.reshape(B * S, D)], [wo], hp, g[0], bta[0], TM_ROW)
            out["dwin_k_p"].append(_heads_from_t(kt, H16))
            out["dwin_v_p"].append(_heads_from_t(vt, H16))
            qk = _proj(hs, w_qk, NS, rope=rope_s)
            q, k = qk[:, :D], qk[:, D:]
            v = _proj(hs, w_v.astype(BF16), NS)
            o = _dil_sample(q, k, v, od, dwin_kt, dwin_vt, DIL_HEADS_PER_STEP)
            hs = _out_ln([o], [wo], hs, g[0], bta[0], NS)
            out["dwin_k_s"].append(k.reshape(NS, 1, H16, HEAD_DIM))
            out["dwin_v_s"].append(v.reshape(NS, 1, H16, HEAD_DIM))
        wq, wk, wv, wo = (t[l].astype(BF16) for t in (w_mem_q, w_mem_k, w_mem_v, w_mem_o))
        mk = _proj(mem2, wk, TM_ROW).reshape(B, n_mem, D)
        mv = _proj(mem2, wv, TM_ROW).reshape(B, n_mem, D)
        out["mem_k_p"].append(mk.reshape(B, n_mem, N_MEM_HEADS, MEM_HEAD_DIM))
        out["mem_v_p"].append(mv.reshape(B, n_mem, N_MEM_HEADS, MEM_HEAD_DIM))
        hp = _mem_block(hp, wq, mk, mv, wo, g[1], bta[1], TM_ROW, S)
        qs_ = _proj(hs, wq, NS)
        os_ = _mem_sample(qs_, l, cache_mem_k, cache_mem_v)
        hs = _out_ln([os_], [wo], hs, g[1], bta[1], NS)
        win, wout = w_ffn_in[l].astype(BF16), w_ffn_out[l].astype(BF16)
        hp, cp = _ffn_prompt(hp, win, ffn_conv_w[l], ffn_conv_b[l], wout, g[2], bta[2], TM_FFN, S)
        st = state_conv[l]
        hs, u = _ffn_sample(hs, st[:, 0], st[:, 1], win, ffn_conv_w[l], ffn_conv_b[l], wout, g[2], bta[2])
        out["conv_p"].append(cp)
        out["conv_s"].append(jnp.stack([st[:, 1], u], axis=1))

    st = jnp.stack
    return (hp.reshape(B, S, D), hs.reshape(NS, 1, D),
            st(out["fox_k_p"]), st(out["fox_v_p"]), st(out["fox_l_p"]), st(out["moba_k_p"]), st(out["moba_v_p"]),
            st(out["dwin_k_p"]), st(out["dwin_v_p"]), st(out["mem_k_p"]), st(out["mem_v_p"]), st(out["conv_p"]),
            st(out["fox_k_s"]), st(out["fox_v_s"]), st(out["fox_l_s"]), st(out["moba_k_s"]), st(out["moba_v_s"]),
            st(out["dwin_k_s"]), st(out["dwin_v_s"]), st(out["conv_s"]))
```

```python
import functools
import math

import jax
import jax.numpy as jnp
from jax import lax
from jax.experimental import pallas as pl
from jax.experimental.pallas import tpu as pltpu

F32 = jnp.float32
BF16 = jnp.bfloat16

LANES = 128

D_MODEL = 1024
HEAD_DIM = 64
HEADS_PER_VREG = LANES // HEAD_DIM
N_MEM_HEADS = 4
MEM_HEAD_DIM = D_MODEL // N_MEM_HEADS
D_FF = 2816
FFN_CHUNK = 256
MOBA_BLOCK = 256
MOBA_BLOCK_SHIFT = 8
MOBA_TOPK = 3
DIL_PAIRS = ((128, 1), (512, 4), (2048, 16))
DIL_KEYS = 128
ROPE_THETA = 10000.0
LN_EPS = 1e-5
DEPTH = 4
DN_ALPHA = (2 * DEPTH) ** 0.25
ATTN_SCALE = HEAD_DIM ** -0.5
MEM_SCALE = MEM_HEAD_DIM ** -0.5
LOG2E = math.log2(math.e)
NEG_BIG = -1e30
VMEM_LIMIT = 56 * 2 ** 20

NT_DIMS = (((1,), (1,)), ((), ()))
HIGHEST = lax.Precision.HIGHEST


def _params(n_axes):
    return pltpu.CompilerParams(dimension_semantics=("arbitrary",) * n_axes,
                                vmem_limit_bytes=VMEM_LIMIT)


def _const_spec(shape):
    zeros = (0,) * len(shape)
    return pl.BlockSpec(shape, lambda *_: zeros, pipeline_mode=pl.Buffered(1))


def _layer_norm(y, g, b):
    mu = jnp.mean(y, axis=-1, keepdims=True)
    yc = y - mu
    var = jnp.mean(yc * yc, axis=-1, keepdims=True)
    return yc * lax.rsqrt(var + LN_EPS) * g + b


def _rope_angles(pos):
    half = HEAD_DIM // 2
    inv = ROPE_THETA ** (-jnp.arange(half, dtype=F32) / half)
    ang = pos.astype(F32)[:, None] * inv[None, :]
    return jnp.cos(ang), jnp.sin(ang)


def _rope_tables(pos):
    cos, sin = _rope_angles(pos)
    cos2 = jnp.tile(jnp.concatenate([cos, cos], axis=-1), (1, HEADS_PER_VREG))
    sin2 = jnp.tile(jnp.concatenate([-sin, sin], axis=-1), (1, HEADS_PER_VREG))
    return cos2, sin2


def _rope_tables_t(pos):
    cos, sin = _rope_angles(pos)
    return jnp.concatenate([cos, cos], axis=-1).T, jnp.concatenate([-sin, sin], axis=-1).T


def _proj_kernel(x_ref, w_ref, o_ref):
    o_ref[...] = jnp.dot(x_ref[...].astype(BF16), w_ref[...], preferred_element_type=F32)


def _proj_rope_kernel(x_ref, w_ref, cos_ref, sin_ref, o_ref):
    z = jnp.dot(x_ref[...].astype(BF16), w_ref[...], preferred_element_type=F32)
    tm, tn = z.shape
    cos, sin = cos_ref[...], sin_ref[...]
    lane = lax.broadcasted_iota(jnp.int32, (tm, LANES), 1)
    first_half = (lane & (HEAD_DIM - 1)) < HEAD_DIM // 2
    for c in range(tn // LANES):
        zc = z[:, c * LANES:(c + 1) * LANES]
        partner = jnp.where(first_half, pltpu.roll(zc, LANES - HEAD_DIM // 2, 1),
                            pltpu.roll(zc, HEAD_DIM // 2, 1))
        o_ref[:, c * LANES:(c + 1) * LANES] = zc * cos + partner * sin


def _proj(x, w, tm, rope=None):
    M, K = x.shape
    N = w.shape[1]
    x_spec = pl.BlockSpec((tm, K), lambda i: (i, 0))
    o_spec = pl.BlockSpec((tm, N), lambda i: (i, 0))
    out_shape = jax.ShapeDtypeStruct((M, N), F32)
    if rope is None:
        return pl.pallas_call(_proj_kernel, grid=(M // tm,), in_specs=[x_spec, _const_spec((K, N))],
                              out_specs=o_spec, out_shape=out_shape, compiler_params=_params(1), name="proj")(x, w)
    cos2, sin2 = rope
    pos_blocks = cos2.shape[0] // tm
    t_spec = pl.BlockSpec((tm, LANES), lambda i: (i % pos_blocks, 0))
    return pl.pallas_call(_proj_rope_kernel, grid=(M // tm,), in_specs=[x_spec, _const_spec((K, N)), t_spec, t_spec],
                          out_specs=o_spec, out_shape=out_shape, compiler_params=_params(1),
                          name="proj_rope")(x, w, cos2, sin2)


def _proj_t_kernel(*refs, ropes):
    n = len(ropes)
    x_ref, wt_refs, cos_ref, sin_ref, o_refs = refs[0], refs[1:1 + n], refs[1 + n], refs[2 + n], refs[3 + n:]
    xb = x_ref[...].astype(BF16)
    half = HEAD_DIM // 2
    for wt_ref, o_ref, rope in zip(wt_refs, o_refs, ropes):
        z = lax.dot_general(wt_ref[...], xb, NT_DIMS, preferred_element_type=F32)
        if not rope:
            o_ref[0] = z
            continue
        cos, sin = cos_ref[...], sin_ref[...]
        for g in range(z.shape[0] // HEAD_DIM):
            zg = z[g * HEAD_DIM:(g + 1) * HEAD_DIM]
            partner = jnp.concatenate([zg[half:], zg[:half]], axis=0)
            o_ref[0, g * HEAD_DIM:(g + 1) * HEAD_DIM, :] = zg * cos + partner * sin


def _proj_t(x, wts, ropes, rope_t, seq_len, keep, tm):
    M, K = x.shape
    B = M // seq_len
    tiles_in, tiles_out = seq_len // tm, keep // tm
    first = tiles_in - tiles_out
    x_spec = pl.BlockSpec((tm, K), lambda i: ((i // tiles_out) * tiles_in + first + i % tiles_out, 0))
    t_spec = pl.BlockSpec((HEAD_DIM, tm), lambda i: (0, first + i % tiles_out))
    o_specs = [pl.BlockSpec((1, wt.shape[0], tm), lambda i: (i // tiles_out, 0, i % tiles_out)) for wt in wts]
    return pl.pallas_call(
        functools.partial(_proj_t_kernel, ropes=tuple(ropes)), grid=(B * tiles_out,),
        in_specs=[x_spec] + [_const_spec(wt.shape) for wt in wts] + [t_spec, t_spec], out_specs=o_specs,
        out_shape=[jax.ShapeDtypeStruct((B, wt.shape[0], keep), F32) for wt in wts],
        compiler_params=_params(1), name="proj_t")(x, *wts, *rope_t)


def _forget_kernel(x_ref, w_ref, b_ref, lf_ref, c_ref, carry_ref, *, tiles_per_seq):
    i = pl.program_id(0)
    z = jnp.dot(x_ref[...].astype(BF16), w_ref[...], preferred_element_type=F32) + b_ref[...]
    lf = -(jnp.maximum(-z, 0.0) + jnp.log1p(jnp.exp(-jnp.abs(z))))
    lf_ref[...] = lf

    @pl.when(i % tiles_per_seq == 0)
    def _():
        carry_ref[...] = jnp.zeros_like(carry_ref)

    tm = lf.shape[0]
    row = lax.broadcasted_iota(jnp.int32, (tm, tm), 0)
    col = lax.broadcasted_iota(jnp.int32, (tm, tm), 1)
    tri = jnp.where(col <= row, 1.0, 0.0).astype(F32)
    c = jnp.dot(tri, lf, precision=HIGHEST, preferred_element_type=F32) + carry_ref[0:1, :]
    c_ref[...] = c
    carry_ref[0:1, :] = c[tm - 1:tm, :]


def _forget(x, w_f, b_f, tm, seq_len):
    M, K = x.shape
    n_heads = w_f.shape[1]
    w = jnp.zeros((K, LANES), BF16).at[:, :n_heads].set(w_f.astype(BF16))
    b = jnp.zeros((1, LANES), F32).at[0, :n_heads].set(b_f)
    kern = functools.partial(_forget_kernel, tiles_per_seq=seq_len // tm)
    out = jax.ShapeDtypeStruct((M, LANES), F32)
    return pl.pallas_call(
        kern, grid=(M // tm,),
        in_specs=[pl.BlockSpec((tm, K), lambda i: (i, 0)), _const_spec((K, LANES)), _const_spec((1, LANES))],
        out_specs=[pl.BlockSpec((tm, LANES), lambda i: (i, 0))] * 2,
        out_shape=[out, out], scratch_shapes=[pltpu.VMEM((8, LANES), F32)],
        compiler_params=_params(1), name="forget")(x, w, b)


def _out_ln_kernel(*refs, n_in):
    a_refs, w_refs = refs[:n_in], refs[n_in:2 * n_in]
    x_ref, g_ref, b_ref, o_ref = refs[2 * n_in:]
    acc = None
    for a_ref, w_ref in zip(a_refs, w_refs):
        d = jnp.dot(a_ref[...].astype(BF16), w_ref[...], preferred_element_type=F32)
        acc = d if acc is None else acc + d
    o_ref[...] = _layer_norm(DN_ALPHA * x_ref[...] + acc, g_ref[...], b_ref[...])


def _out_ln(a_list, w_list, x, g, b, tm):
    M, D = x.shape
    n_in = len(a_list)
    in_specs = [pl.BlockSpec((tm, a.shape[1]), lambda i: (i, 0)) for a in a_list]
    in_specs += [_const_spec(w.shape) for w in w_list]
    in_specs += [pl.BlockSpec((tm, D), lambda i: (i, 0)), _const_spec((1, D)), _const_spec((1, D))]
    return pl.pallas_call(
        functools.partial(_out_ln_kernel, n_in=n_in), grid=(M // tm,), in_specs=in_specs,
        out_specs=pl.BlockSpec((tm, D), lambda i: (i, 0)), out_shape=jax.ShapeDtypeStruct((M, D), F32),
        compiler_params=_params(1), name="out_ln")(*a_list, *w_list, x, g.reshape(1, D), b.reshape(1, D))


def _mem_kernel(x_ref, wq_ref, mk_ref, mv_ref, wo_ref, g_ref, b_ref, o_ref):
    x = x_ref[...]
    q = jnp.dot(x.astype(BF16), wq_ref[...], preferred_element_type=F32)
    mk = mk_ref[0].astype(BF16)
    mv = mv_ref[0].astype(BF16)
    heads = []
    for h in range(N_MEM_HEADS):
        sl = slice(h * MEM_HEAD_DIM, (h + 1) * MEM_HEAD_DIM)
        qh = (q[:, sl] * MEM_SCALE).astype(BF16)
        s = lax.dot_general(qh, mk[:, sl], NT_DIMS, preferred_element_type=F32)
        e = jnp.exp(s - jnp.max(s, axis=-1, keepdims=True))
        l = jnp.sum(e, axis=-1, keepdims=True)
        heads.append(jnp.dot(e.astype(BF16), mv[:, sl], preferred_element_type=F32) / l)
    o = jnp.concatenate(heads, axis=-1)
    mix = jnp.dot(o.astype(BF16), wo_ref[...], preferred_element_type=F32)
    o_ref[...] = _layer_norm(DN_ALPHA * x + mix, g_ref[...], b_ref[...])


def _mem_block(x, wq, mk, mv, wo, g, b, tm, seq_len):
    M, D = x.shape
    n_mem = mk.shape[1]
    tiles_per_seq = seq_len // tm
    row = pl.BlockSpec((tm, D), lambda i: (i, 0))
    mem = pl.BlockSpec((1, n_mem, D), lambda i: (i // tiles_per_seq, 0, 0))
    return pl.pallas_call(
        _mem_kernel, grid=(M // tm,),
        in_specs=[row, _const_spec((D, D)), mem, mem, _const_spec((D, D)),
                  _const_spec((1, D)), _const_spec((1, D))],
        out_specs=row, out_shape=jax.ShapeDtypeStruct((M, D), F32),
        compiler_params=_params(1), name="mem_block")(x, wq, mk, mv, wo, g.reshape(1, D), b.reshape(1, D))


def _gated(ha, hg):
    return ha * (0.5 * hg * (1.0 + lax.erf(hg * (2.0 ** -0.5))))


def _ffn_prompt_kernel(x_ref, win_ref, cw_ref, cb_ref, wout_ref, g_ref, b_ref, o_ref, st_ref, carry_ref, act_ref,
                       *, tiles_per_seq):
    i = pl.program_id(0)

    @pl.when(i % tiles_per_seq == 0)
    def _():
        carry_ref[...] = jnp.zeros_like(carry_ref)

    x = x_ref[...]
    xb = x.astype(BF16)
    tm = x.shape[0]
    row = lax.broadcasted_iota(jnp.int32, (tm, FFN_CHUNK), 0)
    for c in range(D_FF // FFN_CHUNK):
        hs = []
        for off in (c * FFN_CHUNK, D_FF + c * FFN_CHUNK):
            sl = slice(off, off + FFN_CHUNK)
            u = jnp.dot(xb, win_ref[:, sl], preferred_element_type=F32)
            prev2 = carry_ref[0:1, sl]
            prev1 = carry_ref[1:2, sl]
            u1 = jnp.where(row == 0, prev1, pltpu.roll(u, 1, 0))
            u2 = jnp.where(row == 0, prev2, jnp.where(row == 1, prev1, pltpu.roll(u, 2, 0)))
            last = u[tm - 2:tm, :]
            carry_ref[0:2, sl] = last
            st_ref[0, :, sl] = last
            hs.append(cb_ref[:, sl] + cw_ref[0:1, sl] * u2 + cw_ref[1:2, sl] * u1 + cw_ref[2:3, sl] * u)
        act_ref[:, c * FFN_CHUNK:(c + 1) * FFN_CHUNK] = _gated(hs[0], hs[1]).astype(BF16)
    mix = jnp.dot(act_ref[...], wout_ref[...], preferred_element_type=F32)
    o_ref[...] = _layer_norm(DN_ALPHA * x + mix, g_ref[...], b_ref[...])


def _ffn_prompt(x, win, cw, cb, wout, g, b, tm, seq_len):
    M, D = x.shape
    F2 = win.shape[1]
    tiles_per_seq = seq_len // tm
    row = pl.BlockSpec((tm, D), lambda i: (i, 0))
    return pl.pallas_call(
        functools.partial(_ffn_prompt_kernel, tiles_per_seq=tiles_per_seq), grid=(M // tm,),
        in_specs=[row, _const_spec((D, F2)), _const_spec((3, F2)), _const_spec((1, F2)),
                  _const_spec((F2 // 2, D)), _const_spec((1, D)), _const_spec((1, D))],
        out_specs=[row, pl.BlockSpec((1, 2, F2), lambda i: (i // tiles_per_seq, 0, 0))],
        out_shape=[jax.ShapeDtypeStruct((M, D), F32), jax.ShapeDtypeStruct((M // seq_len, 2, F2), F32)],
        scratch_shapes=[pltpu.VMEM((8, F2), F32), pltpu.VMEM((tm, F2 // 2), BF16)],
        compiler_params=_params(1), name="ffn_prompt")(x, win, cw, cb.reshape(1, F2), wout,
                                                       g.reshape(1, D), b.reshape(1, D))


def _ffn_sample_kernel(x_ref, s0_ref, s1_ref, win_ref, cw_ref, cb_ref, wout_ref, g_ref, b_ref, o_ref, u_ref):
    x = x_ref[...]
    xb = x.astype(BF16)
    acc = jnp.zeros(x.shape, F32)
    for c in range(D_FF // FFN_CHUNK):
        hs = []
        for off in (c * FFN_CHUNK, D_FF + c * FFN_CHUNK):
            sl = slice(off, off + FFN_CHUNK)
            u = jnp.dot(xb, win_ref[:, sl], preferred_element_type=F32)
            u_ref[:, sl] = u
            hs.append(cb_ref[:, sl] + cw_ref[0:1, sl] * s0_ref[:, sl] + cw_ref[1:2, sl] * s1_ref[:, sl]
                      + cw_ref[2:3, sl] * u)
        act = _gated(hs[0], hs[1])
        acc = acc + jnp.dot(act.astype(BF16), wout_ref[c * FFN_CHUNK:(c + 1) * FFN_CHUNK, :],
                            preferred_element_type=F32)
    o_ref[...] = _layer_norm(DN_ALPHA * x + acc, g_ref[...], b_ref[...])


def _ffn_sample(x, s0, s1, win, cw, cb, wout, g, b):
    M, D = x.shape
    F2 = win.shape[1]
    return pl.pallas_call(
        _ffn_sample_kernel, grid=(1,),
        in_specs=[_const_spec((M, D)), _const_spec((M, F2)), _const_spec((M, F2)), _const_spec((D, F2)),
                  _const_spec((3, F2)), _const_spec((1, F2)), _const_spec((F2 // 2, D)),
                  _const_spec((1, D)), _const_spec((1, D))],
        out_specs=[pl.BlockSpec((M, D), lambda i: (0, 0)), pl.BlockSpec((M, F2), lambda i: (0, 0))],
        out_shape=[jax.ShapeDtypeStruct((M, D), F32), jax.ShapeDtypeStruct((M, F2), F32)],
        compiler_params=_params(1), name="ffn_sample")(x, s0, s1, win, cw, cb.reshape(1, F2), wout,
                                                       g.reshape(1, D), b.reshape(1, D))


def _lane_masks(rows):
    lane = lax.broadcasted_iota(jnp.int32, (rows, LANES), 1)
    is_h0 = lane < HEAD_DIM
    return lane, (is_h0, jnp.logical_not(is_h0))


def _flash_core(ka_refs, vb_ref, s_ref, q_aug, qi, tq, tk):
    assert tq == 2 * tk
    qs = qi * tq

    def scores(ks, slot):
        for h in range(HEADS_PER_VREG):
            s_ref[slot, h] = lax.dot_general(ka_refs[h][pl.ds(ks, tk), :], q_aug[h], NT_DIMS,
                                             preferred_element_type=F32)

    def consume(ks, slot, states, mask):
        new = []
        for h in range(HEADS_PER_VREG):
            m, l, acc = states[h]
            s = s_ref[slot, h]
            if mask is not None:
                s = jnp.where(mask, s, -jnp.inf)
            m_new = jnp.maximum(m, jnp.max(s, axis=0, keepdims=True))
            alpha = jnp.exp2(m - m_new)
            p = jnp.exp2(s - m_new)
            l = alpha * l + jnp.sum(p, axis=0, keepdims=True)
            vt = vb_ref[h * HEAD_DIM:(h + 1) * HEAD_DIM, pl.ds(ks, tk)]
            acc = alpha * acc + jnp.dot(vt, p.astype(BF16), preferred_element_type=F32)
            new.append((m_new, l, acc))
        return tuple(new)

    def chunk(j):
        return pl.multiple_of(j * tk, tk)

    def pair(jj, states):
        scores(chunk(2 * jj + 1), 1)
        states = consume(chunk(2 * jj), 0, states, None)
        scores(chunk(2 * jj + 2), 0)
        return consume(chunk(2 * jj + 1), 1, states, None)

    init = (jnp.full((1, tq), -jnp.inf, F32), jnp.zeros((1, tq), F32), jnp.zeros((HEAD_DIM, tq), F32))
    scores(0, 0)
    states = lax.fori_loop(0, qi, pair, (init, init))
    kpos = lax.broadcasted_iota(jnp.int32, (tk, tq), 0)
    qpos = lax.broadcasted_iota(jnp.int32, (tk, tq), 1)
    scores(chunk(2 * qi + 1), 1)
    states = consume(chunk(2 * qi), 0, states, kpos <= qpos)
    states = consume(chunk(2 * qi + 1), 1, states, kpos + tk <= qpos)
    (_, l0, a0), (_, l1, a1) = states
    return jnp.concatenate([a0 / l0, a1 / l1], axis=0).T


def _split3(x):
    hi = x.astype(BF16).astype(F32)
    r = x - hi
    mid = r.astype(BF16).astype(F32)
    lo = (r - mid).astype(BF16).astype(F32)
    return hi, mid, lo


def _place(lane, base, cols):
    out = jnp.zeros(lane.shape, F32)
    for i, v in enumerate(cols):
        out = jnp.where(lane == base + i, v, out)
    return out


def _fox_kernel(q_ref, k_ref, vt_ref, cc_ref, o_ref, ka0_ref, ka1_ref, vb_ref, s_ref, *, tq, tk):
    qi = pl.program_id(2)
    S = k_ref.shape[1]
    fill = MOBA_BLOCK
    ka_refs = (ka0_ref, ka1_ref)

    @pl.when(qi == 0)
    def _():
        vb_ref[...] = vt_ref[0].astype(BF16)
        lane, in_head = _lane_masks(fill)

        def body(j, _):
            rs = pl.multiple_of(j * fill, fill)
            kc = k_ref[0, pl.ds(rs, fill), :]
            cc = cc_ref[0, 0, pl.ds(rs, fill), :] * LOG2E
            for h in range(HEADS_PER_VREG):
                hi, mid, lo = _split3(cc[:, h:h + 1])
                one = jnp.ones_like(hi)
                extra = _place(lane, (1 - h) * HEAD_DIM, (-hi, -mid, -lo, one, one, one))
                ka_refs[h][pl.ds(rs, fill), :] = jnp.where(in_head[h], kc, extra).astype(BF16)
            return 0

        lax.fori_loop(0, S // fill, body, 0)

    lane, in_head = _lane_masks(tq)
    q = q_ref[0] * (ATTN_SCALE * LOG2E)
    cq = cc_ref[0, 0, pl.ds(pl.multiple_of(qi * tq, tq), tq), :] * LOG2E
    q_aug = []
    for h in range(HEADS_PER_VREG):
        hi, mid, lo = _split3(cq[:, h:h + 1])
        one = jnp.ones_like(hi)
        extra = _place(lane, (1 - h) * HEAD_DIM, (one, one, one, hi, mid, lo))
        q_aug.append(jnp.where(in_head[h], q, extra).astype(BF16))
    o_ref[0] = _flash_core(ka_refs, vb_ref, s_ref, q_aug, qi, tq, tk)


def _fox_prompt(qk, vt, c, tq, tk):
    B, S, W2 = qk.shape
    W = W2 // 2
    n_pairs = W // LANES
    cc = c.reshape(B, S, n_pairs, HEADS_PER_VREG).transpose(0, 2, 1, 3)
    one = pl.Buffered(1)
    return pl.pallas_call(
        functools.partial(_fox_kernel, tq=tq, tk=tk), grid=(B, n_pairs, S // tq),
        in_specs=[pl.BlockSpec((1, tq, LANES), lambda b, p, i: (b, i, p)),
                  pl.BlockSpec((1, S, LANES), lambda b, p, i: (b, 0, n_pairs + p), pipeline_mode=one),
                  pl.BlockSpec((1, LANES, S), lambda b, p, i: (b, p, 0), pipeline_mode=one),
                  pl.BlockSpec((1, 1, S, HEADS_PER_VREG), lambda b, p, i: (b, p, 0, 0), pipeline_mode=one)],
        out_specs=pl.BlockSpec((1, tq, LANES), lambda b, p, i: (b, i, p)),
        out_shape=jax.ShapeDtypeStruct((B, S, W), F32),
        scratch_shapes=[pltpu.VMEM((S, LANES), BF16), pltpu.VMEM((S, LANES), BF16), pltpu.VMEM((LANES, S), BF16),
                        pltpu.VMEM((2, HEADS_PER_VREG, tk, tq), F32)],
        compiler_params=_params(3), name="fox_prompt")(qk, qk, vt, cc)


def _moba_kernel(q_ref, k_ref, vt_ref, o_ref, ka0_ref, ka1_ref, vb_ref, kmp_ref, s_ref, *, tq, tk):
    qi = pl.program_id(2)
    S = k_ref.shape[1]
    blk = MOBA_BLOCK
    ka_refs = (ka0_ref, ka1_ref)

    @pl.when(qi == 0)
    def _():
        vb_ref[...] = vt_ref[0].astype(BF16)
        kmp_ref[...] = jnp.zeros_like(kmp_ref)
        lane, in_head = _lane_masks(blk)
        blk_lane = lane & (HEAD_DIM - 1)

        def body(j, _):
            rs = pl.multiple_of(j * blk, blk)
            kc = k_ref[0, pl.ds(rs, blk), :]
            onehot = jnp.where(blk_lane == j, 1.0, 0.0)
            for h in range(HEADS_PER_VREG):
                ka_refs[h][pl.ds(rs, blk), :] = jnp.where(in_head[h], kc, onehot).astype(BF16)
            kmean = jnp.sum(kc, axis=0, keepdims=True) * (1.0 / blk)
            h0row = in_head[0][0:1, :]
            kmp_ref[pl.ds(HEAD_DIM + j, 1), :] = jnp.where(h0row, kmean, 0.0)
            kmp_ref[pl.ds(j, 1), :] = jnp.where(h0row, 0.0, kmean)
            return 0

        lax.fori_loop(0, S // blk, body, 0)

    _, in_head = _lane_masks(tq)
    q = q_ref[0]
    gate_t = lax.dot_general(kmp_ref[...], q, NT_DIMS, precision=HIGHEST, preferred_element_type=F32)
    blk = lax.broadcasted_iota(jnp.int32, (HEAD_DIM, tq), 0)
    blk_f = blk.astype(F32)
    qpos = qi * tq + lax.broadcasted_iota(jnp.int32, (HEAD_DIM, tq), 1)
    own = lax.shift_right_logical(qpos, MOBA_BLOCK_SHIFT)
    valid = blk < own
    halves = []
    for rows in range(HEADS_PER_VREG):
        g = jnp.where(valid, gate_t[rows * HEAD_DIM:(rows + 1) * HEAD_DIM], -jnp.inf)
        sel = blk == own
        for _ in range(MOBA_TOPK):
            mx = jnp.max(g, axis=0, keepdims=True)
            first = jnp.min(jnp.where(g == mx, blk_f, float(HEAD_DIM)), axis=0, keepdims=True)
            pick = (blk_f == first) & valid
            sel = sel | pick
            g = jnp.where(pick, -jnp.inf, g)
        halves.append(jnp.where(sel, 0.0, NEG_BIG))
    bias = jnp.concatenate(halves, axis=0).T
    qsc = q * (ATTN_SCALE * LOG2E)
    q_aug = [jnp.where(in_head[h], qsc, bias).astype(BF16) for h in range(HEADS_PER_VREG)]
    o_ref[0] = _flash_core(ka_refs, vb_ref, s_ref, q_aug, qi, tq, tk)


def _moba_prompt(qk, vt, tq, tk):
    B, S, W2 = qk.shape
    W = W2 // 2
    n_pairs = W // LANES
    assert S % MOBA_BLOCK == 0 and S // MOBA_BLOCK <= HEAD_DIM
    assert tk % MOBA_BLOCK == 0 and tq % tk == 0
    one = pl.Buffered(1)
    return pl.pallas_call(
        functools.partial(_moba_kernel, tq=tq, tk=tk), grid=(B, n_pairs, S // tq),
        in_specs=[pl.BlockSpec((1, tq, LANES), lambda b, p, i: (b, i, p)),
                  pl.BlockSpec((1, S, LANES), lambda b, p, i: (b, 0, n_pairs + p), pipeline_mode=one),
                  pl.BlockSpec((1, LANES, S), lambda b, p, i: (b, p, 0), pipeline_mode=one)],
        out_specs=pl.BlockSpec((1, tq, LANES), lambda b, p, i: (b, i, p)),
        out_shape=jax.ShapeDtypeStruct((B, S, W), F32),
        scratch_shapes=[pltpu.VMEM((S, LANES), BF16), pltpu.VMEM((S, LANES), BF16),
                        pltpu.VMEM((LANES, S), BF16), pltpu.VMEM((LANES, LANES), F32),
                        pltpu.VMEM((2, HEADS_PER_VREG, tk, tq), F32)],
        compiler_params=_params(3), name="moba_prompt")(qk, qk, vt)


def _class_rows(ref, dil, cls, first, count):
    if dil == 1:
        return ref[0, first:first + count, :]
    return ref[0, pl.ds(cls + dil * first, count, stride=dil), :]


def _dil_kernel(q_ref, kp_ref, kc_ref, vp_ref, vc_ref, o_ref, ob_ref, eb_ref):
    t = pl.program_id(2)
    tile = q_ref.shape[1]
    sub = DIL_KEYS
    _, (is_h0, _) = _lane_masks(sub)
    a = lax.broadcasted_iota(jnp.int32, (sub, 2 * sub), 0)
    c = lax.broadcasted_iota(jnp.int32, (sub, 2 * sub), 1)
    band = (c >= a) & (c <= a + sub)
    band_first = band & (c >= jnp.where(t > 0, 0, sub))
    for bi, (_, dil) in enumerate(DIL_PAIRS):
        n = tile // dil
        for cls in range(dil):
            for u in range(n // sub):
                q = _class_rows(q_ref, dil, cls, u * sub, sub) * ATTN_SCALE
                if u == 0:
                    kk = jnp.concatenate([_class_rows(kp_ref, dil, cls, n - sub, sub),
                                          _class_rows(kc_ref, dil, cls, 0, sub)], axis=0)
                    vv = jnp.concatenate([_class_rows(vp_ref, dil, cls, n - sub, sub),
                                          _class_rows(vc_ref, dil, cls, 0, sub)], axis=0)
                    mask = band_first
                else:
                    kk = _class_rows(kc_ref, dil, cls, (u - 1) * sub, 2 * sub)
                    vv = _class_rows(vc_ref, dil, cls, (u - 1) * sub, 2 * sub)
                    mask = band
                kk = kk.astype(BF16)
                vv = vv.astype(BF16)
                outs, lses = [], []
                for h in range(HEADS_PER_VREG):
                    qh = jnp.where(is_h0, q, 0.0) if h == 0 else jnp.where(is_h0, 0.0, q)
                    s = lax.dot_general(qh.astype(BF16), kk, NT_DIMS, preferred_element_type=F32)
                    s = jnp.where(mask, s, -jnp.inf)
                    m = jnp.max(s, axis=-1, keepdims=True)
                    p = jnp.exp(s - m)
                    l = jnp.sum(p, axis=-1, keepdims=True)
                    outs.append(jnp.dot(p.astype(BF16), vv, preferred_element_type=F32) / l)
                    lses.append(m + jnp.log(l))
                o_blk = jnp.where(is_h0, outs[0], outs[1])
                e_blk = jnp.where(is_h0, lses[0], lses[1])
                if dil == 1:
                    ob_ref[bi, u * sub:(u + 1) * sub, :] = o_blk
                    eb_ref[bi, u * sub:(u + 1) * sub, :] = e_blk
                else:
                    ob_ref[bi, pl.ds(cls + dil * u * sub, sub, stride=dil), :] = o_blk
                    eb_ref[bi, pl.ds(cls + dil * u * sub, sub, stride=dil), :] = e_blk
    n_br = len(DIL_PAIRS)
    es = [eb_ref[bi] for bi in range(n_br)]
    m = functools.reduce(jnp.maximum, es)
    ws = [jnp.exp(e - m) for e in es]
    o_ref[0] = sum(w * ob_ref[bi] for bi, w in enumerate(ws)) / sum(ws)


def _dil_prompt(qk, v, tile):
    B, S, W = v.shape
    n_pairs = W // LANES
    assert all(tile % (dil * DIL_KEYS) == 0 and win <= tile for win, dil in DIL_PAIRS) and S % tile == 0

    def cur(offset):
        return pl.BlockSpec((1, tile, LANES), lambda b, p, t: (b, t, offset + p))

    def prev(offset):
        return pl.BlockSpec((1, tile, LANES), lambda b, p, t: (b, jnp.maximum(t - 1, 0), offset + p))

    scratch = pltpu.VMEM((len(DIL_PAIRS), tile, LANES), F32)
    return pl.pallas_call(
        _dil_kernel, grid=(B, n_pairs, S // tile),
        in_specs=[cur(0), prev(n_pairs), cur(n_pairs), prev(0), cur(0)],
        out_specs=cur(0), out_shape=jax.ShapeDtypeStruct((B, S, W), F32), scratch_shapes=[scratch, scratch],
        compiler_params=_params(3), name="dil_prompt")(qk, qk, qk, v, v)


def _mem_sample_kernel(q_ref, k_ref, v_ref, o_ref):
    q = q_ref[0] * MEM_SCALE
    s = jnp.sum(k_ref[...] * q[None], axis=-1, keepdims=True)
    p = jnp.exp(s - jnp.max(s, axis=0, keepdims=True))
    o_ref[0] = jnp.sum(p * v_ref[...], axis=0) / jnp.sum(p, axis=0)


def _mem_sample(q, layer, cache_k, cache_v):
    NS, D = q.shape
    _, _, n_mem, H, Dh = cache_k.shape
    vec = pl.BlockSpec((1, H, Dh), lambda n: (n, 0, 0))
    blk = pl.BlockSpec((None, None, n_mem, H, Dh), lambda n: (layer, n, 0, 0, 0))
    out = pl.pallas_call(
        _mem_sample_kernel, grid=(NS,), in_specs=[vec, blk, blk], out_specs=vec,
        out_shape=jax.ShapeDtypeStruct((NS, H, Dh), F32),
        compiler_params=_params(1), name="mem_sample")(q.reshape(NS, H, Dh), cache_k, cache_v)
    return out.reshape(NS, D)


def _head_columns(x, n_heads):
    return x.reshape(x.shape[0], n_heads, HEAD_DIM, 1)


def _as_page(cols, page):
    pad = [(0, 0)] * (cols.ndim - 1) + [(0, page - 1)]
    return jnp.pad(cols, pad)


def _page_scores(kt, q_cols):
    return jnp.concatenate([jnp.sum(kt[h] * q_cols[h], axis=0, keepdims=True) for h in range(len(q_cols))], axis=0)


def _fox_sample_kernel(pt_ref, q_ref, *refs, pp):
    k_refs, v_refs = refs[:pp + 1], refs[pp + 1:2 * pp + 2]
    lf_pool_ref, lf_new_ref, o_ref, m_ref, l_ref, acc_ref, c_ref = refs[2 * pp + 2:]
    n = pl.program_id(0)
    g = pl.program_id(1)
    H = q_ref.shape[1]
    R = k_refs[0].shape[-1]

    @pl.when(g == 0)
    def _():
        m_ref[...] = jnp.full_like(m_ref, -jnp.inf)
        l_ref[...] = jnp.zeros_like(l_ref)
        acc_ref[...] = jnp.zeros_like(acc_ref)
        c_ref[...] = jnp.zeros_like(c_ref)

    q_cols = [q_ref[0, h] * ATTN_SCALE for h in range(H)]
    row = lax.broadcasted_iota(jnp.int32, (R, R), 0)
    col = lax.broadcasted_iota(jnp.int32, (R, R), 1)
    upper = jnp.where(row <= col, 1.0, 0.0).astype(BF16)

    def update(ks, vs, lfs, valid):
        lf_all = jnp.concatenate(lfs, axis=0)
        pieces = jnp.concatenate(_split3(lf_all), axis=0).astype(BF16)
        sums = jnp.dot(pieces, upper, preferred_element_type=F32)
        n_rows = lf_all.shape[0]
        within = sums[0:n_rows] + sums[n_rows:2 * n_rows] + sums[2 * n_rows:3 * n_rows]
        c_run = c_ref[:, 0:1]
        ss = []
        for i, k_ref in enumerate(ks):
            c = within[i * H:(i + 1) * H] + c_run
            c_run = c[:, R - 1:R]
            s = _page_scores(k_ref[...], q_cols) - c
            ss.append(s if valid is None else jnp.where(valid, s, -jnp.inf))
        m_old = m_ref[:, 0:1]
        m_new = functools.reduce(jnp.maximum, [jnp.max(s, axis=1, keepdims=True) for s in ss], m_old)
        alpha = jnp.exp(m_old - m_new)
        ps = [jnp.exp(s - m_new) for s in ss]
        m_ref[...] = jnp.broadcast_to(m_new, m_ref.shape)
        c_ref[...] = jnp.broadcast_to(c_run, c_ref.shape)
        for h in range(H):
            l_ref[h:h + 1, :] = alpha[h:h + 1, :] * l_ref[h:h + 1, :] + sum(p[h:h + 1, :] for p in ps)
            acc_ref[h] = alpha[h:h + 1, :] * acc_ref[h] + sum(v_ref[h] * p[h:h + 1, :] for v_ref, p in zip(vs, ps))

    update(k_refs[:pp], v_refs[:pp], [lf_pool_ref[pt_ref[n, g * pp + i]] for i in range(pp)], None)

    @pl.when(g == pl.num_programs(1) - 1)
    def _():
        update(k_refs[pp:], v_refs[pp:], [lf_new_ref[...]], lax.broadcasted_iota(jnp.int32, (H, R), 1) < 1)
        for h in range(H):
            o_ref[0, h] = (jnp.sum(acc_ref[h], axis=1, keepdims=True)
                           / jnp.sum(l_ref[h:h + 1, :], axis=1, keepdims=True))


def _fox_sample(q, k_new, v_new, lf_new, page_table, layer, pool_kt, pool_vt, pool_lft, pp):
    NS, W = q.shape
    n_pages = page_table.shape[1]
    _, _, H, _, page = pool_kt.shape

    def paged(tail, i):
        zeros = (0,) * len(tail)
        return pl.BlockSpec((None, None) + tail, lambda n, g, pt, i=i: (layer, pt[n, g * pp + i]) + zeros)

    kv_new = pl.BlockSpec((None, H, HEAD_DIM, page), lambda n, g, pt: (n, 0, 0, 0))
    lf_new_spec = pl.BlockSpec((None, H, page), lambda n, g, pt: (n, 0, 0))
    cols = pl.BlockSpec((1, H, HEAD_DIM, 1), lambda n, g, pt: (n, 0, 0, 0))
    in_specs = [cols]
    in_specs += [paged((H, HEAD_DIM, page), i) for i in range(pp)] + [kv_new]
    in_specs += [paged((H, HEAD_DIM, page), i) for i in range(pp)] + [kv_new]
    in_specs += [pl.BlockSpec((None,) + pool_lft.shape[1:], lambda n, g, pt: (layer, 0, 0, 0),
                              pipeline_mode=pl.Buffered(1)), lf_new_spec]
    stat = pltpu.VMEM((H, page), F32)
    out = pl.pallas_call(
        functools.partial(_fox_sample_kernel, pp=pp),
        grid_spec=pltpu.PrefetchScalarGridSpec(
            num_scalar_prefetch=1, grid=(NS, n_pages // pp), in_specs=in_specs, out_specs=cols,
            scratch_shapes=[stat, stat, pltpu.VMEM((H, HEAD_DIM, page), F32), stat]),
        out_shape=jax.ShapeDtypeStruct((NS, H, HEAD_DIM, 1), F32),
        compiler_params=_params(2), name="fox_sample")(
            page_table, _head_columns(q, H),
            *([pool_kt] * pp), _as_page(_head_columns(k_new, H), page),
            *([pool_vt] * pp), _as_page(_head_columns(v_new, H), page),
            pool_lft, _as_page(lf_new.reshape(NS, H, 1), page))
    return out.reshape(NS, W)


def _moba_sample_kernel(pt_ref, q_ref, kn_ref, vn_ref, *refs, bps, n_blocks):
    del pt_ref
    k_refs, v_refs = refs[:2 * bps], refs[2 * bps:4 * bps]
    o_ref, m_ref, l_ref, gate_ref, acc_ref = refs[4 * bps:]
    g = pl.program_id(1)
    H = q_ref.shape[1]
    R = k_refs[0].shape[-1]
    lane = lax.broadcasted_iota(jnp.int32, (H, R), 1)
    lane_d = lax.broadcasted_iota(jnp.int32, (HEAD_DIM, R), 1)

    @pl.when(g == 0)
    def _():
        m_ref[...] = jnp.zeros_like(m_ref)
        l_ref[...] = jnp.zeros_like(l_ref)
        gate_ref[...] = jnp.zeros_like(gate_ref)
        acc_ref[...] = jnp.zeros_like(acc_ref)

    q_cols = [q_ref[0, h] * ATTN_SCALE for h in range(H)]
    ms, ls, gates, probs = m_ref[...], l_ref[...], gate_ref[...], []
    for i in range(bps):
        sa = _page_scores(k_refs[2 * i][...], q_cols)
        sb = _page_scores(k_refs[2 * i + 1][...], q_cols)
        m = jnp.maximum(jnp.max(sa, axis=1, keepdims=True), jnp.max(sb, axis=1, keepdims=True))
        pa, pb = jnp.exp(sa - m), jnp.exp(sb - m)
        l = jnp.sum(pa, axis=1, keepdims=True) + jnp.sum(pb, axis=1, keepdims=True)
        gate = (jnp.sum(sa, axis=1, keepdims=True) + jnp.sum(sb, axis=1, keepdims=True)) * (
            1.0 / (ATTN_SCALE * MOBA_BLOCK))
        hit = lane == g * bps + i
        ms, ls, gates = jnp.where(hit, m, ms), jnp.where(hit, l, ls), jnp.where(hit, gate, gates)
        probs.append((pa, pb))
    m_ref[...], l_ref[...], gate_ref[...] = ms, ls, gates
    for h in range(H):
        acc = acc_ref[h]
        for i, (pa, pb) in enumerate(probs):
            pv = v_refs[2 * i][h] * pa[h:h + 1, :] + v_refs[2 * i + 1][h] * pb[h:h + 1, :]
            acc = jnp.where(lane_d == g * bps + i, jnp.sum(pv, axis=1, keepdims=True), acc)
        acc_ref[h] = acc

    @pl.when(g == pl.num_programs(1) - 1)
    def _():
        lane_f = lane.astype(F32)
        valid = lane < n_blocks
        gates = jnp.where(valid, gate_ref[...], -jnp.inf)
        sel = jnp.zeros((H, R), jnp.bool_)
        for _ in range(MOBA_TOPK):
            mx = jnp.max(gates, axis=1, keepdims=True)
            first = jnp.min(jnp.where(gates == mx, lane_f, float(R)), axis=1, keepdims=True)
            pick = (lane_f == first) & valid
            sel = sel | pick
            gates = jnp.where(pick, -jnp.inf, gates)
        s_new = _page_scores(kn_ref[...], q_cols)[:, 0:1]
        ms = m_ref[...]
        m_fin = jnp.maximum(jnp.max(jnp.where(sel, ms, -jnp.inf), axis=1, keepdims=True), s_new)
        w = jnp.where(sel, jnp.exp(ms - m_fin), 0.0)
        p_new = jnp.exp(s_new - m_fin)
        l = jnp.sum(w * l_ref[...], axis=1, keepdims=True) + p_new
        for h in range(H):
            num = jnp.sum(acc_ref[h] * w[h:h + 1, :], axis=1, keepdims=True) + p_new[h:h + 1, :] * vn_ref[h][:, 0:1]
            o_ref[0, h] = num / l[h:h + 1, :]


def _moba_sample(q, k_new, v_new, page_table, layer, pool_kt, pool_vt, bps):
    NS, W = q.shape
    n_pages = page_table.shape[1]
    _, _, H, _, page = pool_kt.shape
    assert MOBA_BLOCK == 2 * page and n_pages % (2 * bps) == 0
    n_blocks = n_pages // 2
    assert MOBA_TOPK <= n_blocks <= page
    pages = [pl.BlockSpec((None, None, H, HEAD_DIM, page),
                          lambda n, g, pt, i=i: (layer, pt[n, g * 2 * bps + i], 0, 0, 0)) for i in range(2 * bps)]
    kv_new = pl.BlockSpec((None, H, HEAD_DIM, page), lambda n, g, pt: (n, 0, 0, 0))
    cols = pl.BlockSpec((1, H, HEAD_DIM, 1), lambda n, g, pt: (n, 0, 0, 0))
    stat = pltpu.VMEM((H, page), F32)
    out = pl.pallas_call(
        functools.partial(_moba_sample_kernel, bps=bps, n_blocks=n_blocks),
        grid_spec=pltpu.PrefetchScalarGridSpec(
            num_scalar_prefetch=1, grid=(NS, n_blocks // bps),
            in_specs=[cols, kv_new, kv_new] + pages * 2, out_specs=cols,
            scratch_shapes=[stat, stat, stat, pltpu.VMEM((H, HEAD_DIM, page), F32)]),
        out_shape=jax.ShapeDtypeStruct((NS, H, HEAD_DIM, 1), F32),
        compiler_params=_params(2), name="moba_sample")(
            page_table, _head_columns(q, H), _as_page(_head_columns(k_new, H), page),
            _as_page(_head_columns(v_new, H), page), *([pool_kt] * (2 * bps)), *([pool_vt] * (2 * bps)))
    return out.reshape(NS, W)


def _dil_sample_kernel(q_ref, kn_ref, vn_ref, k_ref, v_ref, o_ref):
    hb, _, L = k_ref.shape
    dist = L - lax.broadcasted_iota(jnp.int32, (1, L), 1)
    mult = jnp.zeros((1, L), F32)
    for win, dil in DIL_PAIRS:
        mult = mult + jnp.where(((dist & (dil - 1)) == 0) & (dist <= win), 1.0, 0.0)
    valid = mult > 0.0
    n_br = float(len(DIL_PAIRS))
    for h in range(hb):
        qc = q_ref[0, h] * ATTN_SCALE
        s = jnp.sum(k_ref[h] * qc, axis=0, keepdims=True)
        s_new = jnp.sum(kn_ref[0, h] * qc, axis=0, keepdims=True)
        m = jnp.maximum(jnp.max(jnp.where(valid, s, -jnp.inf), axis=1, keepdims=True), s_new)
        p = jnp.where(valid, jnp.exp(s - m), 0.0) * mult
        p_new = n_br * jnp.exp(s_new - m)
        l = jnp.sum(p, axis=1, keepdims=True) + p_new
        o_ref[0, h] = (jnp.sum(v_ref[h] * p, axis=1, keepdims=True) + p_new * vn_ref[0, h]) / l


def _dil_sample(q, k_new, v_new, layer, buf_kt, buf_vt, hb):
    NS, W = q.shape
    _, _, H, _, L = buf_kt.shape
    assert all(dil & (dil - 1) == 0 and win == DIL_KEYS * dil and win <= L for win, dil in DIL_PAIRS)
    cols = pl.BlockSpec((1, hb, HEAD_DIM, 1), lambda n, j: (n, j, 0, 0))
    buf = pl.BlockSpec((None, None, hb, HEAD_DIM, L), lambda n, j: (layer, n, j, 0, 0))
    out = pl.pallas_call(
        _dil_sample_kernel, grid=(NS, H // hb), in_specs=[cols, cols, cols, buf, buf], out_specs=cols,
        out_shape=jax.ShapeDtypeStruct((NS, H, HEAD_DIM, 1), F32),
        compiler_params=_params(2), name="dil_sample")(
            _head_columns(q, H), _head_columns(k_new, H), _head_columns(v_new, H), buf_kt, buf_vt)
    return out.reshape(NS, W)


TM_PROJ = 512
TM_ROW = 256
TM_FFN = 512
TQ_FLASH = 1024
TK_FLASH = 512
DIL_TILE = 2048
FOX_PAGES_PER_STEP = 8
MOBA_BLOCKS_PER_STEP = 4
DIL_HEADS_PER_STEP = 4


def _positions_last(x):
    n = x.ndim
    return jnp.transpose(x, tuple(range(n - 3)) + (n - 2, n - 1, n - 3))


def _heads_from_t(xt, n_heads):
    B, _, T = xt.shape
    return jnp.transpose(xt.reshape(B, n_heads, HEAD_DIM, T), (0, 3, 1, 2))


def kernel(x_prompt, x_sample, mem_prompt, page_table, cache_fox_k, cache_fox_v, cache_fox_logf, cache_moba_k, cache_moba_v, cache_dwin_k, cache_dwin_v, cache_mem_k, cache_mem_v, state_conv, w_in_even, b_forget, w_out_even, w_in_odd, w_out_odd, w_mem_q, w_mem_k, w_mem_v, w_mem_o, w_ffn_in, ffn_conv_w, ffn_conv_b, w_ffn_out, ln_g, ln_b):
    B, S, D = x_prompt.shape
    NS, T, _ = x_sample.shape
    assert T == 1 and D == D_MODEL
    page = cache_fox_k.shape[2]
    P = page_table.shape[1] * page
    H8 = b_forget.shape[1]
    W8 = H8 * HEAD_DIM
    H16 = D // HEAD_DIM
    n_mem = mem_prompt.shape[1]
    L_C = cache_dwin_k.shape[2]
    assert P >= L_C
    keep = min(DIL_PAIRS[-1][0], S)

    pos_p = jnp.arange(S, dtype=jnp.int32)
    rope_p = _rope_tables(pos_p)
    rope_pt = _rope_tables_t(pos_p)
    rope_s = tuple(jnp.broadcast_to(t, (NS, LANES)) for t in _rope_tables(jnp.full((1,), P, jnp.int32)))

    fox_kt, fox_vt, moba_kt, moba_vt, dwin_kt, dwin_vt = (
        _positions_last(t) for t in (cache_fox_k, cache_fox_v, cache_moba_k, cache_moba_v, cache_dwin_k, cache_dwin_v))
    fox_lft = jnp.swapaxes(cache_fox_logf, -1, -2)

    hp = x_prompt.reshape(B * S, D)
    hs = x_sample.reshape(NS, D)
    mem2 = mem_prompt.reshape(B * n_mem, D)
    out = {name: [] for name in (
        "fox_k_p", "fox_v_p", "fox_l_p", "moba_k_p", "moba_v_p", "dwin_k_p", "dwin_v_p", "mem_k_p", "mem_v_p",
        "conv_p", "fox_k_s", "fox_v_s", "fox_l_s", "moba_k_s", "moba_v_s", "dwin_k_s", "dwin_v_s", "conv_s")}

    for l in range(DEPTH):
        g, bta = ln_g[l], ln_b[l]
        if l % 2 == 0:
            e = l // 2
            w = w_in_even[e]
            cuts = [0, W8, 2 * W8, 3 * W8, 3 * W8 + H8, 4 * W8 + H8, 5 * W8 + H8, 6 * W8 + H8]
            w_qa, w_ka, w_va, w_f, w_qb, w_kb, w_vb = (w[:, cuts[i]:cuts[i + 1]] for i in range(7))
            w_fox = jnp.concatenate([w_qa, w_ka], axis=1).astype(BF16)
            w_moba = jnp.concatenate([w_qb, w_kb], axis=1).astype(BF16)
            wo = w_out_even[e].astype(BF16)
            qk_a = _proj(hp, w_fox, TM_PROJ)
            lf, c = _forget(hp, w_f, b_forget[e], TM_ROW, S)
            qk_b = _proj(hp, w_moba, TM_PROJ, rope=rope_p)
            kt_a, vt_a, kt_b, vt_b = _proj_t(hp, [t.T.astype(BF16) for t in (w_ka, w_va, w_kb, w_vb)],
                                             (False, False, True, False), rope_pt, S, S, TM_PROJ)
            oa = _fox_prompt(qk_a.reshape(B, S, 2 * W8), vt_a, c[:, :H8].reshape(B, S, H8), TQ_FLASH, TK_FLASH)
            ob = _moba_prompt(qk_b.reshape(B, S, 2 * W8), vt_b, TQ_FLASH, TK_FLASH)
            hp = _out_ln([oa.reshape(B * S, W8), ob.reshape(B * S, W8)], [wo[:W8], wo[W8:]], hp,
                         g[0], bta[0], TM_ROW)
            out["fox_k_p"].append(_heads_from_t(kt_a, H8))
            out["fox_v_p"].append(_heads_from_t(vt_a, H8))
            out["fox_l_p"].append(lf[:, :H8].reshape(B, S, H8))
            out["moba_k_p"].append(_heads_from_t(kt_b, H8))
            out["moba_v_p"].append(_heads_from_t(vt_b, H8))
            fox = _proj(hs, jnp.concatenate([w_fox, w_va.astype(BF16)], axis=1), NS)
            qa, ka, va = (fox[:, i * W8:(i + 1) * W8] for i in range(3))
            lf, _ = _forget(hs, w_f, b_forget[e], NS, NS)
            lf = lf[:, :H8]
            mqk = _proj(hs, w_moba, NS, rope=rope_s)
            qb, kb = (mqk[:, i * W8:(i + 1) * W8] for i in range(2))
            vb = _proj(hs, w_vb.astype(BF16), NS)
            oa = _fox_sample(qa, ka, va, lf, page_table, e, fox_kt, fox_vt, fox_lft, FOX_PAGES_PER_STEP)
            ob = _moba_sample(qb, kb, vb, page_table, e, moba_kt, moba_vt, MOBA_BLOCKS_PER_STEP)
            hs = _out_ln([oa, ob], [wo[:W8], wo[W8:]], hs, g[0], bta[0], NS)
            out["fox_k_s"].append(ka.reshape(NS, 1, H8, HEAD_DIM))
            out["fox_v_s"].append(va.reshape(NS, 1, H8, HEAD_DIM))
            out["fox_l_s"].append(lf.reshape(NS, 1, H8))
            out["moba_k_s"].append(kb.reshape(NS, 1, H8, HEAD_DIM))
            out["moba_v_s"].append(vb.reshape(NS, 1, H8, HEAD_DIM))
        else:
            od = l // 2
            w = w_in_odd[od]
            w_qk = w[:, :2 * D].astype(BF16)
            w_k, w_v = w[:, D:2 * D], w[:, 2 * D:]
            wo = w_out_odd[od].astype(BF16)
            qk = _proj(hp, w_qk, TM_PROJ, rope=rope_p).reshape(B, S, 2 * D)
            v = _proj(hp, w_v.astype(BF16), TM_PROJ).reshape(B, S, D)
            kt, vt = _proj_t(hp, [w_k.T.astype(BF16), w_v.T.astype(BF16)], (True, False), rope_pt, S, keep, TM_PROJ)
            o = _dil_prompt(qk, v, DIL_TILE)
            hp = _out_ln([o.reshape(B * S, D)], [wo], hp, g[0], bta[0], TM_ROW)
            out["dwin_k_p"].append(_heads_from_t(kt, H16))
            out["dwin_v_p"].append(_heads_from_t(vt, H16))
            qk = _proj(hs, w_qk, NS, rope=rope_s)
            q, k = qk[:, :D], qk[:, D:]
            v = _proj(hs, w_v.astype(BF16), NS)
            o = _dil_sample(q, k, v, od, dwin_kt, dwin_vt, DIL_HEADS_PER_STEP)
            hs = _out_ln([o], [wo], hs, g[0], bta[0], NS)
            out["dwin_k_s"].append(k.reshape(NS, 1, H16, HEAD_DIM))
            out["dwin_v_s"].append(v.reshape(NS, 1, H16, HEAD_DIM))
        wq, wk, wv, wo = (t[l].astype(BF16) for t in (w_mem_q, w_mem_k, w_mem_v, w_mem_o))
        mk = _proj(mem2, wk, TM_ROW).reshape(B, n_mem, D)
        mv = _proj(mem2, wv, TM_ROW).reshape(B, n_mem, D)
        out["mem_k_p"].append(mk.reshape(B, n_mem, N_MEM_HEADS, MEM_HEAD_DIM))
        out["mem_v_p"].append(mv.reshape(B, n_mem, N_MEM_HEADS, MEM_HEAD_DIM))
        hp = _mem_block(hp, wq, mk, mv, wo, g[1], bta[1], TM_ROW, S)
        qs_ = _proj(hs, wq, NS)
        os_ = _mem_sample(qs_, l, cache_mem_k, cache_mem_v)
        hs = _out_ln([os_], [wo], hs, g[1], bta[1], NS)
        win, wout = w_ffn_in[l].astype(BF16), w_ffn_out[l].astype(BF16)
        hp, cp = _ffn_prompt(hp, win, ffn_conv_w[l], ffn_conv_b[l], wout, g[2], bta[2], TM_FFN, S)
        st = state_conv[l]
        hs, u = _ffn_sample(hs, st[:, 0], st[:, 1], win, ffn_conv_w[l], ffn_conv_b[l], wout, g[2], bta[2])
        out["conv_p"].append(cp)
        out["conv_s"].append(jnp.stack([st[:, 1], u], axis=1))

    st = jnp.stack
    return (hp.reshape(B, S, D), hs.reshape(NS, 1, D),
            st(out["fox_k_p"]), st(out["fox_v_p"]), st(out["fox_l_p"]), st(out["moba_k_p"]), st(out["moba_v_p"]),
            st(out["dwin_k_p"]), st(out["dwin_v_p"]), st(out["mem_k_p"]), st(out["mem_v_p"]), st(out["conv_p"]),
            st(out["fox_k_s"]), st(out["fox_v_s"]), st(out["fox_l_s"]), st(out["moba_k_s"]), st(out["moba_v_s"]),
            st(out["dwin_k_s"]), st(out["dwin_v_s"]), st(out["conv_s"]))
```

```python
import functools
import math

import jax
import jax.numpy as jnp
from jax import lax
from jax.experimental import pallas as pl
from jax.experimental.pallas import tpu as pltpu

F32 = jnp.float32
BF16 = jnp.bfloat16

LANES = 128

D_MODEL = 1024
HEAD_DIM = 64
HEADS_PER_VREG = LANES // HEAD_DIM
N_MEM_HEADS = 4
MEM_HEAD_DIM = D_MODEL // N_MEM_HEADS
D_FF = 2816
FFN_CHUNK = 256
MOBA_BLOCK = 256
MOBA_BLOCK_SHIFT = 8
MOBA_TOPK = 3
DIL_PAIRS = ((128, 1), (512, 4), (2048, 16))
DIL_KEYS = 128
ROPE_THETA = 10000.0
LN_EPS = 1e-5
DEPTH = 4
DN_ALPHA = (2 * DEPTH) ** 0.25
ATTN_SCALE = HEAD_DIM ** -0.5
MEM_SCALE = MEM_HEAD_DIM ** -0.5
LOG2E = math.log2(math.e)
NEG_BIG = -1e30
VMEM_LIMIT = 56 * 2 ** 20

NT_DIMS = (((1,), (1,)), ((), ()))
HIGHEST = lax.Precision.HIGHEST


def _params(n_axes):
    return pltpu.CompilerParams(dimension_semantics=("arbitrary",) * n_axes,
                                vmem_limit_bytes=VMEM_LIMIT)


def _const_spec(shape):
    zeros = (0,) * len(shape)
    return pl.BlockSpec(shape, lambda *_: zeros, pipeline_mode=pl.Buffered(1))


def _layer_norm(y, g, b):
    mu = jnp.mean(y, axis=-1, keepdims=True)
    yc = y - mu
    var = jnp.mean(yc * yc, axis=-1, keepdims=True)
    return yc * lax.rsqrt(var + LN_EPS) * g + b


def _rope_angles(pos):
    half = HEAD_DIM // 2
    inv = ROPE_THETA ** (-jnp.arange(half, dtype=F32) / half)
    ang = pos.astype(F32)[:, None] * inv[None, :]
    return jnp.cos(ang), jnp.sin(ang)


def _rope_tables(pos):
    cos, sin = _rope_angles(pos)
    cos2 = jnp.tile(jnp.concatenate([cos, cos], axis=-1), (1, HEADS_PER_VREG))
    sin2 = jnp.tile(jnp.concatenate([-sin, sin], axis=-1), (1, HEADS_PER_VREG))
    return cos2, sin2


def _rope_tables_t(pos):
    cos, sin = _rope_angles(pos)
    return jnp.concatenate([cos, cos], axis=-1).T, jnp.concatenate([-sin, sin], axis=-1).T


def _proj_kernel(x_ref, w_ref, o_ref):
    o_ref[...] = jnp.dot(x_ref[...].astype(BF16), w_ref[...], preferred_element_type=F32)


def _proj_rope_kernel(x_ref, w_ref, cos_ref, sin_ref, o_ref):
    z = jnp.dot(x_ref[...].astype(BF16), w_ref[...], preferred_element_type=F32)
    tm, tn = z.shape
    cos, sin = cos_ref[...], sin_ref[...]
    lane = lax.broadcasted_iota(jnp.int32, (tm, LANES), 1)
    first_half = (lane & (HEAD_DIM - 1)) < HEAD_DIM // 2
    for c in range(tn // LANES):
        zc = z[:, c * LANES:(c + 1) * LANES]
        partner = jnp.where(first_half, pltpu.roll(zc, LANES - HEAD_DIM // 2, 1),
                            pltpu.roll(zc, HEAD_DIM // 2, 1))
        o_ref[:, c * LANES:(c + 1) * LANES] = zc * cos + partner * sin


def _proj(x, w, tm, rope=None):
    M, K = x.shape
    N = w.shape[1]
    x_spec = pl.BlockSpec((tm, K), lambda i: (i, 0))
    o_spec = pl.BlockSpec((tm, N), lambda i: (i, 0))
    out_shape = jax.ShapeDtypeStruct((M, N), F32)
    if rope is None:
        return pl.pallas_call(_proj_kernel, grid=(M // tm,), in_specs=[x_spec, _const_spec((K, N))],
                              out_specs=o_spec, out_shape=out_shape, compiler_params=_params(1), name="proj")(x, w)
    cos2, sin2 = rope
    pos_blocks = cos2.shape[0] // tm
    t_spec = pl.BlockSpec((tm, LANES), lambda i: (i % pos_blocks, 0))
    return pl.pallas_call(_proj_rope_kernel, grid=(M // tm,), in_specs=[x_spec, _const_spec((K, N)), t_spec, t_spec],
                          out_specs=o_spec, out_shape=out_shape, compiler_params=_params(1),
                          name="proj_rope")(x, w, cos2, sin2)


def _proj_t_kernel(*refs, ropes):
    n = len(ropes)
    x_ref, wt_refs, cos_ref, sin_ref, o_refs = refs[0], refs[1:1 + n], refs[1 + n], refs[2 + n], refs[3 + n:]
    xb = x_ref[...].astype(BF16)
    half = HEAD_DIM // 2
    for wt_ref, o_ref, rope in zip(wt_refs, o_refs, ropes):
        z = lax.dot_general(wt_ref[...], xb, NT_DIMS, preferred_element_type=F32)
        if not rope:
            o_ref[0] = z
            continue
        cos, sin = cos_ref[...], sin_ref[...]
        for g in range(z.shape[0] // HEAD_DIM):
            zg = z[g * HEAD_DIM:(g + 1) * HEAD_DIM]
            partner = jnp.concatenate([zg[half:], zg[:half]], axis=0)
            o_ref[0, g * HEAD_DIM:(g + 1) * HEAD_DIM, :] = zg * cos + partner * sin


def _proj_t(x, wts, ropes, rope_t, seq_len, keep, tm):
    M, K = x.shape
    B = M // seq_len
    tiles_in, tiles_out = seq_len // tm, keep // tm
    first = tiles_in - tiles_out
    x_spec = pl.BlockSpec((tm, K), lambda i: ((i // tiles_out) * tiles_in + first + i % tiles_out, 0))
    t_spec = pl.BlockSpec((HEAD_DIM, tm), lambda i: (0, first + i % tiles_out))
    o_specs = [pl.BlockSpec((1, wt.shape[0], tm), lambda i: (i // tiles_out, 0, i % tiles_out)) for wt in wts]
    return pl.pallas_call(
        functools.partial(_proj_t_kernel, ropes=tuple(ropes)), grid=(B * tiles_out,),
        in_specs=[x_spec] + [_const_spec(wt.shape) for wt in wts] + [t_spec, t_spec], out_specs=o_specs,
        out_shape=[jax.ShapeDtypeStruct((B, wt.shape[0], keep), F32) for wt in wts],
        compiler_params=_params(1), name="proj_t")(x, *wts, *rope_t)


def _forget_kernel(x_ref, w_ref, b_ref, lf_ref, c_ref, carry_ref, *, tiles_per_seq):
    i = pl.program_id(0)
    z = jnp.dot(x_ref[...].astype(BF16), w_ref[...], preferred_element_type=F32) + b_ref[...]
    lf = -(jnp.maximum(-z, 0.0) + jnp.log1p(jnp.exp(-jnp.abs(z))))
    lf_ref[...] = lf

    @pl.when(i % tiles_per_seq == 0)
    def _():
        carry_ref[...] = jnp.zeros_like(carry_ref)

    tm = lf.shape[0]
    row = lax.broadcasted_iota(jnp.int32, (tm, tm), 0)
    col = lax.broadcasted_iota(jnp.int32, (tm, tm), 1)
    tri = jnp.where(col <= row, 1.0, 0.0).astype(F32)
    c = jnp.dot(tri, lf, precision=HIGHEST, preferred_element_type=F32) + carry_ref[0:1, :]
    c_ref[...] = c
    carry_ref[0:1, :] = c[tm - 1:tm, :]


def _forget(x, w_f, b_f, tm, seq_len):
    M, K = x.shape
    n_heads = w_f.shape[1]
    w = jnp.zeros((K, LANES), BF16).at[:, :n_heads].set(w_f.astype(BF16))
    b = jnp.zeros((1, LANES), F32).at[0, :n_heads].set(b_f)
    kern = functools.partial(_forget_kernel, tiles_per_seq=seq_len // tm)
    out = jax.ShapeDtypeStruct((M, LANES), F32)
    return pl.pallas_call(
        kern, grid=(M // tm,),
        in_specs=[pl.BlockSpec((tm, K), lambda i: (i, 0)), _const_spec((K, LANES)), _const_spec((1, LANES))],
        out_specs=[pl.BlockSpec((tm, LANES), lambda i: (i, 0))] * 2,
        out_shape=[out, out], scratch_shapes=[pltpu.VMEM((8, LANES), F32)],
        compiler_params=_params(1), name="forget")(x, w, b)


def _out_ln_kernel(*refs, n_in):
    a_refs, w_refs = refs[:n_in], refs[n_in:2 * n_in]
    x_ref, g_ref, b_ref, o_ref = refs[2 * n_in:]
    acc = None
    for a_ref, w_ref in zip(a_refs, w_refs):
        d = jnp.dot(a_ref[...].astype(BF16), w_ref[...], preferred_element_type=F32)
        acc = d if acc is None else acc + d
    o_ref[...] = _layer_norm(DN_ALPHA * x_ref[...] + acc, g_ref[...], b_ref[...])


def _out_ln(a_list, w_list, x, g, b, tm):
    M, D = x.shape
    n_in = len(a_list)
    in_specs = [pl.BlockSpec((tm, a.shape[1]), lambda i: (i, 0)) for a in a_list]
    in_specs += [_const_spec(w.shape) for w in w_list]
    in_specs += [pl.BlockSpec((tm, D), lambda i: (i, 0)), _const_spec((1, D)), _const_spec((1, D))]
    return pl.pallas_call(
        functools.partial(_out_ln_kernel, n_in=n_in), grid=(M // tm,), in_specs=in_specs,
        out_specs=pl.BlockSpec((tm, D), lambda i: (i, 0)), out_shape=jax.ShapeDtypeStruct((M, D), F32),
        compiler_params=_params(1), name="out_ln")(*a_list, *w_list, x, g.reshape(1, D), b.reshape(1, D))


def _mem_kernel(x_ref, wq_ref, mk_ref, mv_ref, wo_ref, g_ref, b_ref, o_ref):
    x = x_ref[...]
    q = jnp.dot(x.astype(BF16), wq_ref[...], preferred_element_type=F32)
    mk = mk_ref[0].astype(BF16)
    mv = mv_ref[0].astype(BF16)
    heads = []
    for h in range(N_MEM_HEADS):
        sl = slice(h * MEM_HEAD_DIM, (h + 1) * MEM_HEAD_DIM)
        qh = (q[:, sl] * MEM_SCALE).astype(BF16)
        s = lax.dot_general(qh, mk[:, sl], NT_DIMS, preferred_element_type=F32)
        e = jnp.exp(s - jnp.max(s, axis=-1, keepdims=True))
        l = jnp.sum(e, axis=-1, keepdims=True)
        heads.append(jnp.dot(e.astype(BF16), mv[:, sl], preferred_element_type=F32) / l)
    o = jnp.concatenate(heads, axis=-1)
    mix = jnp.dot(o.astype(BF16), wo_ref[...], preferred_element_type=F32)
    o_ref[...] = _layer_norm(DN_ALPHA * x + mix, g_ref[...], b_ref[...])


def _mem_block(x, wq, mk, mv, wo, g, b, tm, seq_len):
    M, D = x.shape
    n_mem = mk.shape[1]
    tiles_per_seq = seq_len // tm
    row = pl.BlockSpec((tm, D), lambda i: (i, 0))
    mem = pl.BlockSpec((1, n_mem, D), lambda i: (i // tiles_per_seq, 0, 0))
    return pl.pallas_call(
        _mem_kernel, grid=(M // tm,),
        in_specs=[row, _const_spec((D, D)), mem, mem, _const_spec((D, D)),
                  _const_spec((1, D)), _const_spec((1, D))],
        out_specs=row, out_shape=jax.ShapeDtypeStruct((M, D), F32),
        compiler_params=_params(1), name="mem_block")(x, wq, mk, mv, wo, g.reshape(1, D), b.reshape(1, D))


def _gated(ha, hg):
    return ha * (0.5 * hg * (1.0 + lax.erf(hg * (2.0 ** -0.5))))


def _ffn_prompt_kernel(x_ref, win_ref, cw_ref, cb_ref, wout_ref, g_ref, b_ref, o_ref, st_ref, carry_ref, act_ref,
                       *, tiles_per_seq):
    i = pl.program_id(0)

    @pl.when(i % tiles_per_seq == 0)
    def _():
        carry_ref[...] = jnp.zeros_like(carry_ref)

    x = x_ref[...]
    xb = x.astype(BF16)
    tm = x.shape[0]
    row = lax.broadcasted_iota(jnp.int32, (tm, FFN_CHUNK), 0)
    for c in range(D_FF // FFN_CHUNK):
        hs = []
        for off in (c * FFN_CHUNK, D_FF + c * FFN_CHUNK):
            sl = slice(off, off + FFN_CHUNK)
            u = jnp.dot(xb, win_ref[:, sl], preferred_element_type=F32)
            prev2 = carry_ref[0:1, sl]
            prev1 = carry_ref[1:2, sl]
            u1 = jnp.where(row == 0, prev1, pltpu.roll(u, 1, 0))
            u2 = jnp.where(row == 0, prev2, jnp.where(row == 1, prev1, pltpu.roll(u, 2, 0)))
            last = u[tm - 2:tm, :]
            carry_ref[0:2, sl] = last
            st_ref[0, :, sl] = last
            hs.append(cb_ref[:, sl] + cw_ref[0:1, sl] * u2 + cw_ref[1:2, sl] * u1 + cw_ref[2:3, sl] * u)
        act_ref[:, c * FFN_CHUNK:(c + 1) * FFN_CHUNK] = _gated(hs[0], hs[1]).astype(BF16)
    mix = jnp.dot(act_ref[...], wout_ref[...], preferred_element_type=F32)
    o_ref[...] = _layer_norm(DN_ALPHA * x + mix, g_ref[...], b_ref[...])


def _ffn_prompt(x, win, cw, cb, wout, g, b, tm, seq_len):
    M, D = x.shape
    F2 = win.shape[1]
    tiles_per_seq = seq_len // tm
    row = pl.BlockSpec((tm, D), lambda i: (i, 0))
    return pl.pallas_call(
        functools.partial(_ffn_prompt_kernel, tiles_per_seq=tiles_per_seq), grid=(M // tm,),
        in_specs=[row, _const_spec((D, F2)), _const_spec((3, F2)), _const_spec((1, F2)),
                  _const_spec((F2 // 2, D)), _const_spec((1, D)), _const_spec((1, D))],
        out_specs=[row, pl.BlockSpec((1, 2, F2), lambda i: (i // tiles_per_seq, 0, 0))],
        out_shape=[jax.ShapeDtypeStruct((M, D), F32), jax.ShapeDtypeStruct((M // seq_len, 2, F2), F32)],
        scratch_shapes=[pltpu.VMEM((8, F2), F32), pltpu.VMEM((tm, F2 // 2), BF16)],
        compiler_params=_params(1), name="ffn_prompt")(x, win, cw, cb.reshape(1, F2), wout,
                                                       g.reshape(1, D), b.reshape(1, D))


def _ffn_sample_kernel(x_ref, s0_ref, s1_ref, win_ref, cw_ref, cb_ref, wout_ref, g_ref, b_ref, o_ref, u_ref):
    x = x_ref[...]
    xb = x.astype(BF16)
    acc = jnp.zeros(x.shape, F32)
    for c in range(D_FF // FFN_CHUNK):
        hs = []
        for off in (c * FFN_CHUNK, D_FF + c * FFN_CHUNK):
            sl = slice(off, off + FFN_CHUNK)
            u = jnp.dot(xb, win_ref[:, sl], preferred_element_type=F32)
            u_ref[:, sl] = u
            hs.append(cb_ref[:, sl] + cw_ref[0:1, sl] * s0_ref[:, sl] + cw_ref[1:2, sl] * s1_ref[:, sl]
                      + cw_ref[2:3, sl] * u)
        act = _gated(hs[0], hs[1])
        acc = acc + jnp.dot(act.astype(BF16), wout_ref[c * FFN_CHUNK:(c + 1) * FFN_CHUNK, :],
                            preferred_element_type=F32)
    o_ref[...] = _layer_norm(DN_ALPHA * x + acc, g_ref[...], b_ref[...])


def _ffn_sample(x, s0, s1, win, cw, cb, wout, g, b):
    M, D = x.shape
    F2 = win.shape[1]
    return pl.pallas_call(
        _ffn_sample_kernel, grid=(1,),
        in_specs=[_const_spec((M, D)), _const_spec((M, F2)), _const_spec((M, F2)), _const_spec((D, F2)),
                  _const_spec((3, F2)), _const_spec((1, F2)), _const_spec((F2 // 2, D)),
                  _const_spec((1, D)), _const_spec((1, D))],
        out_specs=[pl.BlockSpec((M, D), lambda i: (0, 0)), pl.BlockSpec((M, F2), lambda i: (0, 0))],
        out_shape=[jax.ShapeDtypeStruct((M, D), F32), jax.ShapeDtypeStruct((M, F2), F32)],
        compiler_params=_params(1), name="ffn_sample")(x, s0, s1, win, cw, cb.reshape(1, F2), wout,
                                                       g.reshape(1, D), b.reshape(1, D))


def _lane_masks(rows):
    lane = lax.broadcasted_iota(jnp.int32, (rows, LANES), 1)
    is_h0 = lane < HEAD_DIM
    return lane, (is_h0, jnp.logical_not(is_h0))


def _flash_core(ka_refs, vb_ref, s_ref, q_aug, qi, tq, tk):
    assert tq == 2 * tk
    qs = qi * tq

    def scores(ks, slot):
        for h in range(HEADS_PER_VREG):
            s_ref[slot, h] = lax.dot_general(ka_refs[h][pl.ds(ks, tk), :], q_aug[h], NT_DIMS,
                                             preferred_element_type=F32)

    def consume(ks, slot, states, mask):
        new = []
        for h in range(HEADS_PER_VREG):
            m, l, acc = states[h]
            s = s_ref[slot, h]
            if mask is not None:
                s = jnp.where(mask, s, -jnp.inf)
            m_new = jnp.maximum(m, jnp.max(s, axis=0, keepdims=True))
            alpha = jnp.exp2(m - m_new)
            p = jnp.exp2(s - m_new)
            l = alpha * l + jnp.sum(p, axis=0, keepdims=True)
            vt = vb_ref[h * HEAD_DIM:(h + 1) * HEAD_DIM, pl.ds(ks, tk)]
            acc = alpha * acc + jnp.dot(vt, p.astype(BF16), preferred_element_type=F32)
            new.append((m_new, l, acc))
        return tuple(new)

    def chunk(j):
        return pl.multiple_of(j * tk, tk)

    def pair(jj, states):
        scores(chunk(2 * jj + 1), 1)
        states = consume(chunk(2 * jj), 0, states, None)
        scores(chunk(2 * jj + 2), 0)
        return consume(chunk(2 * jj + 1), 1, states, None)

    init = (jnp.full((1, tq), -jnp.inf, F32), jnp.zeros((1, tq), F32), jnp.zeros((HEAD_DIM, tq), F32))
    scores(0, 0)
    states = lax.fori_loop(0, qi, pair, (init, init))
    kpos = lax.broadcasted_iota(jnp.int32, (tk, tq), 0)
    qpos = lax.broadcasted_iota(jnp.int32, (tk, tq), 1)
    scores(chunk(2 * qi + 1), 1)
    states = consume(chunk(2 * qi), 0, states, kpos <= qpos)
    states = consume(chunk(2 * qi + 1), 1, states, kpos + tk <= qpos)
    (_, l0, a0), (_, l1, a1) = states
    return jnp.concatenate([a0 / l0, a1 / l1], axis=0).T


def _split3(x):
    hi = x.astype(BF16).astype(F32)
    r = x - hi
    mid = r.astype(BF16).astype(F32)
    lo = (r - mid).astype(BF16).astype(F32)
    return hi, mid, lo


def _place(lane, base, cols):
    out = jnp.zeros(lane.shape, F32)
    for i, v in enumerate(cols):
        out = jnp.where(lane == base + i, v, out)
    return out


def _fox_kernel(q_ref, k_ref, vt_ref, cc_ref, o_ref, ka0_ref, ka1_ref, vb_ref, s_ref, *, tq, tk):
    qi = pl.program_id(2)
    S = k_ref.shape[1]
    fill = MOBA_BLOCK
    ka_refs = (ka0_ref, ka1_ref)

    @pl.when(qi == 0)
    def _():
        vb_ref[...] = vt_ref[0].astype(BF16)
        lane, in_head = _lane_masks(fill)

        def body(j, _):
            rs = pl.multiple_of(j * fill, fill)
            kc = k_ref[0, pl.ds(rs, fill), :]
            cc = cc_ref[0, 0, pl.ds(rs, fill), :] * LOG2E
            for h in range(HEADS_PER_VREG):
                hi, mid, lo = _split3(cc[:, h:h + 1])
                one = jnp.ones_like(hi)
                extra = _place(lane, (1 - h) * HEAD_DIM, (-hi, -mid, -lo, one, one, one))
                ka_refs[h][pl.ds(rs, fill), :] = jnp.where(in_head[h], kc, extra).astype(BF16)
            return 0

        lax.fori_loop(0, S // fill, body, 0)

    lane, in_head = _lane_masks(tq)
    q = q_ref[0] * (ATTN_SCALE * LOG2E)
    cq = cc_ref[0, 0, pl.ds(pl.multiple_of(qi * tq, tq), tq), :] * LOG2E
    q_aug = []
    for h in range(HEADS_PER_VREG):
        hi, mid, lo = _split3(cq[:, h:h + 1])
        one = jnp.ones_like(hi)
        extra = _place(lane, (1 - h) * HEAD_DIM, (one, one, one, hi, mid, lo))
        q_aug.append(jnp.where(in_head[h], q, extra).astype(BF16))
    o_ref[0] = _flash_core(ka_refs, vb_ref, s_ref, q_aug, qi, tq, tk)


def _fox_prompt(qk, vt, c, tq, tk):
    B, S, W2 = qk.shape
    W = W2 // 2
    n_pairs = W // LANES
    cc = c.reshape(B, S, n_pairs, HEADS_PER_VREG).transpose(0, 2, 1, 3)
    one = pl.Buffered(1)
    return pl.pallas_call(
        functools.partial(_fox_kernel, tq=tq, tk=tk), grid=(B, n_pairs, S // tq),
        in_specs=[pl.BlockSpec((1, tq, LANES), lambda b, p, i: (b, i, p)),
                  pl.BlockSpec((1, S, LANES), lambda b, p, i: (b, 0, n_pairs + p), pipeline_mode=one),
                  pl.BlockSpec((1, LANES, S), lambda b, p, i: (b, p, 0), pipeline_mode=one),
                  pl.BlockSpec((1, 1, S, HEADS_PER_VREG), lambda b, p, i: (b, p, 0, 0), pipeline_mode=one)],
        out_specs=pl.BlockSpec((1, tq, LANES), lambda b, p, i: (b, i, p)),
        out_shape=jax.ShapeDtypeStruct((B, S, W), F32),
        scratch_shapes=[pltpu.VMEM((S, LANES), BF16), pltpu.VMEM((S, LANES), BF16), pltpu.VMEM((LANES, S), BF16),
                        pltpu.VMEM((2, HEADS_PER_VREG, tk, tq), F32)],
        compiler_params=_params(3), name="fox_prompt")(qk, qk, vt, cc)


def _moba_kernel(q_ref, k_ref, vt_ref, o_ref, ka0_ref, ka1_ref, vb_ref, kmp_ref, s_ref, *, tq, tk):
    qi = pl.program_id(2)
    S = k_ref.shape[1]
    blk = MOBA_BLOCK
    ka_refs = (ka0_ref, ka1_ref)

    @pl.when(qi == 0)
    def _():
        vb_ref[...] = vt_ref[0].astype(BF16)
        kmp_ref[...] = jnp.zeros_like(kmp_ref)
        lane, in_head = _lane_masks(blk)
        blk_lane = lane & (HEAD_DIM - 1)

        def body(j, _):
            rs = pl.multiple_of(j * blk, blk)
            kc = k_ref[0, pl.ds(rs, blk), :]
            onehot = jnp.where(blk_lane == j, 1.0, 0.0)
            for h in range(HEADS_PER_VREG):
                ka_refs[h][pl.ds(rs, blk), :] = jnp.where(in_head[h], kc, onehot).astype(BF16)
            kmean = jnp.sum(kc, axis=0, keepdims=True) * (1.0 / blk)
            h0row = in_head[0][0:1, :]
            kmp_ref[pl.ds(HEAD_DIM + j, 1), :] = jnp.where(h0row, kmean, 0.0)
            kmp_ref[pl.ds(j, 1), :] = jnp.where(h0row, 0.0, kmean)
            return 0

        lax.fori_loop(0, S // blk, body, 0)

    _, in_head = _lane_masks(tq)
    q = q_ref[0]
    gate_t = lax.dot_general(kmp_ref[...], q, NT_DIMS, precision=HIGHEST, preferred_element_type=F32)
    blk = lax.broadcasted_iota(jnp.int32, (HEAD_DIM, tq), 0)
    blk_f = blk.astype(F32)
    qpos = qi * tq + lax.broadcasted_iota(jnp.int32, (HEAD_DIM, tq), 1)
    own = lax.shift_right_logical(qpos, MOBA_BLOCK_SHIFT)
    valid = blk < own
    halves = []
    for rows in range(HEADS_PER_VREG):
        g = jnp.where(valid, gate_t[rows * HEAD_DIM:(rows + 1) * HEAD_DIM], -jnp.inf)
        sel = blk == own
        for _ in range(MOBA_TOPK):
            mx = jnp.max(g, axis=0, keepdims=True)
            first = jnp.min(jnp.where(g == mx, blk_f, float(HEAD_DIM)), axis=0, keepdims=True)
            pick = (blk_f == first) & valid
            sel = sel | pick
            g = jnp.where(pick, -jnp.inf, g)
        halves.append(jnp.where(sel, 0.0, NEG_BIG))
    bias = jnp.concatenate(halves, axis=0).T
    qsc = q * (ATTN_SCALE * LOG2E)
    q_aug = [jnp.where(in_head[h], qsc, bias).astype(BF16) for h in range(HEADS_PER_VREG)]
    o_ref[0] = _flash_core(ka_refs, vb_ref, s_ref, q_aug, qi, tq, tk)


def _moba_prompt(qk, vt, tq, tk):
    B, S, W2 = qk.shape
    W = W2 // 2
    n_pairs = W // LANES
    assert S % MOBA_BLOCK == 0 and S // MOBA_BLOCK <= HEAD_DIM
    assert tk % MOBA_BLOCK == 0 and tq % tk == 0
    one = pl.Buffered(1)
    return pl.pallas_call(
        functools.partial(_moba_kernel, tq=tq, tk=tk), grid=(B, n_pairs, S // tq),
        in_specs=[pl.BlockSpec((1, tq, LANES), lambda b, p, i: (b, i, p)),
                  pl.BlockSpec((1, S, LANES), lambda b, p, i: (b, 0, n_pairs + p), pipeline_mode=one),
                  pl.BlockSpec((1, LANES, S), lambda b, p, i: (b, p, 0), pipeline_mode=one)],
        out_specs=pl.BlockSpec((1, tq, LANES), lambda b, p, i: (b, i, p)),
        out_shape=jax.ShapeDtypeStruct((B, S, W), F32),
        scratch_shapes=[pltpu.VMEM((S, LANES), BF16), pltpu.VMEM((S, LANES), BF16),
                        pltpu.VMEM((LANES, S), BF16), pltpu.VMEM((LANES, LANES), F32),
                        pltpu.VMEM((2, HEADS_PER_VREG, tk, tq), F32)],
        compiler_params=_params(3), name="moba_prompt")(qk, qk, vt)


def _class_rows(ref, dil, cls, first, count):
    if dil == 1:
        return ref[0, first:first + count, :]
    return ref[0, pl.ds(cls + dil * first, count, stride=dil), :]


def _dil_kernel(q_ref, kp_ref, kc_ref, vp_ref, vc_ref, o_ref, ob_ref, eb_ref):
    t = pl.program_id(2)
    tile = q_ref.shape[1]
    sub = DIL_KEYS
    _, (is_h0, _) = _lane_masks(sub)
    a = lax.broadcasted_iota(jnp.int32, (sub, 2 * sub), 0)
    c = lax.broadcasted_iota(jnp.int32, (sub, 2 * sub), 1)
    band = (c >= a) & (c <= a + sub)
    band_first = band & (c >= jnp.where(t > 0, 0, sub))
    for bi, (_, dil) in enumerate(DIL_PAIRS):
        n = tile // dil
        for cls in range(dil):
            for u in range(n // sub):
                q = _class_rows(q_ref, dil, cls, u * sub, sub) * ATTN_SCALE
                if u == 0:
                    kk = jnp.concatenate([_class_rows(kp_ref, dil, cls, n - sub, sub),
                                          _class_rows(kc_ref, dil, cls, 0, sub)], axis=0)
                    vv = jnp.concatenate([_class_rows(vp_ref, dil, cls, n - sub, sub),
                                          _class_rows(vc_ref, dil, cls, 0, sub)], axis=0)
                    mask = band_first
                else:
                    kk = _class_rows(kc_ref, dil, cls, (u - 1) * sub, 2 * sub)
                    vv = _class_rows(vc_ref, dil, cls, (u - 1) * sub, 2 * sub)
                    mask = band
                kk = kk.astype(BF16)
                vv = vv.astype(BF16)
                outs, lses = [], []
                for h in range(HEADS_PER_VREG):
                    qh = jnp.where(is_h0, q, 0.0) if h == 0 else jnp.where(is_h0, 0.0, q)
                    s = lax.dot_general(qh.astype(BF16), kk, NT_DIMS, preferred_element_type=F32)
                    s = jnp.where(mask, s, -jnp.inf)
                    m = jnp.max(s, axis=-1, keepdims=True)
                    p = jnp.exp(s - m)
                    l = jnp.sum(p, axis=-1, keepdims=True)
                    outs.append(jnp.dot(p.astype(BF16), vv, preferred_element_type=F32) / l)
                    lses.append(m + jnp.log(l))
                o_blk = jnp.where(is_h0, outs[0], outs[1])
                e_blk = jnp.where(is_h0, lses[0], lses[1])
                if dil == 1:
                    ob_ref[bi, u * sub:(u + 1) * sub, :] = o_blk
                    eb_ref[bi, u * sub:(u + 1) * sub, :] = e_blk
                else:
                    ob_ref[bi, pl.ds(cls + dil * u * sub, sub, stride=dil), :] = o_blk
                    eb_ref[bi, pl.ds(cls + dil * u * sub, sub, stride=dil), :] = e_blk
    n_br = len(DIL_PAIRS)
    es = [eb_ref[bi] for bi in range(n_br)]
    m = functools.reduce(jnp.maximum, es)
    ws = [jnp.exp(e - m) for e in es]
    o_ref[0] = sum(w * ob_ref[bi] for bi, w in enumerate(ws)) / sum(ws)


def _dil_prompt(qk, v, tile):
    B, S, W = v.shape
    n_pairs = W // LANES
    assert all(tile % (dil * DIL_KEYS) == 0 and win <= tile for win, dil in DIL_PAIRS) and S % tile == 0

    def cur(offset):
        return pl.BlockSpec((1, tile, LANES), lambda b, p, t: (b, t, offset + p))

    def prev(offset):
        return pl.BlockSpec((1, tile, LANES), lambda b, p, t: (b, jnp.maximum(t - 1, 0), offset + p))

    scratch = pltpu.VMEM((len(DIL_PAIRS), tile, LANES), F32)
    return pl.pallas_call(
        _dil_kernel, grid=(B, n_pairs, S // tile),
        in_specs=[cur(0), prev(n_pairs), cur(n_pairs), prev(0), cur(0)],
        out_specs=cur(0), out_shape=jax.ShapeDtypeStruct((B, S, W), F32), scratch_shapes=[scratch, scratch],
        compiler_params=_params(3), name="dil_prompt")(qk, qk, qk, v, v)


def _mem_sample_kernel(q_ref, k_ref, v_ref, o_ref):
    q = q_ref[0] * MEM_SCALE
    s = jnp.sum(k_ref[...] * q[None], axis=-1, keepdims=True)
    p = jnp.exp(s - jnp.max(s, axis=0, keepdims=True))
    o_ref[0] = jnp.sum(p * v_ref[...], axis=0) / jnp.sum(p, axis=0)


def _mem_sample(q, layer, cache_k, cache_v):
    NS, D = q.shape
    _, _, n_mem, H, Dh = cache_k.shape
    vec = pl.BlockSpec((1, H, Dh), lambda n: (n, 0, 0))
    blk = pl.BlockSpec((None, None, n_mem, H, Dh), lambda n: (layer, n, 0, 0, 0))
    out = pl.pallas_call(
        _mem_sample_kernel, grid=(NS,), in_specs=[vec, blk, blk], out_specs=vec,
        out_shape=jax.ShapeDtypeStruct((NS, H, Dh), F32),
        compiler_params=_params(1), name="mem_sample")(q.reshape(NS, H, Dh), cache_k, cache_v)
    return out.reshape(NS, D)


def _head_columns(x, n_heads):
    return x.reshape(x.shape[0], n_heads, HEAD_DIM, 1)


def _as_page(cols, page):
    pad = [(0, 0)] * (cols.ndim - 1) + [(0, page - 1)]
    return jnp.pad(cols, pad)


def _page_scores(kt, q_cols):
    return jnp.concatenate([jnp.sum(kt[h] * q_cols[h], axis=0, keepdims=True) for h in range(len(q_cols))], axis=0)


def _fox_sample_kernel(pt_ref, q_ref, *refs, pp):
    k_refs, v_refs = refs[:pp + 1], refs[pp + 1:2 * pp + 2]
    lf_pool_ref, lf_new_ref, o_ref, m_ref, l_ref, acc_ref, c_ref = refs[2 * pp + 2:]
    n = pl.program_id(0)
    g = pl.program_id(1)
    H = q_ref.shape[1]
    R = k_refs[0].shape[-1]

    @pl.when(g == 0)
    def _():
        m_ref[...] = jnp.full_like(m_ref, -jnp.inf)
        l_ref[...] = jnp.zeros_like(l_ref)
        acc_ref[...] = jnp.zeros_like(acc_ref)
        c_ref[...] = jnp.zeros_like(c_ref)

    q_cols = [q_ref[0, h] * ATTN_SCALE for h in range(H)]
    row = lax.broadcasted_iota(jnp.int32, (R, R), 0)
    col = lax.broadcasted_iota(jnp.int32, (R, R), 1)
    upper = jnp.where(row <= col, 1.0, 0.0).astype(BF16)

    def update(ks, vs, lfs, valid):
        lf_all = jnp.concatenate(lfs, axis=0)
        pieces = jnp.concatenate(_split3(lf_all), axis=0).astype(BF16)
        sums = jnp.dot(pieces, upper, preferred_element_type=F32)
        n_rows = lf_all.shape[0]
        within = sums[0:n_rows] + sums[n_rows:2 * n_rows] + sums[2 * n_rows:3 * n_rows]
        c_run = c_ref[:, 0:1]
        ss = []
        for i, k_ref in enumerate(ks):
            c = within[i * H:(i + 1) * H] + c_run
            c_run = c[:, R - 1:R]
            s = _page_scores(k_ref[...], q_cols) - c
            ss.append(s if valid is None else jnp.where(valid, s, -jnp.inf))
        m_old = m_ref[:, 0:1]
        m_new = functools.reduce(jnp.maximum, [jnp.max(s, axis=1, keepdims=True) for s in ss], m_old)
        alpha = jnp.exp(m_old - m_new)
        ps = [jnp.exp(s - m_new) for s in ss]
        m_ref[...] = jnp.broadcast_to(m_new, m_ref.shape)
        c_ref[...] = jnp.broadcast_to(c_run, c_ref.shape)
        for h in range(H):
            l_ref[h:h + 1, :] = alpha[h:h + 1, :] * l_ref[h:h + 1, :] + sum(p[h:h + 1, :] for p in ps)
            acc_ref[h] = alpha[h:h + 1, :] * acc_ref[h] + sum(v_ref[h] * p[h:h + 1, :] for v_ref, p in zip(vs, ps))

    update(k_refs[:pp], v_refs[:pp], [lf_pool_ref[pt_ref[n, g * pp + i]] for i in range(pp)], None)

    @pl.when(g == pl.num_programs(1) - 1)
    def _():
        update(k_refs[pp:], v_refs[pp:], [lf_new_ref[...]], lax.broadcasted_iota(jnp.int32, (H, R), 1) < 1)
        for h in range(H):
            o_ref[0, h] = (jnp.sum(acc_ref[h], axis=1, keepdims=True)
                           / jnp.sum(l_ref[h:h + 1, :], axis=1, keepdims=True))


def _fox_sample(q, k_new, v_new, lf_new, page_table, layer, pool_kt, pool_vt, pool_lft, pp):
    NS, W = q.shape
    n_pages = page_table.shape[1]
    _, _, H, _, page = pool_kt.shape

    def paged(tail, i):
        zeros = (0,) * len(tail)
        return pl.BlockSpec((None, None) + tail, lambda n, g, pt, i=i: (layer, pt[n, g * pp + i]) + zeros)

    kv_new = pl.BlockSpec((None, H, HEAD_DIM, page), lambda n, g, pt: (n, 0, 0, 0))
    lf_new_spec = pl.BlockSpec((None, H, page), lambda n, g, pt: (n, 0, 0))
    cols = pl.BlockSpec((1, H, HEAD_DIM, 1), lambda n, g, pt: (n, 0, 0, 0))
    in_specs = [cols]
    in_specs += [paged((H, HEAD_DIM, page), i) for i in range(pp)] + [kv_new]
    in_specs += [paged((H, HEAD_DIM, page), i) for i in range(pp)] + [kv_new]
    in_specs += [pl.BlockSpec((None,) + pool_lft.shape[1:], lambda n, g, pt: (layer, 0, 0, 0),
                              pipeline_mode=pl.Buffered(1)), lf_new_spec]
    stat = pltpu.VMEM((H, page), F32)
    out = pl.pallas_call(
        functools.partial(_fox_sample_kernel, pp=pp),
        grid_spec=pltpu.PrefetchScalarGridSpec(
            num_scalar_prefetch=1, grid=(NS, n_pages // pp), in_specs=in_specs, out_specs=cols,
            scratch_shapes=[stat, stat, pltpu.VMEM((H, HEAD_DIM, page), F32), stat]),
        out_shape=jax.ShapeDtypeStruct((NS, H, HEAD_DIM, 1), F32),
        compiler_params=_params(2), name="fox_sample")(
            page_table, _head_columns(q, H),
            *([pool_kt] * pp), _as_page(_head_columns(k_new, H), page),
            *([pool_vt] * pp), _as_page(_head_columns(v_new, H), page),
            pool_lft, _as_page(lf_new.reshape(NS, H, 1), page))
    return out.reshape(NS, W)


def _moba_sample_kernel(pt_ref, q_ref, kn_ref, vn_ref, *refs, bps, n_blocks):
    del pt_ref
    k_refs, v_refs = refs[:2 * bps], refs[2 * bps:4 * bps]
    o_ref, m_ref, l_ref, gate_ref, acc_ref = refs[4 * bps:]
    g = pl.program_id(1)
    H = q_ref.shape[1]
    R = k_refs[0].shape[-1]
    lane = lax.broadcasted_iota(jnp.int32, (H, R), 1)
    lane_d = lax.broadcasted_iota(jnp.int32, (HEAD_DIM, R), 1)

    @pl.when(g == 0)
    def _():
        m_ref[...] = jnp.zeros_like(m_ref)
        l_ref[...] = jnp.zeros_like(l_ref)
        gate_ref[...] = jnp.zeros_like(gate_ref)
        acc_ref[...] = jnp.zeros_like(acc_ref)

    q_cols = [q_ref[0, h] * ATTN_SCALE for h in range(H)]
    ms, ls, gates, probs = m_ref[...], l_ref[...], gate_ref[...], []
    for i in range(bps):
        sa = _page_scores(k_refs[2 * i][...], q_cols)
        sb = _page_scores(k_refs[2 * i + 1][...], q_cols)
        m = jnp.maximum(jnp.max(sa, axis=1, keepdims=True), jnp.max(sb, axis=1, keepdims=True))
        pa, pb = jnp.exp(sa - m), jnp.exp(sb - m)
        l = jnp.sum(pa, axis=1, keepdims=True) + jnp.sum(pb, axis=1, keepdims=True)
        gate = (jnp.sum(sa, axis=1, keepdims=True) + jnp.sum(sb, axis=1, keepdims=True)) * (
            1.0 / (ATTN_SCALE * MOBA_BLOCK))
        hit = lane == g * bps + i
        ms, ls, gates = jnp.where(hit, m, ms), jnp.where(hit, l, ls), jnp.where(hit, gate, gates)
        probs.append((pa, pb))
    m_ref[...], l_ref[...], gate_ref[...] = ms, ls, gates
    for h in range(H):
        acc = acc_ref[h]
        for i, (pa, pb) in enumerate(probs):
            pv = v_refs[2 * i][h] * pa[h:h + 1, :] + v_refs[2 * i + 1][h] * pb[h:h + 1, :]
            acc = jnp.where(lane_d == g * bps + i, jnp.sum(pv, axis=1, keepdims=True), acc)
        acc_ref[h] = acc

    @pl.when(g == pl.num_programs(1) - 1)
    def _():
        lane_f = lane.astype(F32)
        valid = lane < n_blocks
        gates = jnp.where(valid, gate_ref[...], -jnp.inf)
        sel = jnp.zeros((H, R), jnp.bool_)
        for _ in range(MOBA_TOPK):
            mx = jnp.max(gates, axis=1, keepdims=True)
            first = jnp.min(jnp.where(gates == mx, lane_f, float(R)), axis=1, keepdims=True)
            pick = (lane_f == first) & valid
            sel = sel | pick
            gates = jnp.where(pick, -jnp.inf, gates)
        s_new = _page_scores(kn_ref[...], q_cols)[:, 0:1]
        ms = m_ref[...]
        m_fin = jnp.maximum(jnp.max(jnp.where(sel, ms, -jnp.inf), axis=1, keepdims=True), s_new)
        w = jnp.where(sel, jnp.exp(ms - m_fin), 0.0)
        p_new = jnp.exp(s_new - m_fin)
        l = jnp.sum(w * l_ref[...], axis=1, keepdims=True) + p_new
        for h in range(H):
            num = jnp.sum(acc_ref[h] * w[h:h + 1, :], axis=1, keepdims=True) + p_new[h:h + 1, :] * vn_ref[h][:, 0:1]
            o_ref[0, h] = num / l[h:h + 1, :]


def _moba_sample(q, k_new, v_new, page_table, layer, pool_kt, pool_vt, bps):
    NS, W = q.shape
    n_pages = page_table.shape[1]
    _, _, H, _, page = pool_kt.shape
    assert MOBA_BLOCK == 2 * page and n_pages % (2 * bps) == 0
    n_blocks = n_pages // 2
    assert MOBA_TOPK <= n_blocks <= page
    pages = [pl.BlockSpec((None, None, H, HEAD_DIM, page),
                          lambda n, g, pt, i=i: (layer, pt[n, g * 2 * bps + i], 0, 0, 0)) for i in range(2 * bps)]
    kv_new = pl.BlockSpec((None, H, HEAD_DIM, page), lambda n, g, pt: (n, 0, 0, 0))
    cols = pl.BlockSpec((1, H, HEAD_DIM, 1), lambda n, g, pt: (n, 0, 0, 0))
    stat = pltpu.VMEM((H, page), F32)
    out = pl.pallas_call(
        functools.partial(_moba_sample_kernel, bps=bps, n_blocks=n_blocks),
        grid_spec=pltpu.PrefetchScalarGridSpec(
            num_scalar_prefetch=1, grid=(NS, n_blocks // bps),
            in_specs=[cols, kv_new, kv_new] + pages * 2, out_specs=cols,
            scratch_shapes=[stat, stat, stat, pltpu.VMEM((H, HEAD_DIM, page), F32)]),
        out_shape=jax.ShapeDtypeStruct((NS, H, HEAD_DIM, 1), F32),
        compiler_params=_params(2), name="moba_sample")(
            page_table, _head_columns(q, H), _as_page(_head_columns(k_new, H), page),
            _as_page(_head_columns(v_new, H), page), *([pool_kt] * (2 * bps)), *([pool_vt] * (2 * bps)))
    return out.reshape(NS, W)


def _dil_sample_kernel(q_ref, kn_ref, vn_ref, k_ref, v_ref, o_ref):
    hb, _, L = k_ref.shape
    dist = L - lax.broadcasted_iota(jnp.int32, (1, L), 1)
    mult = jnp.zeros((1, L), F32)
    for win, dil in DIL_PAIRS:
        mult = mult + jnp.where(((dist & (dil - 1)) == 0) & (dist <= win), 1.0, 0.0)
    valid = mult > 0.0
    n_br = float(len(DIL_PAIRS))
    for h in range(hb):
        qc = q_ref[0, h] * ATTN_SCALE
        s = jnp.sum(k_ref[h] * qc, axis=0, keepdims=True)
        s_new = jnp.sum(kn_ref[0, h] * qc, axis=0, keepdims=True)
        m = jnp.maximum(jnp.max(jnp.where(valid, s, -jnp.inf), axis=1, keepdims=True), s_new)
        p = jnp.where(valid, jnp.exp(s - m), 0.0) * mult
        p_new = n_br * jnp.exp(s_new - m)
        l = jnp.sum(p, axis=1, keepdims=True) + p_new
        o_ref[0, h] = (jnp.sum(v_ref[h] * p, axis=1, keepdims=True) + p_new * vn_ref[0, h]) / l


def _dil_sample(q, k_new, v_new, layer, buf_kt, buf_vt, hb):
    NS, W = q.shape
    _, _, H, _, L = buf_kt.shape
    assert all(dil & (dil - 1) == 0 and win == DIL_KEYS * dil and win <= L for win, dil in DIL_PAIRS)
    cols = pl.BlockSpec((1, hb, HEAD_DIM, 1), lambda n, j: (n, j, 0, 0))
    buf = pl.BlockSpec((None, None, hb, HEAD_DIM, L), lambda n, j: (layer, n, j, 0, 0))
    out = pl.pallas_call(
        _dil_sample_kernel, grid=(NS, H // hb), in_specs=[cols, cols, cols, buf, buf], out_specs=cols,
        out_shape=jax.ShapeDtypeStruct((NS, H, HEAD_DIM, 1), F32),
        compiler_params=_params(2), name="dil_sample")(
            _head_columns(q, H), _head_columns(k_new, H), _head_columns(v_new, H), buf_kt, buf_vt)
    return out.reshape(NS, W)


TM_PROJ = 512
TM_ROW = 256
TM_FFN = 512
TQ_FLASH = 1024
TK_FLASH = 512
DIL_TILE = 2048
FOX_PAGES_PER_STEP = 16
MOBA_BLOCKS_PER_STEP = 8
DIL_HEADS_PER_STEP = 4


def _positions_last(x):
    n = x.ndim
    return jnp.transpose(x, tuple(range(n - 3)) + (n - 2, n - 1, n - 3))


def _heads_from_t(xt, n_heads):
    B, _, T = xt.shape
    return jnp.transpose(xt.reshape(B, n_heads, HEAD_DIM, T), (0, 3, 1, 2))


def kernel(x_prompt, x_sample, mem_prompt, page_table, cache_fox_k, cache_fox_v, cache_fox_logf, cache_moba_k, cache_moba_v, cache_dwin_k, cache_dwin_v, cache_mem_k, cache_mem_v, state_conv, w_in_even, b_forget, w_out_even, w_in_odd, w_out_odd, w_mem_q, w_mem_k, w_mem_v, w_mem_o, w_ffn_in, ffn_conv_w, ffn_conv_b, w_ffn_out, ln_g, ln_b):
    B, S, D = x_prompt.shape
    NS, T, _ = x_sample.shape
    assert T == 1 and D == D_MODEL
    page = cache_fox_k.shape[2]
    P = page_table.shape[1] * page
    H8 = b_forget.shape[1]
    W8 = H8 * HEAD_DIM
    H16 = D // HEAD_DIM
    n_mem = mem_prompt.shape[1]
    L_C = cache_dwin_k.shape[2]
    assert P >= L_C
    keep = min(DIL_PAIRS[-1][0], S)

    pos_p = jnp.arange(S, dtype=jnp.int32)
    rope_p = _rope_tables(pos_p)
    rope_pt = _rope_tables_t(pos_p)
    rope_s = tuple(jnp.broadcast_to(t, (NS, LANES)) for t in _rope_tables(jnp.full((1,), P, jnp.int32)))

    fox_kt, fox_vt, moba_kt, moba_vt, dwin_kt, dwin_vt = (
        _positions_last(t) for t in (cache_fox_k, cache_fox_v, cache_moba_k, cache_moba_v, cache_dwin_k, cache_dwin_v))
    fox_lft = jnp.swapaxes(cache_fox_logf, -1, -2)

    hp = x_prompt.reshape(B * S, D)
    hs = x_sample.reshape(NS, D)
    mem2 = mem_prompt.reshape(B * n_mem, D)
    out = {name: [] for name in (
        "fox_k_p", "fox_v_p", "fox_l_p", "moba_k_p", "moba_v_p", "dwin_k_p", "dwin_v_p", "mem_k_p", "mem_v_p",
        "conv_p", "fox_k_s", "fox_v_s", "fox_l_s", "moba_k_s", "moba_v_s", "dwin_k_s", "dwin_v_s", "conv_s")}

    for l in range(DEPTH):
        g, bta = ln_g[l], ln_b[l]
        if l % 2 == 0:
            e = l // 2
            w = w_in_even[e]
            cuts = [0, W8, 2 * W8, 3 * W8, 3 * W8 + H8, 4 * W8 + H8, 5 * W8 + H8, 6 * W8 + H8]
            w_qa, w_ka, w_va, w_f, w_qb, w_kb, w_vb = (w[:, cuts[i]:cuts[i + 1]] for i in range(7))
            w_fox = jnp.concatenate([w_qa, w_ka], axis=1).astype(BF16)
            w_moba = jnp.concatenate([w_qb, w_kb], axis=1).astype(BF16)
            wo = w_out_even[e].astype(BF16)
            qk_a = _proj(hp, w_fox, TM_PROJ)
            lf, c = _forget(hp, w_f, b_forget[e], TM_ROW, S)
            qk_b = _proj(hp, w_moba, TM_PROJ, rope=rope_p)
            kt_a, vt_a, kt_b, vt_b = _proj_t(hp, [t.T.astype(BF16) for t in (w_ka, w_va, w_kb, w_vb)],
                                             (False, False, True, False), rope_pt, S, S, TM_PROJ)
            oa = _fox_prompt(qk_a.reshape(B, S, 2 * W8), vt_a, c[:, :H8].reshape(B, S, H8), TQ_FLASH, TK_FLASH)
            ob = _moba_prompt(qk_b.reshape(B, S, 2 * W8), vt_b, TQ_FLASH, TK_FLASH)
            hp = _out_ln([oa.reshape(B * S, W8), ob.reshape(B * S, W8)], [wo[:W8], wo[W8:]], hp,
                         g[0], bta[0], TM_ROW)
            out["fox_k_p"].append(_heads_from_t(kt_a, H8))
            out["fox_v_p"].append(_heads_from_t(vt_a, H8))
            out["fox_l_p"].append(lf[:, :H8].reshape(B, S, H8))
            out["moba_k_p"].append(_heads_from_t(kt_b, H8))
            out["moba_v_p"].append(_heads_from_t(vt_b, H8))
            fox = _proj(hs, jnp.concatenate([w_fox, w_va.astype(BF16)], axis=1), NS)
            qa, ka, va = (fox[:, i * W8:(i + 1) * W8] for i in range(3))
            lf, _ = _forget(hs, w_f, b_forget[e], NS, NS)
            lf = lf[:, :H8]
            mqk = _proj(hs, w_moba, NS, rope=rope_s)
            qb, kb = (mqk[:, i * W8:(i + 1) * W8] for i in range(2))
            vb = _proj(hs, w_vb.astype(BF16), NS)
            oa = _fox_sample(qa, ka, va, lf, page_table, e, fox_kt, fox_vt, fox_lft, FOX_PAGES_PER_STEP)
            ob = _moba_sample(qb, kb, vb, page_table, e, moba_kt, moba_vt, MOBA_BLOCKS_PER_STEP)
            hs = _out_ln([oa, ob], [wo[:W8], wo[W8:]], hs, g[0], bta[0], NS)
            out["fox_k_s"].append(ka.reshape(NS, 1, H8, HEAD_DIM))
            out["fox_v_s"].append(va.reshape(NS, 1, H8, HEAD_DIM))
            out["fox_l_s"].append(lf.reshape(NS, 1, H8))
            out["moba_k_s"].append(kb.reshape(NS, 1, H8, HEAD_DIM))
            out["moba_v_s"].append(vb.reshape(NS, 1, H8, HEAD_DIM))
        else:
            od = l // 2
            w = w_in_odd[od]
            w_qk = w[:, :2 * D].astype(BF16)
            w_k, w_v = w[:, D:2 * D], w[:, 2 * D:]
            wo = w_out_odd[od].astype(BF16)
            qk = _proj(hp, w_qk, TM_PROJ, rope=rope_p).reshape(B, S, 2 * D)
            v = _proj(hp, w_v.astype(BF16), TM_PROJ).reshape(B, S, D)
            kt, vt = _proj_t(hp, [w_k.T.astype(BF16), w_v.T.astype(BF16)], (True, False), rope_pt, S, keep, TM_PROJ)
            o = _dil_prompt(qk, v, DIL_TILE)
            hp = _out_ln([o.reshape(B * S, D)], [wo], hp, g[0], bta[0], TM_ROW)
            out["dwin_k_p"].append(_heads_from_t(kt, H16))
            out["dwin_v_p"].append(_heads_from_t(vt, H16))
            qk = _proj(hs, w_qk, NS, rope=rope_s)
            q, k = qk[:, :D], qk[:, D:]
            v = _proj(hs, w_v.astype(BF16), NS)
            o = _dil_sample(q, k, v, od, dwin_kt, dwin_vt, DIL_HEADS_PER_STEP)
            hs = _out_ln([o], [wo], hs, g[0], bta[0], NS)
            out["dwin_k_s"].append(k.reshape(NS, 1, H16, HEAD_DIM))
            out["dwin_v_s"].append(v.reshape(NS, 1, H16, HEAD_DIM))
        wq, wk, wv, wo = (t[l].astype(BF16) for t in (w_mem_q, w_mem_k, w_mem_v, w_mem_o))
        mk = _proj(mem2, wk, TM_ROW).reshape(B, n_mem, D)
        mv = _proj(mem2, wv, TM_ROW).reshape(B, n_mem, D)
        out["mem_k_p"].append(mk.reshape(B, n_mem, N_MEM_HEADS, MEM_HEAD_DIM))
        out["mem_v_p"].append(mv.reshape(B, n_mem, N_MEM_HEADS, MEM_HEAD_DIM))
        hp = _mem_block(hp, wq, mk, mv, wo, g[1], bta[1], TM_ROW, S)
        qs_ = _proj(hs, wq, NS)
        os_ = _mem_sample(qs_, l, cache_mem_k, cache_mem_v)
        hs = _out_ln([os_], [wo], hs, g[1], bta[1], NS)
        win, wout = w_ffn_in[l].astype(BF16), w_ffn_out[l].astype(BF16)
        hp, cp = _ffn_prompt(hp, win, ffn_conv_w[l], ffn_conv_b[l], wout, g[2], bta[2], TM_FFN, S)
        st = state_conv[l]
        hs, u = _ffn_sample(hs, st[:, 0], st[:, 1], win, ffn_conv_w[l], ffn_conv_b[l], wout, g[2], bta[2])
        out["conv_p"].append(cp)
        out["conv_s"].append(jnp.stack([st[:, 1], u], axis=1))

    st = jnp.stack
    return (hp.reshape(B, S, D), hs.reshape(NS, 1, D),
            st(out["fox_k_p"]), st(out["fox_v_p"]), st(out["fox_l_p"]), st(out["moba_k_p"]), st(out["moba_v_p"]),
            st(out["dwin_k_p"]), st(out["dwin_v_p"]), st(out["mem_k_p"]), st(out["mem_v_p"]), st(out["conv_p"]),
            st(out["fox_k_s"]), st(out["fox_v_s"]), st(out["fox_l_s"]), st(out["moba_k_s"]), st(out["moba_v_s"]),
            st(out["dwin_k_s"]), st(out["dwin_v_s"]), st(out["conv_s"]))
```
